```python
import math
import jax
import jax.numpy as jnp
from jax import lax
import numpy as np


D_MODEL = 1024
BATCH = 8
SEQ = 2048
DEPTH = 4

GRID_W = 64
CTX_LEN = 256
Q_BLOCK = 128
ROPE_THETA = 10000.0
EPS = 1e-6
F32 = jnp.float32

DIFF_HEADS = 4
DIFF_HEAD_DIM = 48
DIFF_V_DIM = 96
GQA_Q_HEADS = 6
GQA_KV_HEADS = 2
GQA_HEAD_DIM = 64
MLSTM_HEADS = 4
MLSTM_HEAD_DIM = 64
MLSTM_CHUNK = 64
MLSTM_CONV = 3

DIFF_WIDTH = 384
GQA_WIDTH = 384
MLSTM_WIDTH = 256
MIX_WIDTH = 1024
IN_SPLITS = (384, 384, 384, 384, 128, 128, 512, 256, 256, 16)
IN_WIDTH = 2832

MOE_GROUPS = 4
MOE_EXPERTS_PER_GROUP = 8
MOE_EXPERTS = 32
MOE_TOP_K = 2
MOE_HIDDEN = 512
MOE_BLOCK = 128

kernel_name = 'hybrid_diffusion_block'


def rms_norm(x, g):
    xf = x.astype(F32)
    y = xf * lax.rsqrt(jnp.mean(xf * xf, axis=-1, keepdims=True) + EPS)
    return (y * g.astype(F32)).astype(x.dtype)


def modulate(h, shift, scale):
    return h * (1.0 + scale) + shift


def rope_1d(x, pos):
    nf = x.shape[-1] // 2
    freqs = ROPE_THETA ** (-jnp.arange(nf, dtype=F32) / nf)
    ang = pos.astype(F32)[:, None] * freqs
    ang = ang.reshape((ang.shape[0],) + (1,) * (x.ndim - 3) + (nf,))
    cos, sin = jnp.cos(ang), jnp.sin(ang)
    xf = x.astype(F32)
    x1, x2 = xf[..., :nf], xf[..., nf:]
    return jnp.concatenate([x1 * cos - x2 * sin, x2 * cos + x1 * sin], axis=-1).astype(x.dtype)


def axial_rope(x, rows, cols):
    half = x.shape[-1] // 2
    return jnp.concatenate([rope_1d(x[..., :half], rows), rope_1d(x[..., half:], cols)], axis=-1)


def to_query_blocks(q):
    nb = q.shape[-2] // Q_BLOCK
    qb = q.reshape(q.shape[:-2] + (nb, Q_BLOCK, q.shape[-1]))
    return jnp.moveaxis(qb, -3, 0)


def from_query_blocks(o):
    o = jnp.moveaxis(o, 0, -3)
    return o.reshape(o.shape[:-3] + (o.shape[-3] * o.shape[-2], o.shape[-1]))


def diff_attention(q1, q2, k1, k2, v, lam):
    scale = q1.shape[-1] ** -0.5

    def block(qs):
        a, b = qs
        p1 = jax.nn.softmax(jnp.einsum('bhqd,bhkd->bhqk', a, k1).astype(F32) * scale, axis=-1)
        p2 = jax.nn.softmax(jnp.einsum('bhqd,bhkd->bhqk', b, k2).astype(F32) * scale, axis=-1)
        return jnp.einsum('bhqk,bhkv->bhqv', (p1 - lam * p2).astype(v.dtype), v)

    return from_query_blocks(lax.map(block, (to_query_blocks(q1), to_query_blocks(q2))))


def gqa_attention(q, k, v):
    scale = q.shape[-1] ** -0.5

    def block(qb):
        p = jax.nn.softmax(jnp.einsum('bhgqd,bhkd->bhgqk', qb, k).astype(F32) * scale, axis=-1)
        return jnp.einsum('bhgqk,bhkd->bhgqd', p.astype(v.dtype), v)

    return from_query_blocks(lax.map(block, to_query_blocks(q)))


def centred_conv(x, w, b):
    pad = w.shape[0] // 2
    y = lax.conv_general_dilated(x, w[:, None, :], window_strides=(1,), padding=[(pad, pad)],
                                 dimension_numbers=('NWC', 'WIO', 'NWC'), feature_group_count=x.shape[-1])
    return y + b


def mlstm_zero_state(batch):
    return (jnp.zeros((batch, MLSTM_HEADS, MLSTM_HEAD_DIM, MLSTM_HEAD_DIM), F32),
            jnp.zeros((batch, MLSTM_HEADS, MLSTM_HEAD_DIM), F32),
            jnp.zeros((batch, MLSTM_HEADS), F32))


def mlstm_scan(q, k, v, log_i, log_f, state):
    B, H, T, _ = q.shape
    dv = v.shape[-1]
    L = MLSTM_CHUNK
    nc = T // L

    def chunks(a):
        return jnp.moveaxis(a.astype(F32).reshape((B, H, nc, L) + a.shape[3:]), 2, 0)

    tril = jnp.tril(jnp.ones((L, L), dtype=bool))

    def step(carry, inp):
        C, n, m = carry
        qc, kc, vc, li, lf = inp
        b = jnp.cumsum(lf, axis=-1)
        d_log = jnp.where(tril, b[..., :, None] - b[..., None, :] + li[..., None, :], -jnp.inf)
        inter = b + m[..., None]
        m_t = jnp.maximum(inter, jnp.max(d_log, axis=-1))
        w = jnp.exp(d_log - m_t[..., None]) * jnp.einsum('bhtd,bhsd->bhts', qc, kc)
        a = jnp.exp(inter - m_t)
        num = jnp.einsum('bhts,bhsv->bhtv', w, vc) + a[..., None] * jnp.einsum('bhvd,bhtd->bhtv', C, qc)
        den = jnp.sum(w, axis=-1) + a * jnp.einsum('bhd,bhtd->bht', n, qc)
        h = num / jnp.maximum(jnp.abs(den), jnp.exp(-m_t))[..., None]
        b_end = b[..., -1]
        g = b_end[..., None] - b + li
        m_new = jnp.maximum(b_end + m, jnp.max(g, axis=-1))
        ws = jnp.exp(g - m_new[..., None])
        a_end = jnp.exp(b_end + m - m_new)
        C_new = a_end[..., None, None] * C + jnp.einsum('bhs,bhsv,bhsd->bhvd', ws, vc, kc)
        n_new = a_end[..., None] * n + jnp.einsum('bhs,bhsd->bhd', ws, kc)
        return (C_new, n_new, m_new), h

    state, hs = lax.scan(step, state, (chunks(q), chunks(k), chunks(v), chunks(log_i), chunks(log_f)))
    h = jnp.moveaxis(hs, 0, 2).reshape(B, H, T, dv)
    return h.astype(v.dtype), state


def hierarchical_moe(xf, wg, bg, we, be, w1, w3, w2):
    N, D = xf.shape
    E = w1.shape[0]
    g_prob = jax.nn.softmax((xf @ wg).astype(F32) + bg.astype(F32), axis=-1)
    g_top, g_idx = lax.top_k(g_prob, 1)
    e_logits = ((xf @ we).astype(F32) + be.astype(F32)).reshape(N, MOE_GROUPS, MOE_EXPERTS_PER_GROUP)
    e_in_group = jnp.take_along_axis(e_logits, g_idx[:, :, None], axis=1)[:, 0]
    e_top, e_sub = lax.top_k(e_in_group, MOE_TOP_K)
    gates = jax.nn.softmax(e_top, axis=-1) * g_top
    expert = (g_idx * MOE_EXPERTS_PER_GROUP + e_sub).astype(jnp.int32)
    NK = N * MOE_TOP_K
    e_flat = expert.reshape(-1)
    g_flat = gates.reshape(-1)
    tok_flat = jnp.repeat(jnp.arange(N, dtype=jnp.int32), MOE_TOP_K)
    order = jnp.argsort(e_flat)
    e_sorted, tok_sorted, g_sorted = e_flat[order], tok_flat[order], g_flat[order]
    counts = jnp.zeros((E,), jnp.int32).at[e_flat].add(1)
    padded = (counts + MOE_BLOCK - 1) // MOE_BLOCK * MOE_BLOCK
    pad_end = jnp.cumsum(padded)
    pad_start = pad_end - padded
    start = jnp.cumsum(counts) - counts
    dest = pad_start[e_sorted] + jnp.arange(NK, dtype=jnp.int32) - start[e_sorted]
    P = -(-(NK + E * MOE_BLOCK) // MOE_BLOCK) * MOE_BLOCK
    row_tok = jnp.full((P,), N, jnp.int32).at[dest].set(tok_sorted)
    row_gate = jnp.zeros((P,), F32).at[dest].set(g_sorted)
    n_blk = P // MOE_BLOCK
    blk_start = jnp.arange(n_blk, dtype=jnp.int32) * MOE_BLOCK
    blk_expert = jnp.minimum(jnp.sum(pad_end[None, :] <= blk_start[:, None], axis=-1), E - 1).astype(jnp.int32)
    x_rows = jnp.concatenate([xf, jnp.zeros((1, D), xf.dtype)], axis=0)[row_tok].reshape(n_blk, MOE_BLOCK, D)

    def expert_block(args):
        xb, e = args
        return (jax.nn.silu(xb @ w1[e]) * (xb @ w3[e])) @ w2[e]

    y = lax.map(expert_block, (x_rows, blk_expert)).reshape(P, D)
    out = jnp.zeros((N + 1, D), y.dtype).at[row_tok].add(y * row_gate[:, None].astype(y.dtype))
    return out[:N]


def hybrid_layer(x, ctx, c, c_ctx, rows, cols, lam_init, update_ctx,
                 norm1_g, norm2_g, w_mod, b_mod, w_in, w_out,
                 diff_q_norm, diff_k_norm, diff_lambda, diff_subln, gqa_q_norm, gqa_k_norm,
                 mlstm_conv_w, mlstm_conv_b, mlstm_gate_b, mlstm_head_norm,
                 moe_wg, moe_bg, moe_we, moe_be, moe_w1, moe_w3, moe_w2):
    B, T, D = x.shape
    C = ctx.shape[1]
    sh1, sc1, g1, sh2, sc2, g2 = jnp.split((jax.nn.silu(c) @ w_mod + b_mod)[:, None, :], 6, axis=-1)
    sh1c, sc1c, g1c, sh2c, sc2c, g2c = jnp.split(jax.nn.silu(c_ctx) @ w_mod + b_mod, 6, axis=-1)
    split_at = [int(i) for i in np.cumsum(IN_SPLITS)[:-1]]
    dq_l, dk_l, dv_l, gq_l, gk_l, gv_l, mqk_l, mv_l, mo_l, mg_l = jnp.split(
        modulate(rms_norm(x, norm1_g), sh1, sc1) @ w_in, split_at, axis=-1)
    dq_c, dk_c, dv_c, gq_c, gk_c, gv_c, mqk_c, mv_c, mo_c, mg_c = jnp.split(
        modulate(rms_norm(ctx, norm1_g), sh1c, sc1c) @ w_in, split_at, axis=-1)

    def heads(p, n, d):
        return p.reshape(p.shape[0], p.shape[1], n, d).transpose(0, 2, 1, 3)

    def qk_heads(p, n, d, g, rope):
        y = rms_norm(p.reshape(p.shape[0], p.shape[1], n, d), g)
        if rope:
            y = axial_rope(y, rows, cols)
        return y.transpose(0, 2, 1, 3)

    def diff_qk(p, g, rope):
        y = rms_norm(p.reshape(p.shape[0], p.shape[1], DIFF_HEADS, 2, DIFF_HEAD_DIM), g)
        if rope:
            y = axial_rope(y, rows, cols)
        y = y.transpose(3, 0, 2, 1, 4)
        return y[0], y[1]

    def cat(a_c, a_l):
        return jnp.concatenate([a_c, a_l], axis=2)

    lam = (jnp.exp(jnp.sum(diff_lambda[0] * diff_lambda[1]).astype(F32))
           - jnp.exp(jnp.sum(diff_lambda[2] * diff_lambda[3]).astype(F32)) + lam_init)
    k1_c, k2_c = diff_qk(dk_c, diff_k_norm, False)
    k1_l, k2_l = diff_qk(dk_l, diff_k_norm, True)
    q1_l, q2_l = diff_qk(dq_l, diff_q_norm, True)
    dv_ch = heads(dv_c, DIFF_HEADS, DIFF_V_DIM)
    dv_lh = heads(dv_l, DIFF_HEADS, DIFF_V_DIM)

    def diff_out(o):
        o = rms_norm(o.transpose(0, 2, 1, 3), diff_subln) * (1.0 - lam_init)
        return o.reshape(o.shape[0], o.shape[1], DIFF_WIDTH)

    diff_lat = diff_out(diff_attention(q1_l, q2_l, cat(k1_c, k1_l), cat(k2_c, k2_l), cat(dv_ch, dv_lh), lam))

    def group_q(q):
        return q.reshape(q.shape[0], GQA_KV_HEADS, GQA_Q_HEADS // GQA_KV_HEADS, q.shape[2], GQA_HEAD_DIM)

    def gqa_out(o):
        o = o.reshape(o.shape[0], GQA_Q_HEADS, o.shape[3], GQA_HEAD_DIM).transpose(0, 2, 1, 3)
        return o.reshape(o.shape[0], o.shape[1], GQA_WIDTH)

    gk_ch = qk_heads(gk_c, GQA_KV_HEADS, GQA_HEAD_DIM, gqa_k_norm, False)
    gk_lh = qk_heads(gk_l, GQA_KV_HEADS, GQA_HEAD_DIM, gqa_k_norm, True)
    gv_ch = heads(gv_c, GQA_KV_HEADS, GQA_HEAD_DIM)
    gv_lh = heads(gv_l, GQA_KV_HEADS, GQA_HEAD_DIM)
    gq_lh = group_q(qk_heads(gq_l, GQA_Q_HEADS, GQA_HEAD_DIM, gqa_q_norm, True))
    gqa_lat = gqa_out(gqa_attention(gq_lh, cat(gk_ch, gk_lh), cat(gv_ch, gv_lh)))

    def mlstm_inputs(qk, v, gates):
        qk = jax.nn.silu(centred_conv(qk, mlstm_conv_w, mlstm_conv_b))
        q, k = jnp.split(qk, 2, axis=-1)
        q = heads(q, MLSTM_HEADS, MLSTM_HEAD_DIM)
        k = heads(k, MLSTM_HEADS, MLSTM_HEAD_DIM) * (MLSTM_HEAD_DIM ** -0.5)
        v = heads(v, MLSTM_HEADS, MLSTM_HEAD_DIM)
        gt = (gates + mlstm_gate_b).astype(F32).reshape(gates.shape[0], gates.shape[1], 4, MLSTM_HEADS)
        gt = gt.transpose(2, 0, 3, 1)
        return q, k, v, gt[0], jax.nn.log_sigmoid(gt[1]), gt[2], jax.nn.log_sigmoid(gt[3])

    def flip(a):
        return jnp.flip(a, axis=2)

    def mlstm_out(h, o):
        h = rms_norm(h.transpose(0, 2, 1, 3), mlstm_head_norm.reshape(MLSTM_HEADS, MLSTM_HEAD_DIM))
        h = h * jax.nn.sigmoid(o).reshape(h.shape)
        return h.reshape(h.shape[0], h.shape[1], MLSTM_WIDTH)

    qc, kc, vc, ifc, ffc, ibc, fbc = mlstm_inputs(mqk_c, mv_c, mg_c)
    ql, kl, vl, ifl, ffl, ibl, fbl = mlstm_inputs(mqk_l, mv_l, mg_l)
    zero = mlstm_zero_state(B)
    hf_c, st_f = mlstm_scan(qc, kc, vc, ifc, ffc, zero)
    hb_c, st_b = mlstm_scan(flip(qc), flip(kc), flip(vc), flip(ibc), flip(fbc), zero)
    hf_l, _ = mlstm_scan(ql, kl, vl, ifl, ffl, st_f)
    hb_l, _ = mlstm_scan(flip(ql), flip(kl), flip(vl), flip(ibl), flip(fbl), st_b)
    mlstm_lat = mlstm_out(hf_l + flip(hb_l), mo_l)

    x = x + g1 * (jnp.concatenate([diff_lat, gqa_lat, mlstm_lat], axis=-1) @ w_out)
    f_l = modulate(rms_norm(x, norm2_g), sh2, sc2)
    moe_args = (moe_wg, moe_bg, moe_we, moe_be, moe_w1, moe_w3, moe_w2)
    if update_ctx:
        q1_c, q2_c = diff_qk(dq_c, diff_q_norm, False)
        diff_ctx = diff_out(diff_attention(q1_c, q2_c, k1_c, k2_c, dv_ch, lam))
        gq_ch = group_q(qk_heads(gq_c, GQA_Q_HEADS, GQA_HEAD_DIM, gqa_q_norm, False))
        gqa_ctx = gqa_out(gqa_attention(gq_ch, gk_ch, gv_ch))
        mlstm_ctx = mlstm_out(hf_c + flip(hb_c), mo_c)
        ctx = ctx + g1c * (jnp.concatenate([diff_ctx, gqa_ctx, mlstm_ctx], axis=-1) @ w_out)
        f_c = modulate(rms_norm(ctx, norm2_g), sh2c, sc2c)
        y = hierarchical_moe(jnp.concatenate([f_l.reshape(-1, D), f_c.reshape(-1, D)], axis=0), *moe_args)
        x = x + g2 * y[:B * T].reshape(B, T, D)
        ctx = ctx + g2c * y[B * T:].reshape(B, C, D)
    else:
        x = x + g2 * hierarchical_moe(f_l.reshape(-1, D), *moe_args).reshape(B, T, D)
    return x, ctx


def setup_inputs(seed: int = 0) -> dict:
    key = jax.random.key(seed)
    ks = jax.random.split(key, 27)
    D = D_MODEL
    L = DEPTH

    def nrm(k, shape, scale):
        return scale * jax.random.normal(k, shape, F32)

    fb = jnp.linspace(3.0, 6.0, MLSTM_HEADS, dtype=F32)
    zb = jnp.zeros((MLSTM_HEADS,), F32)
    gate_base = jnp.concatenate([zb, fb, zb, fb])
    return {
        'x': nrm(ks[0], (BATCH, SEQ, D), 1.0),
        'c': nrm(ks[1], (BATCH, D), 1.0),
        'ctx': nrm(ks[2], (BATCH, CTX_LEN, D), 1.0),
        'c_ctx': nrm(ks[3], (D,), 1.0),
        'norm1_g': 1.0 + nrm(ks[4], (L, D), 0.02),
        'norm2_g': 1.0 + nrm(ks[5], (L, D), 0.02),
        'w_mod': nrm(ks[6], (L, D, 6 * D), 0.5 * D ** -0.5),
        'b_mod': nrm(ks[7], (L, 6 * D), 0.02),
        'w_in': nrm(ks[8], (L, D, IN_WIDTH), D ** -0.5),
        'w_out': nrm(ks[9], (L, MIX_WIDTH, D), MIX_WIDTH ** -0.5),
        'diff_q_norm': 1.0 + nrm(ks[10], (L, DIFF_HEAD_DIM), 0.02),
        'diff_k_norm': 1.0 + nrm(ks[11], (L, DIFF_HEAD_DIM), 0.02),
        'diff_lambda': nrm(ks[12], (L, 4, DIFF_HEAD_DIM), 0.1),
        'diff_subln': 1.0 + nrm(ks[13], (L, DIFF_V_DIM), 0.02),
        'gqa_q_norm': 1.0 + nrm(ks[14], (L, GQA_HEAD_DIM), 0.02),
        'gqa_k_norm': 1.0 + nrm(ks[15], (L, GQA_HEAD_DIM), 0.02),
        'mlstm_conv_w': nrm(ks[16], (L, MLSTM_CONV, 2 * MLSTM_WIDTH), MLSTM_CONV ** -0.5),
        'mlstm_conv_b': nrm(ks[17], (L, 2 * MLSTM_WIDTH), 0.02),
        'mlstm_gate_b': gate_base + nrm(ks[18], (L, 4 * MLSTM_HEADS), 0.1),
        'mlstm_head_norm': 1.0 + nrm(ks[19], (L, MLSTM_WIDTH), 0.02),
        'moe_wg': nrm(ks[20], (L, D, MOE_GROUPS), D ** -0.5),
        'moe_bg': nrm(ks[21], (L, MOE_GROUPS), 0.01),
        'moe_we': nrm(ks[22], (L, D, MOE_EXPERTS), D ** -0.5),
        'moe_be': nrm(ks[23], (L, MOE_EXPERTS), 0.01),
        'moe_w1': nrm(ks[24], (L, MOE_EXPERTS, D, MOE_HIDDEN), D ** -0.5),
        'moe_w3': nrm(ks[25], (L, MOE_EXPERTS, D, MOE_HIDDEN), D ** -0.5),
        'moe_w2': nrm(ks[26], (L, MOE_EXPERTS, MOE_HIDDEN, D), MOE_HIDDEN ** -0.5),
    }


def reference(x, c, ctx, c_ctx, norm1_g, norm2_g, w_mod, b_mod, w_in, w_out,
              diff_q_norm, diff_k_norm, diff_lambda, diff_subln, gqa_q_norm, gqa_k_norm,
              mlstm_conv_w, mlstm_conv_b, mlstm_gate_b, mlstm_head_norm,
              moe_wg, moe_bg, moe_we, moe_be, moe_w1, moe_w3, moe_w2):
    n_rows = x.shape[1] // GRID_W
    rows = jnp.repeat(jnp.arange(n_rows, dtype=jnp.int32), GRID_W)
    cols = jnp.tile(jnp.arange(GRID_W, dtype=jnp.int32), n_rows)
    for l in range(DEPTH):
        lam_init = 0.8 - 0.6 * math.exp(-0.3 * l)
        x, ctx = hybrid_layer(x, ctx, c, c_ctx, rows, cols, lam_init, l < DEPTH - 1,
                              norm1_g[l], norm2_g[l], w_mod[l], b_mod[l], w_in[l], w_out[l],
                              diff_q_norm[l], diff_k_norm[l], diff_lambda[l], diff_subln[l],
                              gqa_q_norm[l], gqa_k_norm[l],
                              mlstm_conv_w[l], mlstm_conv_b[l], mlstm_gate_b[l], mlstm_head_norm[l],
                              moe_wg[l], moe_bg[l], moe_we[l], moe_be[l], moe_w1[l], moe_w3[l], moe_w2[l])
    return x
```

```python
import functools
import math

import numpy as np
import jax
import jax.numpy as jnp
from jax import lax
from jax.experimental import pallas as pl
from jax.experimental.pallas import tpu as pltpu

F32 = jnp.float32
BF16 = jnp.bfloat16
I32 = jnp.int32

D_MODEL = 1024
DEPTH = 4
GRID_W = 64
CTX_LEN = 256
SEQ = 2048
ROPE_THETA = 10000.0
EPS = 1e-6

DIFF_HEADS = 4
DIFF_HEAD_DIM = 48
DIFF_V_DIM = 96
GQA_Q_HEADS = 6
GQA_KV_HEADS = 2
GQA_HEAD_DIM = 64
MLSTM_HEADS = 4
MLSTM_HEAD_DIM = 64
MLSTM_CHUNK = 64
IN_WIDTH = 2832
MOE_GROUPS = 4
MOE_EPG = 8
MOE_EXPERTS = 32
MOE_HIDDEN = 512

LANE = 128
TILE = 256
SEG_TILES = (CTX_LEN + SEQ) // TILE
SEG_ROWS = CTX_LEN + SEQ
N_CHUNKS = SEG_ROWS // MLSTM_CHUNK
CTX_CHUNKS = CTX_LEN // MLSTM_CHUNK
MOE_TILE = 256
VMEM_LIMIT = 56 * 1024 * 1024

C_DQ, C_DK, C_DV, C_GQ, C_GK, C_GV, C_MQK, C_MV, C_MO, C_MG, C_END = (
    0, 512, 1024, 1536, 1920, 2048, 2176, 2688, 2944, 3200, 3328)
MIX_ROWS = 512 + 384 + 256


def _cparams(sem):
    return pltpu.CompilerParams(dimension_semantics=sem, vmem_limit_bytes=VMEM_LIMIT)


def _diff_lane_dim(lane):
    slot, i = lane // 32, lane % 32
    which, typ = slot % 2, slot // 2
    if i >= 24:
        return None
    part, j = i // 12, i % 12
    return which, part * 24 + typ * 12 + j


def _gqa_lane_dim(lane):
    slot, i = lane // 32, lane % 32
    sel, typ = slot % 2, slot // 2
    part, j = i // 16, i % 16
    return sel, part * 32 + typ * 16 + j


def _build_in_cols():
    cols = np.full((C_END,), -1, np.int64)
    for h in range(DIFF_HEADS):
        for lane in range(LANE):
            r = _diff_lane_dim(lane)
            if r is not None:
                which, d = r
                cols[C_DQ + h * LANE + lane] = 0 + h * 96 + which * 48 + d
                cols[C_DK + h * LANE + lane] = 384 + h * 96 + which * 48 + d
            if lane < DIFF_V_DIM:
                cols[C_DV + h * LANE + lane] = 768 + h * 96 + lane
    for p in range(3):
        for lane in range(LANE):
            sel, d = _gqa_lane_dim(lane)
            cols[C_GQ + p * LANE + lane] = 1152 + (p + 3 * sel) * 64 + d
    for lane in range(LANE):
        sel, d = _gqa_lane_dim(lane)
        cols[C_GK + lane] = 1536 + sel * 64 + d
        cols[C_GV + lane] = 1664 + lane
    cols[C_MQK:C_MQK + 512] = 1792 + np.arange(512)
    cols[C_MV:C_MV + 256] = 2304 + np.arange(256)
    cols[C_MO:C_MO + 256] = 2560 + np.arange(256)
    cols[C_MG:C_MG + 16] = 2816 + np.arange(16)
    return cols


def _build_out_rows():
    rows = np.full((MIX_ROWS,), -1, np.int64)
    for h in range(DIFF_HEADS):
        rows[h * LANE:h * LANE + DIFF_V_DIM] = h * 96 + np.arange(96)
    for p in range(3):
        rows[512 + p * LANE:512 + p * LANE + 64] = 384 + p * 64 + np.arange(64)
        rows[512 + p * LANE + 64:512 + (p + 1) * LANE] = 384 + (p + 3) * 64 + np.arange(64)
    rows[896:1152] = 768 + np.arange(256)
    return rows


_IN_COLS = _build_in_cols()
_OUT_ROWS = _build_out_rows()
_DIFF_DIMS = np.array([(-1 if _diff_lane_dim(l) is None else _diff_lane_dim(l)[1]) for l in range(LANE)])
_GQA_DIMS = np.array([_gqa_lane_dim(l)[1] for l in range(LANE)])
_PAIR_MAT = ((np.arange(LANE)[:, None] // 32) % 2 == (np.arange(LANE)[None, :] // 32) % 2).astype(np.float32)


def _rope_tables(nf, pad_from):
    t = jnp.arange(SEQ, dtype=I32)
    rows = (t // GRID_W).astype(F32)
    cols = (t % GRID_W).astype(F32)
    freqs = ROPE_THETA ** (-jnp.arange(nf, dtype=F32) / nf)
    lane = np.arange(LANE)
    i = lane % 32
    typ = (lane // 32) // 2
    use_rows = i < nf
    fidx = np.where(use_rows, i, i - nf)
    valid = i < pad_from
    fidx = np.where(valid, fidx, 0)
    ang = jnp.where(jnp.asarray(use_rows)[None, :], rows[:, None], cols[:, None]) * freqs[jnp.asarray(fidx)][None, :]
    cos = jnp.where(jnp.asarray(valid)[None, :], jnp.cos(ang), 1.0)
    sin = jnp.where(jnp.asarray(valid)[None, :], jnp.sin(ang), 0.0)
    sin = sin * jnp.asarray(np.where(typ == 0, -1.0, 1.0), F32)[None, :]
    cos = jnp.concatenate([jnp.ones((CTX_LEN, LANE), F32), cos], axis=0)
    sin = jnp.concatenate([jnp.zeros((CTX_LEN, LANE), F32), sin], axis=0)
    return cos, sin


def _dot(a, b):
    return jnp.dot(a, b, preferred_element_type=F32)


def _dot_nt(a, b):
    return lax.dot_general(a, b, (((1,), (1,)), ((), ())), preferred_element_type=F32)


def _dot_tn(a, b):
    return lax.dot_general(a, b, (((0,), (0,)), ((), ())), preferred_element_type=F32)


def _split3(x):
    x1 = x.astype(BF16)
    r1 = x - x1.astype(F32)
    x2 = r1.astype(BF16)
    x3 = (r1 - x2.astype(F32)).astype(BF16)
    return x1, x2, x3


def _dot_f32_by_exact(x, m):
    x1, x2, x3 = _split3(x)
    return _dot(x1, m) + _dot(x2, m) + _dot(x3, m)


def _exact_by_dot_f32(m, x):
    x1, x2, x3 = _split3(x)
    return _dot(m, x1) + _dot(m, x2) + _dot(m, x3)


def _sigmoid(x):
    return 1.0 / (1.0 + jnp.exp(-x))


def _silu(x):
    return x * _sigmoid(x)


def _log_sigmoid(x):
    return jnp.minimum(x, 0.0) - jnp.log1p(jnp.exp(-jnp.abs(x)))


def _seg_tile(i, nq, off):
    return (i // nq) * SEG_TILES + off + i % nq


def _mod_row(i, nq, off):
    return jnp.where((off + i % nq) == 0, 8, i // nq)


MOD_BN = 1536


def _mod_kernel(c_ref, w_ref, b_ref, o_ref):
    a = _silu(c_ref[...]).astype(BF16)
    o_ref[0] = _dot(a, w_ref[0].astype(BF16)) + b_ref[0]


def _modulation(cvec, w_mod, b_mod):
    nb = 6 * D_MODEL // MOD_BN
    return pl.pallas_call(
        _mod_kernel,
        grid=(DEPTH, nb),
        in_specs=[pl.BlockSpec((16, D_MODEL), lambda l, n: (0, 0)),
                  pl.BlockSpec((1, D_MODEL, MOD_BN), lambda l, n: (l, 0, n)),
                  pl.BlockSpec((1, 1, MOD_BN), lambda l, n: (l, 0, n))],
        out_specs=pl.BlockSpec((1, 16, MOD_BN), lambda l, n: (l, 0, n)),
        out_shape=jax.ShapeDtypeStruct((DEPTH, 16, 6 * D_MODEL), F32),
        compiler_params=_cparams(("arbitrary", "arbitrary")),
        name="modulation",
    )(cvec, w_mod, b_mod.reshape(DEPTH, 1, 6 * D_MODEL))


def _inproj_kernel(has_moe, *refs):
    if has_moe:
        (x_ref, y_ref, rg_ref, modp_ref, mod_ref, n1_ref, w_ref, gains_ref, gb_ref, cd_ref, sd_ref, cg_ref, sg_ref,
         pm_ref, xo_ref, dq_ref, dk_ref, dv_ref, gq_ref, gk_ref, gv_ref, mqk_ref, mv_ref, mo_ref, mg_ref) = refs
    else:
        (x_ref, mod_ref, n1_ref, w_ref, gains_ref, gb_ref, cd_ref, sd_ref, cg_ref, sg_ref,
         pm_ref, dq_ref, dk_ref, dv_ref, gq_ref, gk_ref, gv_ref, mqk_ref, mv_ref, mo_ref, mg_ref) = refs
    x = x_ref[...]
    if has_moe:
        g2 = modp_ref[0, :, 5 * D_MODEL:6 * D_MODEL]
        rg = rg_ref[...]
        y = y_ref[...]
        x = x + g2 * (rg[:, 0:1] * y[:, :D_MODEL] + rg[:, 1:2] * y[:, D_MODEL:])
        xo_ref[...] = x
    sh = mod_ref[0, :, 0:D_MODEL]
    sc = mod_ref[0, :, D_MODEL:2 * D_MODEL]
    xn = x * lax.rsqrt(jnp.mean(x * x, axis=-1, keepdims=True) + EPS) * n1_ref[...]
    h = (xn * (1.0 + sc) + sh).astype(BF16)
    pm = pm_ref[...]

    def qk_block(col, gain_col, inv_dim, cos, sin):
        yb = _dot(h, w_ref[:, col:col + LANE])
        ssq = _dot_f32_by_exact(yb * yb, pm)
        yn = yb * lax.rsqrt(ssq * inv_dim + EPS) * gains_ref[:, gain_col:gain_col + LANE]
        return (yn * cos + pltpu.roll(yn, 64, 1) * sin).astype(BF16)

    cd, sd, cg, sg = cd_ref[...], sd_ref[...], cg_ref[...], sg_ref[...]
    for hb in range(DIFF_HEADS):
        dq_ref[:, hb * LANE:(hb + 1) * LANE] = qk_block(C_DQ + hb * LANE, hb * LANE, 1.0 / DIFF_HEAD_DIM, cd, sd)
        dk_ref[:, hb * LANE:(hb + 1) * LANE] = qk_block(C_DK + hb * LANE, 512 + hb * LANE, 1.0 / DIFF_HEAD_DIM, cd, sd)
    for p in range(3):
        gq_ref[:, p * LANE:(p + 1) * LANE] = qk_block(C_GQ + p * LANE, 1024 + p * LANE, 1.0 / GQA_HEAD_DIM, cg, sg)
    gk_ref[...] = qk_block(C_GK, 1408, 1.0 / GQA_HEAD_DIM, cg, sg)
    dv_ref[...] = _dot(h, w_ref[:, C_DV:C_GQ]).astype(BF16)
    gv_ref[...] = _dot(h, w_ref[:, C_GV:C_MQK]).astype(BF16)
    mqk_ref[...] = _dot(h, w_ref[:, C_MQK:C_MV])
    mv_ref[...] = _dot(h, w_ref[:, C_MV:C_MO]).astype(BF16)
    mo_ref[...] = _dot(h, w_ref[:, C_MO:C_MG])
    mg_ref[...] = _dot(h, w_ref[:, C_MG:C_END]) + gb_ref[...]


def _inproj(x, moe_in, mod_l, n1g, w_in_l, gains, gate_b, tabs, pair_mat, nt):
    has_moe = moe_in is not None
    n_tiles = nt // TILE
    tile_map = lambda i: (i, 0)
    mod_spec = pl.BlockSpec((1, 1, 6 * D_MODEL), lambda i: (_mod_row(i, SEG_TILES, 0), 0, 0))
    tab_spec = pl.BlockSpec((TILE, LANE), lambda i: (i % SEG_TILES, 0))

    def full(shape):
        return pl.BlockSpec(shape, lambda i: (0,) * len(shape))

    in_specs = [pl.BlockSpec((TILE, D_MODEL), tile_map)]
    args = [x]
    if has_moe:
        y_flat, rg, mod_prev = moe_in
        in_specs += [pl.BlockSpec((TILE, 2 * D_MODEL), tile_map), pl.BlockSpec((TILE, LANE), tile_map), mod_spec]
        args += [y_flat, rg, mod_prev]
    in_specs += [mod_spec, full((1, D_MODEL)), full((D_MODEL, C_END)), full((1, 1536)), full((1, LANE)),
                 tab_spec, tab_spec, tab_spec, tab_spec, full((LANE, LANE))]
    args += [mod_l, n1g, w_in_l, gains, gate_b, tabs[0], tabs[1], tabs[2], tabs[3], pair_mat]

    def o(width, dtype):
        return pl.BlockSpec((TILE, width), tile_map), jax.ShapeDtypeStruct((nt, width), dtype)

    outs = []
    if has_moe:
        outs.append(o(D_MODEL, F32))
    outs += [o(512, BF16), o(512, BF16), o(512, BF16), o(384, BF16), o(LANE, BF16), o(LANE, BF16),
             o(512, F32), o(256, BF16), o(256, F32), o(LANE, F32)]
    res = pl.pallas_call(
        functools.partial(_inproj_kernel, has_moe),
        grid=(n_tiles,),
        in_specs=in_specs,
        out_specs=[s for s, _ in outs],
        out_shape=[s for _, s in outs],
        compiler_params=_cparams(("arbitrary",)),
        name="inproj_moe" if has_moe else "inproj",
    )(*args)
    if has_moe:
        return res[0], res[1:]
    return x, res


def _lambda_value(lam_ref, lam_init):
    lam = lam_ref[...]
    s01 = jnp.sum(lam[0:1] * lam[1:2], axis=-1, keepdims=True)
    s23 = jnp.sum(lam[2:3] * lam[3:4], axis=-1, keepdims=True)
    return jnp.exp(s01) - jnp.exp(s23) + lam_init


def _softmax_parts(s):
    e = jnp.exp(s - jnp.max(s, axis=-1, keepdims=True))
    return e, jnp.sum(e, axis=-1, keepdims=True)


def _run_ctx_or_full(with_ctx, run, k_ref, v_ref):
    if not with_ctx:
        run(k_ref[...], v_ref[...])
        return
    t = pl.program_id(2)

    @pl.when(t == 0)
    def _():
        run(k_ref[0:TILE, :], v_ref[0:TILE, :])

    @pl.when(t > 0)
    def _():
        run(k_ref[...], v_ref[...])


def _attn_maps(with_ctx):
    if with_ctx:
        m = lambda b, h, t: (b * SEG_TILES + t, h)
        return SEG_TILES, m, m
    nq = SEG_TILES - 1
    return nq, (lambda b, h, t: (b * SEG_TILES + 1 + t, h)), (lambda b, h, t: (b * nq + t, h))


def _diff_attn_kernel(lam_init, with_ctx, q_ref, k_ref, v_ref, lam_ref, sg_ref, o_ref):
    def run(kk, vv):
        q = q_ref[...]
        first = (lax.broadcasted_iota(I32, kk.shape, 1) // 32) % 2 == 0
        zero = jnp.zeros_like(kk)
        e1, l1 = _softmax_parts(_dot_nt(q, jnp.where(first, kk, zero)))
        e2, l2 = _softmax_parts(_dot_nt(q, jnp.where(first, zero, kk)))
        lam = _lambda_value(lam_ref, lam_init)
        p = e1 * (1.0 / l1) - e2 * (lam / l2)
        o = _dot(p.astype(BF16), vv)
        ms = jnp.sum(o * o, axis=-1, keepdims=True) * (1.0 / DIFF_V_DIM)
        o_ref[...] = (o * lax.rsqrt(ms + EPS) * sg_ref[...] * (1.0 - lam_init)).astype(BF16)

    _run_ctx_or_full(with_ctx, run, k_ref, v_ref)


def _diff_attention(dq, dk, dv, lam_pad, subln, lam_init, nt, with_ctx):
    nb = nt // SEG_ROWS
    nq, q_map, o_map = _attn_maps(with_ctx)
    kv_map = lambda b, h, t: (b, h)
    return pl.pallas_call(
        functools.partial(_diff_attn_kernel, lam_init, with_ctx),
        grid=(nb, DIFF_HEADS, nq),
        in_specs=[pl.BlockSpec((TILE, LANE), q_map), pl.BlockSpec((SEG_ROWS, LANE), kv_map),
                  pl.BlockSpec((SEG_ROWS, LANE), kv_map), pl.BlockSpec((8, LANE), lambda b, h, t: (0, 0)),
                  pl.BlockSpec((1, LANE), lambda b, h, t: (0, 0))],
        out_specs=pl.BlockSpec((TILE, LANE), o_map),
        out_shape=jax.ShapeDtypeStruct((nb * nq * TILE, 512), BF16),
        compiler_params=_cparams(("arbitrary", "arbitrary", "arbitrary")),
        name="diff_attn",
    )(dq, dk, dv, lam_pad, subln)


def _gqa_attn_kernel(with_ctx, q_ref, k_ref, v_ref, o_ref):
    def run(kk, vv):
        q = q_ref[...]
        first = (lax.broadcasted_iota(I32, kk.shape, 1) // 32) % 2 == 0
        zero = jnp.zeros_like(kk)

        def head(kmask):
            e, l = _softmax_parts(_dot_nt(q, kmask))
            return _dot(e.astype(BF16), vv) * (1.0 / l)

        oa = head(jnp.where(first, kk, zero))
        ob = head(jnp.where(first, zero, kk))
        lane = lax.broadcasted_iota(I32, oa.shape, 1)
        o_ref[...] = jnp.where(lane < GQA_HEAD_DIM, oa, ob).astype(BF16)

    _run_ctx_or_full(with_ctx, run, k_ref, v_ref)


def _gqa_attention(gq, gk, gv, nt, with_ctx):
    nb = nt // SEG_ROWS
    nq, q_map, o_map = _attn_maps(with_ctx)
    kv_map = lambda b, p, t: (b, 0)
    return pl.pallas_call(
        functools.partial(_gqa_attn_kernel, with_ctx),
        grid=(nb, 3, nq),
        in_specs=[pl.BlockSpec((TILE, LANE), q_map), pl.BlockSpec((SEG_ROWS, LANE), kv_map),
                  pl.BlockSpec((SEG_ROWS, LANE), kv_map)],
        out_specs=pl.BlockSpec((TILE, LANE), o_map),
        out_shape=jax.ShapeDtypeStruct((nb * nq * TILE, 384), BF16),
        compiler_params=_cparams(("arbitrary", "arbitrary", "arbitrary")),
        name="gqa_attn",
    )(gq, gk, gv)


def _mlstm_kernel(mqk_ref, mv_ref, mg_ref, mgt_ref, cw_ref, cb_ref, hf_ref, hb_ref, qk_s):
    L = MLSTM_CHUNK
    w0, w1, w2, cb = cw_ref[0:1, :], cw_ref[1:2, :], cw_ref[2:3, :], cb_ref[...]
    rid = lax.broadcasted_iota(I32, (TILE, 512), 0)
    kscale = jnp.where(lax.broadcasted_iota(I32, (1, 512), 1) < 256, 1.0, MLSTM_HEAD_DIM ** -0.5)
    zrow = jnp.zeros((1, 512), F32)
    for c in range(SEG_TILES):
        r0 = c * TILE
        xc = mqk_ref[r0:r0 + TILE, :]
        prev = zrow if c in (0, 1) else mqk_ref[r0 - 1:r0, :]
        nxt = zrow if c in (0, SEG_TILES - 1) else mqk_ref[r0 + TILE:r0 + TILE + 1, :]
        up = jnp.where(rid == 0, prev, pltpu.roll(xc, 1, 0))
        dn = jnp.where(rid == TILE - 1, nxt, pltpu.roll(xc, TILE - 1, 0))
        y = w0 * up + w1 * xc + w2 * dn + cb
        qk_s[r0:r0 + TILE, :] = _silu(y) * kscale

    ti = lax.broadcasted_iota(I32, (L, L), 0)
    si = lax.broadcasted_iota(I32, (L, L), 1)
    tri_le = (si <= ti)
    m_le = tri_le.astype(BF16)
    m_ge = (si >= ti).astype(BF16)
    row2 = lax.broadcasted_iota(I32, (2 * L, L), 0)
    trow = row2 % L
    scol = lax.broadcasted_iota(I32, (2 * L, L), 1)
    top2 = row2 < L
    mask_f = scol <= trow
    mask_b = scol >= trow
    lane_lo = lax.broadcasted_iota(I32, (L, LANE), 1) < L
    lane_lo2 = lax.broadcasted_iota(I32, (2 * L, LANE), 1) < L
    rr = lax.broadcasted_iota(I32, (LANE, LANE), 0)
    cc = lax.broadcasted_iota(I32, (LANE, LANE), 1)
    blockdiag = (rr < L) == (cc < L)
    rows_lo = lax.broadcasted_iota(I32, (LANE, 1), 0) < L
    lane1 = lax.broadcasted_iota(I32, (1, LANE), 1) < L
    top_col = lax.broadcasted_iota(I32, (2 * L, 1), 0) < L

    def chain(c, is_fwd, p, gcol, bcol_all, grow, brow_all, state, out_ref):
        ct, nvec, m0, m1 = state
        h0, h1 = 2 * p, 2 * p + 1
        gi, gf = (0, 4) if is_fwd else (8, 12)
        r0 = pl.multiple_of(c * L, L)
        q128 = qk_s[pl.ds(r0, L), p * LANE:(p + 1) * LANE]
        k128 = qk_s[pl.ds(r0, L), 256 + p * LANE:256 + (p + 1) * LANE]
        v128 = mv_ref[pl.ds(r0, L), p * LANE:(p + 1) * LANE]

        def stack_cols(arr, j0, j1):
            return jnp.concatenate([arr[:, j0:j0 + 1], arr[:, j1:j1 + 1]], axis=0)

        def stack_rows(arr, j0, j1):
            return jnp.where(top2, arr[j0:j0 + 1, :], arr[j1:j1 + 1, :])

        bcol = stack_cols(bcol_all, gf + h0, gf + h1)
        licol = stack_cols(gcol, gi + h0, gi + h1)
        brow = stack_rows(brow_all, gf + h0, gf + h1)
        lirow = stack_rows(grow, gi + h0, gi + h1)
        mcol = jnp.where(top_col, m0, m1)
        dlog = jnp.where(mask_f if is_fwd else mask_b, bcol - brow + lirow, -jnp.inf)
        inter = bcol + mcol
        mt = jnp.maximum(inter, jnp.max(dlog, axis=-1, keepdims=True))
        qb = q128.astype(BF16)
        zq = jnp.zeros_like(qb)
        qstack = jnp.concatenate([jnp.where(lane_lo, qb, zq), jnp.where(lane_lo, zq, qb)], axis=0)
        kb = k128.astype(BF16)
        w = jnp.exp(dlog - mt) * _dot_nt(qstack, kb)
        a = jnp.exp(inter - mt)
        intra = _dot(w.astype(BF16), v128)
        num_intra = jnp.where(lane_lo, intra[:L], intra[L:])
        a128 = jnp.where(lane_lo, a[:L], a[L:])
        num = num_intra + a128 * _dot(qb, ct.astype(BF16))
        sumw = jnp.sum(w, axis=-1, keepdims=True)
        qn = q128 * nvec
        qn0 = jnp.sum(jnp.where(lane_lo, qn, 0.0), axis=-1, keepdims=True)
        qn1 = jnp.sum(jnp.where(lane_lo, 0.0, qn), axis=-1, keepdims=True)
        den = jnp.where(lane_lo, sumw[:L] + a[:L] * qn0, sumw[L:] + a[L:] * qn1)
        mt128 = jnp.where(lane_lo, mt[:L], mt[L:])
        out_ref[pl.ds(r0, L), p * LANE:(p + 1) * LANE] = num / jnp.maximum(jnp.abs(den), jnp.exp(-mt128))
        e0 = (L - 1) if is_fwd else 0
        bend0 = bcol[e0:e0 + 1, :]
        bend1 = bcol[L + e0:L + e0 + 1, :]
        bend = jnp.where(top_col, bend0, bend1)
        g = bend - bcol + licol
        m0n = jnp.maximum(bend0 + m0, jnp.max(g[:L], axis=0, keepdims=True))
        m1n = jnp.maximum(bend1 + m1, jnp.max(g[L:], axis=0, keepdims=True))
        ws = jnp.exp(g - jnp.where(top_col, m0n, m1n))
        ae0 = jnp.exp(bend0 + m0 - m0n)
        ae1 = jnp.exp(bend1 + m1 - m1n)
        ws128 = jnp.where(lane_lo, ws[:L], ws[L:])
        vw = (v128.astype(F32) * ws128).astype(BF16)
        upd = _dot_tn(kb, vw)
        ct_new = jnp.where(rows_lo, ae0, ae1) * ct + jnp.where(blockdiag, upd, 0.0)
        n_new = jnp.where(lane1, ae0, ae1) * nvec + jnp.sum(k128 * ws128, axis=0, keepdims=True)
        return ct_new, n_new, m0n, m1n

    def body(i, carry):
        cf = i
        cbk = jnp.where(i < CTX_CHUNKS, CTX_CHUNKS - 1 - i, N_CHUNKS + CTX_CHUNKS - 1 - i)
        new = []
        for d, (c, out_ref) in enumerate(((cf, hf_ref), (cbk, hb_ref))):
            is_fwd = d == 0
            r0 = pl.multiple_of(c * L, L)
            gcol = mg_ref[pl.ds(r0, L), :]
            grow = mgt_ref[0, c]
            lf_col = _log_sigmoid(gcol)
            lf_row = _log_sigmoid(grow)
            if is_fwd:
                bcol_all = _exact_by_dot_f32(m_le, lf_col)
                brow_all = _dot_f32_by_exact(lf_row, m_ge)
            else:
                bcol_all = _exact_by_dot_f32(m_ge, lf_col)
                brow_all = _dot_f32_by_exact(lf_row, m_le)
            for p in range(2):
                st = carry[d * 2 + p]
                new.append(chain(c, is_fwd, p, gcol, bcol_all, grow, brow_all, st, out_ref))
        return tuple(new)

    z = (jnp.zeros((LANE, LANE), F32), jnp.zeros((1, LANE), F32), jnp.zeros((1, 1), F32), jnp.zeros((1, 1), F32))
    lax.fori_loop(0, N_CHUNKS, body, (z, z, z, z))


def _mlstm(mqk, mv, mg, mgt, conv_w, conv_b, nt):
    nb = nt // SEG_ROWS
    blk = lambda w: pl.BlockSpec((SEG_ROWS, w), lambda b: (b, 0))
    return pl.pallas_call(
        _mlstm_kernel,
        grid=(nb,),
        in_specs=[blk(512), blk(256), blk(LANE),
                  pl.BlockSpec((1, N_CHUNKS, 16, MLSTM_CHUNK), lambda b: (b, 0, 0, 0)),
                  pl.BlockSpec((8, 512), lambda b: (0, 0)), pl.BlockSpec((1, 512), lambda b: (0, 0))],
        out_specs=[blk(256), blk(256)],
        out_shape=[jax.ShapeDtypeStruct((nt, 256), F32), jax.ShapeDtypeStruct((nt, 256), F32)],
        scratch_shapes=[pltpu.VMEM((SEG_ROWS, 512), F32)],
        compiler_params=_cparams(("arbitrary",)),
        name="mlstm",
    )(mqk, mv, mg, mgt, conv_w, conv_b)


def _outproj_kernel(do_ref, go_ref, hf_ref, hb_ref, mo_ref, x_ref, mod_ref, n2_ref, hn_ref, wo_ref, wr_hi_ref, wr_lo_ref,
                    rb_ref, hm_ref, xm_ref, f_ref, ri_ref, rg_ref, cnt_ref, carry):
    i = pl.program_id(0)

    @pl.when(i == 0)
    def _():
        carry[...] = jnp.zeros_like(carry)

    hsum = hf_ref[...] + hb_ref[...]
    ssq = _dot_f32_by_exact(hsum * hsum, hm_ref[...])
    ml = hsum * lax.rsqrt(ssq * (1.0 / MLSTM_HEAD_DIM) + EPS) * hn_ref[...] * _sigmoid(mo_ref[...])
    acc = _dot(do_ref[...], wo_ref[0:512, :])
    acc += _dot(go_ref[...], wo_ref[512:896, :])
    acc += _dot(ml.astype(BF16), wo_ref[896:MIX_ROWS, :])
    g1 = mod_ref[0, :, 2 * D_MODEL:3 * D_MODEL]
    sh2 = mod_ref[0, :, 3 * D_MODEL:4 * D_MODEL]
    sc2 = mod_ref[0, :, 4 * D_MODEL:5 * D_MODEL]
    x = x_ref[...] + g1 * acc
    xm_ref[...] = x
    xn = x * lax.rsqrt(jnp.mean(x * x, axis=-1, keepdims=True) + EPS) * n2_ref[...]
    f = xn * (1.0 + sc2) + sh2
    f_ref[...] = f
    f1, f2, _ = _split3(f)
    logits = _dot(f1, wr_hi_ref[...]) + _dot(f2, wr_hi_ref[...]) + _dot(f1, wr_lo_ref[...]) + rb_ref[...]
    lane = lax.broadcasted_iota(I32, logits.shape, 1)
    neg = jnp.float32(-jnp.inf)
    big = jnp.int32(1 << 20)
    is_g = lane < MOE_GROUPS
    lg = jnp.where(is_g, logits, neg)
    gmax = jnp.max(lg, axis=-1, keepdims=True)
    g_top = 1.0 / jnp.sum(jnp.exp(lg - gmax), axis=-1, keepdims=True)
    g_idx = jnp.min(jnp.where(lg == gmax, lane, big), axis=-1, keepdims=True)
    in_grp = (lane >= MOE_GROUPS) & (lane < MOE_GROUPS + MOE_EXPERTS) & ((lane - MOE_GROUPS) // MOE_EPG == g_idx)
    le = jnp.where(in_grp, logits, neg)
    v1 = jnp.max(le, axis=-1, keepdims=True)
    l1 = jnp.min(jnp.where(le == v1, lane, big), axis=-1, keepdims=True)
    le2 = jnp.where(lane == l1, neg, le)
    v2 = jnp.max(le2, axis=-1, keepdims=True)
    l2 = jnp.min(jnp.where(le2 == v2, lane, big), axis=-1, keepdims=True)
    ex = jnp.exp(v2 - v1)
    gate1 = g_top / (1.0 + ex)
    gate2 = gate1 * ex
    oh1 = (lane == l1)
    oh2 = (lane == l2)
    both = oh1.astype(BF16) + oh2.astype(BF16)
    ri_ = lax.broadcasted_iota(I32, (TILE, TILE), 0)
    ci_ = lax.broadcasted_iota(I32, (TILE, TILE), 1)
    before = _dot((ci_ < ri_).astype(BF16), both) + carry[0:1, :]
    rank1 = jnp.sum(jnp.where(oh1, before, 0.0), axis=-1, keepdims=True)
    rank2 = jnp.sum(jnp.where(oh2, before, 0.0), axis=-1, keepdims=True)
    carry[0:1, :] = carry[0:1, :] + jnp.sum(both.astype(F32), axis=0, keepdims=True)
    cnt_ref[...] = jnp.broadcast_to(carry[0:1, :], cnt_ref.shape)
    ri = jnp.where(lane == 0, l1 - MOE_GROUPS, jnp.where(lane == 1, l2 - MOE_GROUPS,
         jnp.where(lane == 2, rank1.astype(I32), jnp.where(lane == 3, rank2.astype(I32), 0))))
    ri_ref[...] = ri
    rg_ref[...] = jnp.where(lane == 0, gate1, jnp.where(lane == 1, gate2, 0.0))


def _outproj(do, go, hf, hb, mo, x, mod_l, n2g, hn, w_out_l, wr_hi, wr_lo, rb, head_mat, nt, with_ctx):
    nb = nt // SEG_ROWS
    nq, off = (SEG_TILES, 0) if with_ctx else (SEG_TILES - 1, 1)
    n_steps = nb * nq
    n_out = n_steps * TILE
    in_map = lambda i: (_seg_tile(i, nq, off), 0)
    out_map = lambda i: (i, 0)

    def full(shape):
        return pl.BlockSpec(shape, lambda i: (0,) * len(shape))

    in_specs = [pl.BlockSpec((TILE, 512), out_map), pl.BlockSpec((TILE, 384), out_map), pl.BlockSpec((TILE, 256), in_map),
                pl.BlockSpec((TILE, 256), in_map), pl.BlockSpec((TILE, 256), in_map), pl.BlockSpec((TILE, D_MODEL), in_map),
                pl.BlockSpec((1, 1, 6 * D_MODEL), lambda i: (_mod_row(i, nq, off), 0, 0)),
                full((1, D_MODEL)), full((1, 256)), full((MIX_ROWS, D_MODEL)), full((D_MODEL, LANE)), full((D_MODEL, LANE)),
                full((1, LANE)), full((256, 256))]
    out_specs = [pl.BlockSpec((TILE, D_MODEL), out_map), pl.BlockSpec((TILE, D_MODEL), out_map),
                 pl.BlockSpec((TILE, LANE), out_map), pl.BlockSpec((TILE, LANE), out_map), pl.BlockSpec((8, LANE), lambda i: (0, 0))]
    out_shape = [jax.ShapeDtypeStruct((n_out, D_MODEL), F32), jax.ShapeDtypeStruct((n_out, D_MODEL), F32),
                 jax.ShapeDtypeStruct((n_out, LANE), I32), jax.ShapeDtypeStruct((n_out, LANE), F32),
                 jax.ShapeDtypeStruct((8, LANE), F32)]
    return pl.pallas_call(
        _outproj_kernel,
        grid=(n_steps,),
        in_specs=in_specs,
        out_specs=out_specs,
        out_shape=out_shape,
        scratch_shapes=[pltpu.VMEM((8, LANE), F32)],
        compiler_params=_cparams(("arbitrary",)),
        name="outproj_route",
    )(do, go, hf, hb, mo, x, mod_l, n2g, hn, w_out_l, wr_hi, wr_lo, rb, head_mat)


DMA_UNROLL = 8


def _for_rows(n, fn):
    groups = lax.shift_right_logical(n, int(math.log2(DMA_UNROLL)))

    def group(g, _):
        for u in range(DMA_UNROLL):
            fn(g * DMA_UNROLL + u)
        return 0

    def single(r, _):
        fn(r)
        return 0

    lax.fori_loop(0, groups, group, 0)
    lax.fori_loop(groups * DMA_UNROLL, n, single, 0)


def _expert_kernel(be_ref, nv_ref, src_ref, srcn_ref, dst_ref, f_hbm, w1_ref, w3_ref, w2_ref, y_hbm,
                   xbuf, ybuf, gsem, ssem):
    i = pl.program_id(0)
    nv = nv_ref[i]
    nv_next = nv_ref[i + 1]
    slot = i % 2

    def gather_copy(idx_ref, s, r):
        return pltpu.make_async_copy(f_hbm.at[pl.ds(idx_ref[0, 0, r], 1)], xbuf.at[s, pl.ds(r, 1)], gsem.at[s])

    def scatter_copy(r):
        return pltpu.make_async_copy(ybuf.at[pl.ds(r, 1)], y_hbm.at[pl.ds(dst_ref[0, 0, r], 1)], ssem)

    def wait_rows(n, copy_of):
        whole = lax.shift_right_logical(n, 3) * 8

        @pl.when(whole > 0)
        def _():
            copy_of(pl.multiple_of(whole, 8)).wait()

        def single(r, _):
            copy_of(1).wait()
            return 0

        lax.fori_loop(whole, n, single, 0)

    def gather_wait(n):
        wait_rows(n, lambda m: pltpu.make_async_copy(f_hbm.at[pl.ds(0, m)], xbuf.at[slot, pl.ds(0, m)], gsem.at[slot]))

    def scatter_wait(n):
        wait_rows(n, lambda m: pltpu.make_async_copy(ybuf.at[pl.ds(0, m)], y_hbm.at[pl.ds(0, m)], ssem))

    @pl.when(i == 0)
    def _():
        xbuf[...] = jnp.zeros_like(xbuf)
        _for_rows(nv, lambda r: gather_copy(src_ref, 0, r).start())

    @pl.when(nv_next > 0)
    def _():
        _for_rows(nv_next, lambda r: gather_copy(srcn_ref, 1 - slot, r).start())

    @pl.when(nv > 0)
    def _():
        gather_wait(nv)
        xb = xbuf[slot].astype(BF16)
        hh = _silu(_dot(xb, w1_ref[0])) * _dot(xb, w3_ref[0])
        y = _dot(hh.astype(BF16), w2_ref[0])

        @pl.when(i > 0)
        def _():
            scatter_wait(nv_ref[jnp.maximum(i - 1, 0)])

        ybuf[...] = y
        _for_rows(nv, lambda r: scatter_copy(r).start())

        @pl.when(nv_next == 0)
        def _():
            scatter_wait(nv)


def _experts(f, row_src, row_dst, blk_expert, blk_valid, w1, w3, w2, n_tok):
    n_blk = row_src.shape[0]
    idx_spec = lambda fn: pl.BlockSpec((1, 1, MOE_TILE), fn, memory_space=pltpu.SMEM)
    w_spec = lambda shape: pl.BlockSpec((1,) + shape, lambda i, be, nv: (be[i], 0, 0))
    grid_spec = pltpu.PrefetchScalarGridSpec(
        num_scalar_prefetch=2,
        grid=(n_blk,),
        in_specs=[idx_spec(lambda i, be, nv: (i, 0, 0)),
                  idx_spec(lambda i, be, nv: (jnp.minimum(i + 1, n_blk - 1), 0, 0)),
                  idx_spec(lambda i, be, nv: (i, 0, 0)),
                  pl.BlockSpec(memory_space=pl.ANY),
                  w_spec((D_MODEL, MOE_HIDDEN)), w_spec((D_MODEL, MOE_HIDDEN)), w_spec((MOE_HIDDEN, D_MODEL))],
        out_specs=pl.BlockSpec(memory_space=pl.ANY),
        scratch_shapes=[pltpu.VMEM((2, MOE_TILE, D_MODEL), F32), pltpu.VMEM((MOE_TILE, D_MODEL), F32),
                        pltpu.SemaphoreType.DMA((2,)), pltpu.SemaphoreType.DMA(())],
    )
    return pl.pallas_call(
        _expert_kernel,
        grid_spec=grid_spec,
        out_shape=jax.ShapeDtypeStruct((2 * n_tok, D_MODEL), F32),
        compiler_params=_cparams(("arbitrary",)),
        name="experts",
    )(blk_expert, blk_valid, row_src, row_src, row_dst, f, w1, w3, w2)


def _route_plan(ri, counts, n_tok):
    n_blk = (2 * n_tok) // MOE_TILE + MOE_EXPERTS
    p_rows = n_blk * MOE_TILE
    cnt = counts[0, MOE_GROUPS:MOE_GROUPS + MOE_EXPERTS].astype(I32)
    padded = (cnt + MOE_TILE - 1) // MOE_TILE * MOE_TILE
    pad_end = jnp.cumsum(padded)
    pad_start = pad_end - padded
    dest = pad_start[ri[:, 0:2]] + ri[:, 2:4]
    row_dst = jnp.zeros((p_rows,), I32).at[dest.reshape(-1)].set(jnp.arange(2 * n_tok, dtype=I32))
    row_src = row_dst // 2
    blk_start = jnp.arange(n_blk + 1, dtype=I32) * MOE_TILE
    blk_expert = jnp.minimum(jnp.sum(pad_end[None, :] <= blk_start[:, None], axis=-1), MOE_EXPERTS - 1).astype(I32)
    in_expert = blk_start - pad_start[blk_expert]
    blk_valid = jnp.where(blk_start < pad_end[-1], jnp.clip(cnt[blk_expert] - in_expert, 0, MOE_TILE), 0).astype(I32)
    return (row_src.reshape(n_blk, 1, MOE_TILE), row_dst.reshape(n_blk, 1, MOE_TILE), blk_expert[:n_blk], blk_valid)


def _final_kernel(x_ref, y_ref, rg_ref, mod_ref, o_ref):
    g2 = mod_ref[0, :, 5 * D_MODEL:6 * D_MODEL]
    rg = rg_ref[...]
    y = y_ref[...]
    o_ref[...] = x_ref[...] + g2 * (rg[:, 0:1] * y[:, :D_MODEL] + rg[:, 1:2] * y[:, D_MODEL:])


def _final(x, y_flat, rg, mod_l, n_tok):
    nq = SEQ // TILE
    tile_map = lambda i: (i, 0)
    return pl.pallas_call(
        _final_kernel,
        grid=(n_tok // TILE,),
        in_specs=[pl.BlockSpec((TILE, D_MODEL), tile_map), pl.BlockSpec((TILE, 2 * D_MODEL), tile_map),
                  pl.BlockSpec((TILE, LANE), tile_map), pl.BlockSpec((1, 1, 6 * D_MODEL), lambda i: (i // nq, 0, 0))],
        out_specs=pl.BlockSpec((TILE, D_MODEL), tile_map),
        out_shape=jax.ShapeDtypeStruct((n_tok, D_MODEL), F32),
        compiler_params=_cparams(("arbitrary",)),
        name="final_residual",
    )(x, y_flat, rg, mod_l)


def _take_padded(w, idx, axis):
    idx = np.asarray(idx)
    out = jnp.take(w, jnp.asarray(np.maximum(idx, 0)), axis=axis)
    mask = jnp.asarray(idx >= 0, w.dtype)
    shape = [1] * w.ndim
    shape[axis] = idx.shape[0]
    return out * mask.reshape(shape)


def _lane_gain(g, dims, scale):
    return _take_padded(g, dims, 1) * scale


def kernel(x, c, ctx, c_ctx, norm1_g, norm2_g, w_mod, b_mod, w_in, w_out, diff_q_norm, diff_k_norm, diff_lambda, diff_subln, gqa_q_norm, gqa_k_norm, mlstm_conv_w, mlstm_conv_b, mlstm_gate_b, mlstm_head_norm, moe_wg, moe_bg, moe_we, moe_be, moe_w1, moe_w3, moe_w2):
    B = x.shape[0]
    nt = B * SEG_ROWS
    n_lat = B * SEQ

    w_in_r = _take_padded(w_in, _IN_COLS, 2).astype(BF16)
    w_out_r = _take_padded(w_out, _OUT_ROWS, 1).astype(BF16)
    gq_d = _lane_gain(diff_q_norm, _DIFF_DIMS, DIFF_HEAD_DIM ** -0.5)
    gk_d = _lane_gain(diff_k_norm, _DIFF_DIMS, 1.0)
    gq_g = _lane_gain(gqa_q_norm, _GQA_DIMS, GQA_HEAD_DIM ** -0.5)
    gk_g = _lane_gain(gqa_k_norm, _GQA_DIMS, 1.0)
    gains = jnp.concatenate([jnp.tile(gq_d, (1, 4)), jnp.tile(gk_d, (1, 4)), jnp.tile(gq_g, (1, 3)), gk_g], axis=1)
    gains = gains.reshape(DEPTH, 1, 1536)
    gate_b = jnp.pad(mlstm_gate_b, ((0, 0), (0, LANE - 16))).reshape(DEPTH, 1, LANE)
    subln = jnp.pad(diff_subln, ((0, 0), (0, LANE - DIFF_V_DIM))).reshape(DEPTH, 1, LANE)
    lam_pad = jnp.pad(diff_lambda, ((0, 0), (0, 4), (0, LANE - DIFF_HEAD_DIM)))
    conv_w = jnp.pad(mlstm_conv_w, ((0, 0), (0, 5), (0, 0)))
    conv_b = mlstm_conv_b.reshape(DEPTH, 1, 512)
    w_r = jnp.pad(jnp.concatenate([moe_wg, moe_we], axis=2), ((0, 0), (0, 0), (0, LANE - 36)))
    wr_hi = w_r.astype(BF16)
    wr_lo = (w_r - wr_hi.astype(F32)).astype(BF16)
    rb = jnp.pad(jnp.concatenate([moe_bg, moe_be], axis=1), ((0, 0), (0, LANE - 36))).reshape(DEPTH, 1, LANE)
    w1b, w3b, w2b = moe_w1.astype(BF16), moe_w3.astype(BF16), moe_w2.astype(BF16)
    pair_mat = jnp.asarray(_PAIR_MAT, BF16)
    head_mat = jnp.asarray((np.arange(256)[:, None] // 64 == np.arange(256)[None, :] // 64).astype(np.float32), BF16)
    tabs = _rope_tables(12, 24) + _rope_tables(16, 32)

    xa = jnp.concatenate([ctx, x], axis=1).reshape(nt, D_MODEL)
    assert B <= 8, "row 8 of the modulation table is reserved for the context conditioning"
    cvec = jnp.concatenate([c, jnp.zeros((8 - B, D_MODEL), F32), c_ctx[None, :], jnp.zeros((7, D_MODEL), F32)], axis=0)
    mod = _modulation(cvec, w_mod, b_mod).reshape(DEPTH, 16, 1, 6 * D_MODEL)

    moe_in = None
    for l in range(DEPTH):
        last = l == DEPTH - 1
        lam_init = 0.8 - 0.6 * math.exp(-0.3 * l)
        xa, (dq, dk, dv, gq, gk, gv, mqk, mv, mo, mg) = _inproj(
            xa, moe_in, mod[l], norm1_g[l].reshape(1, D_MODEL), w_in_r[l], gains[l], gate_b[l], tabs, pair_mat, nt)
        do = _diff_attention(dq, dk, dv, lam_pad[l], subln[l], lam_init, nt, with_ctx=not last)
        go = _gqa_attention(gq, gk, gv, nt, with_ctx=not last)
        mgt = mg[:, :16].reshape(B, N_CHUNKS, MLSTM_CHUNK, 16).transpose(0, 1, 3, 2)
        hf, hb = _mlstm(mqk, mv, mg, mgt, conv_w[l], conv_b[l], nt)
        xm, f, ri, rg, counts = _outproj(do, go, hf, hb, mo, xa, mod[l], norm2_g[l].reshape(1, D_MODEL),
                                         mlstm_head_norm[l].reshape(1, 256), w_out_r[l], wr_hi[l], wr_lo[l], rb[l],
                                         head_mat, nt, with_ctx=not last)
        n_tok = n_lat if last else nt
        row_src, row_dst, blk_expert, blk_valid = _route_plan(ri, counts, n_tok)
        y = _experts(f, row_src, row_dst, blk_expert, blk_valid, w1b[l], w3b[l], w2b[l], n_tok)
        y_flat = y.reshape(n_tok, 2 * D_MODEL)
        if last:
            out = _final(xm, y_flat, rg, mod[l][:B], n_tok)
            return out.reshape(B, SEQ, D_MODEL)
        xa = xm
        moe_in = (y_flat, rg, mod[l])
```

```python
import functools
import math

import numpy as np
import jax
import jax.numpy as jnp
from jax import lax
from jax.experimental import pallas as pl
from jax.experimental.pallas import tpu as pltpu

F32 = jnp.float32
BF16 = jnp.bfloat16
I32 = jnp.int32

D_MODEL = 1024
DEPTH = 4
GRID_W = 64
CTX_LEN = 256
SEQ = 2048
ROPE_THETA = 10000.0
EPS = 1e-6

DIFF_HEADS = 4
DIFF_HEAD_DIM = 48
DIFF_V_DIM = 96
GQA_Q_HEADS = 6
GQA_KV_HEADS = 2
GQA_HEAD_DIM = 64
MLSTM_HEADS = 4
MLSTM_HEAD_DIM = 64
MLSTM_CHUNK = 64
IN_WIDTH = 2832
MOE_GROUPS = 4
MOE_EPG = 8
MOE_EXPERTS = 32
MOE_HIDDEN = 512

LANE = 128
TILE = 256
SEG_TILES = (CTX_LEN + SEQ) // TILE
SEG_ROWS = CTX_LEN + SEQ
N_CHUNKS = SEG_ROWS // MLSTM_CHUNK
CTX_CHUNKS = CTX_LEN // MLSTM_CHUNK
MOE_TILE = 256
VMEM_LIMIT = 56 * 1024 * 1024

C_DQ, C_DK, C_DV, C_GQ, C_GK, C_GV, C_MQK, C_MV, C_MO, C_MG, C_END = (
    0, 512, 1024, 1536, 1920, 2048, 2176, 2688, 2944, 3200, 3328)
MIX_ROWS = 512 + 384 + 256


def _cparams(sem):
    return pltpu.CompilerParams(dimension_semantics=sem, vmem_limit_bytes=VMEM_LIMIT)


_PAIR_MAT = ((np.arange(LANE)[:, None] // 32) % 2 == (np.arange(LANE)[None, :] // 32) % 2).astype(np.float32)


def _rope_tables(nf, pad_from):
    t = jnp.arange(SEQ, dtype=I32)
    rows = (t // GRID_W).astype(F32)
    cols = (t % GRID_W).astype(F32)
    freqs = ROPE_THETA ** (-jnp.arange(nf, dtype=F32) / nf)
    lane = np.arange(LANE)
    i = lane % 32
    typ = (lane // 32) // 2
    use_rows = i < nf
    fidx = np.where(use_rows, i, i - nf)
    valid = i < pad_from
    fidx = np.where(valid, fidx, 0)
    ang = jnp.where(jnp.asarray(use_rows)[None, :], rows[:, None], cols[:, None]) * freqs[jnp.asarray(fidx)][None, :]
    cos = jnp.where(jnp.asarray(valid)[None, :], jnp.cos(ang), 1.0)
    sin = jnp.where(jnp.asarray(valid)[None, :], jnp.sin(ang), 0.0)
    sin = sin * jnp.asarray(np.where(typ == 0, -1.0, 1.0), F32)[None, :]
    cos = jnp.concatenate([jnp.ones((CTX_LEN, LANE), F32), cos], axis=0)
    sin = jnp.concatenate([jnp.zeros((CTX_LEN, LANE), F32), sin], axis=0)
    return cos, sin


def _dot(a, b):
    return jnp.dot(a, b, preferred_element_type=F32)


def _dot_nt(a, b):
    return lax.dot_general(a, b, (((1,), (1,)), ((), ())), preferred_element_type=F32)


def _dot_tn(a, b):
    return lax.dot_general(a, b, (((0,), (0,)), ((), ())), preferred_element_type=F32)


def _split3(x):
    x1 = x.astype(BF16)
    r1 = x - x1.astype(F32)
    x2 = r1.astype(BF16)
    x3 = (r1 - x2.astype(F32)).astype(BF16)
    return x1, x2, x3


def _dot_f32_by_exact(x, m):
    x1, x2, x3 = _split3(x)
    return _dot(x1, m) + _dot(x2, m) + _dot(x3, m)


def _exact_by_dot_f32(m, x):
    x1, x2, x3 = _split3(x)
    return _dot(m, x1) + _dot(m, x2) + _dot(m, x3)


def _sigmoid(x):
    return 1.0 / (1.0 + jnp.exp(-x))


def _silu(x):
    return x * _sigmoid(x)


def _log_sigmoid(x):
    return jnp.minimum(x, 0.0) - jnp.log1p(jnp.exp(-jnp.abs(x)))


def _seg_tile(i, nq, off):
    return (i // nq) * SEG_TILES + off + i % nq


def _mod_row(i, nq, off):
    return jnp.where((off + i % nq) == 0, 8, i // nq)


MOD_BN = 1536


def _mod_kernel(c_ref, w_ref, b_ref, o_ref):
    a = _silu(c_ref[...]).astype(BF16)
    o_ref[0] = _dot(a, w_ref[0].astype(BF16)) + b_ref[0]


def _modulation(cvec, w_mod, b_mod):
    nb = 6 * D_MODEL // MOD_BN
    return pl.pallas_call(
        _mod_kernel,
        grid=(DEPTH, nb),
        in_specs=[pl.BlockSpec((16, D_MODEL), lambda l, n: (0, 0)),
                  pl.BlockSpec((1, D_MODEL, MOD_BN), lambda l, n: (l, 0, n)),
                  pl.BlockSpec((1, 1, MOD_BN), lambda l, n: (l, 0, n))],
        out_specs=pl.BlockSpec((1, 16, MOD_BN), lambda l, n: (l, 0, n)),
        out_shape=jax.ShapeDtypeStruct((DEPTH, 16, 6 * D_MODEL), F32),
        compiler_params=_cparams(("arbitrary", "arbitrary")),
        name="modulation",
    )(cvec, w_mod, b_mod.reshape(DEPTH, 1, 6 * D_MODEL))


def _inproj_kernel(has_moe, *refs):
    if has_moe:
        (x_ref, y_ref, rg_ref, modp_ref, mod_ref, n1_ref, w_ref, gains_ref, gb_ref, cd_ref, sd_ref, cg_ref, sg_ref,
         pm_ref, xo_ref, dq_ref, dk_ref, dv_ref, gq_ref, gk_ref, gv_ref, mqk_ref, mv_ref, mo_ref, mg_ref) = refs
    else:
        (x_ref, mod_ref, n1_ref, w_ref, gains_ref, gb_ref, cd_ref, sd_ref, cg_ref, sg_ref,
         pm_ref, dq_ref, dk_ref, dv_ref, gq_ref, gk_ref, gv_ref, mqk_ref, mv_ref, mo_ref, mg_ref) = refs
    x = x_ref[...]
    if has_moe:
        g2 = modp_ref[0, :, 5 * D_MODEL:6 * D_MODEL]
        rg = rg_ref[...]
        x = x + g2 * (rg[:, 0:1] * y_ref[0] + rg[:, 1:2] * y_ref[1])
        xo_ref[...] = x
    sh = mod_ref[0, :, 0:D_MODEL]
    sc = mod_ref[0, :, D_MODEL:2 * D_MODEL]
    xn = x * lax.rsqrt(jnp.mean(x * x, axis=-1, keepdims=True) + EPS) * n1_ref[...]
    h = (xn * (1.0 + sc) + sh).astype(BF16)
    pm = pm_ref[...]

    def qk_block(col, gain_col, inv_dim, cos, sin):
        yb = _dot(h, w_ref[:, col:col + LANE])
        ssq = _dot_f32_by_exact(yb * yb, pm)
        yn = yb * lax.rsqrt(ssq * inv_dim + EPS) * gains_ref[:, gain_col:gain_col + LANE]
        return (yn * cos + pltpu.roll(yn, 64, 1) * sin).astype(BF16)

    cd, sd, cg, sg = cd_ref[...], sd_ref[...], cg_ref[...], sg_ref[...]
    for hb in range(DIFF_HEADS):
        dq_ref[:, hb * LANE:(hb + 1) * LANE] = qk_block(C_DQ + hb * LANE, hb * LANE, 1.0 / DIFF_HEAD_DIM, cd, sd)
        dk_ref[:, hb * LANE:(hb + 1) * LANE] = qk_block(C_DK + hb * LANE, 512 + hb * LANE, 1.0 / DIFF_HEAD_DIM, cd, sd)
    for p in range(3):
        gq_ref[:, p * LANE:(p + 1) * LANE] = qk_block(C_GQ + p * LANE, 1024 + p * LANE, 1.0 / GQA_HEAD_DIM, cg, sg)
    gk_ref[...] = qk_block(C_GK, 1408, 1.0 / GQA_HEAD_DIM, cg, sg)
    dv_ref[...] = _dot(h, w_ref[:, C_DV:C_GQ]).astype(BF16)
    gv_ref[...] = _dot(h, w_ref[:, C_GV:C_MQK]).astype(BF16)
    mqk_ref[...] = _dot(h, w_ref[:, C_MQK:C_MV])
    mv_ref[...] = _dot(h, w_ref[:, C_MV:C_MO]).astype(BF16)
    mo_ref[...] = _dot(h, w_ref[:, C_MO:C_MG])
    mg_ref[...] = _dot(h, w_ref[:, C_MG:C_END]) + gb_ref[...]


def _inproj(x, moe_in, mod_l, n1g, w_in_l, gains, gate_b, tabs, pair_mat, nt):
    has_moe = moe_in is not None
    n_tiles = nt // TILE
    tile_map = lambda i: (i, 0)
    mod_spec = pl.BlockSpec((1, 1, 6 * D_MODEL), lambda i: (_mod_row(i, SEG_TILES, 0), 0, 0))
    tab_spec = pl.BlockSpec((TILE, LANE), lambda i: (i % SEG_TILES, 0))

    def full(shape):
        return pl.BlockSpec(shape, lambda i: (0,) * len(shape))

    in_specs = [pl.BlockSpec((TILE, D_MODEL), tile_map)]
    args = [x]
    if has_moe:
        y_flat, rg, mod_prev = moe_in
        in_specs += [pl.BlockSpec((2, TILE, D_MODEL), lambda i: (0, i, 0)), pl.BlockSpec((TILE, LANE), tile_map), mod_spec]
        args += [y_flat, rg, mod_prev]
    in_specs += [mod_spec, full((1, D_MODEL)), full((D_MODEL, C_END)), full((1, 1536)), full((1, LANE)),
                 tab_spec, tab_spec, tab_spec, tab_spec, full((LANE, LANE))]
    args += [mod_l, n1g, w_in_l, gains, gate_b, tabs[0], tabs[1], tabs[2], tabs[3], pair_mat]

    def o(width, dtype):
        return pl.BlockSpec((TILE, width), tile_map), jax.ShapeDtypeStruct((nt, width), dtype)

    outs = []
    if has_moe:
        outs.append(o(D_MODEL, F32))
    outs += [o(512, BF16), o(512, BF16), o(512, BF16), o(384, BF16), o(LANE, BF16), o(LANE, BF16),
             o(512, F32), o(256, BF16), o(256, F32), o(LANE, F32)]
    res = pl.pallas_call(
        functools.partial(_inproj_kernel, has_moe),
        grid=(n_tiles,),
        in_specs=in_specs,
        out_specs=[s for s, _ in outs],
        out_shape=[s for _, s in outs],
        compiler_params=_cparams(("arbitrary",)),
        name="inproj_moe" if has_moe else "inproj",
    )(*args)
    if has_moe:
        return res[0], res[1:]
    return x, res


def _lambda_value(lam_ref, lam_init):
    lam = lam_ref[...]
    s01 = jnp.sum(lam[0:1] * lam[1:2], axis=-1, keepdims=True)
    s23 = jnp.sum(lam[2:3] * lam[3:4], axis=-1, keepdims=True)
    return jnp.exp(s01) - jnp.exp(s23) + lam_init


LOG2E = 1.4426950408889634
SAFE_LOG2_RANGE = 60.0


def _exp_scores(s, stabilise):
    if stabilise:
        s = s - jnp.max(s, axis=-1, keepdims=True)
    return jnp.exp2(s).astype(BF16)


def _score_bound(q_gain, k_gain, head_dim):
    return (1.02 * LOG2E * math.sqrt(head_dim)) * jnp.max(jnp.abs(q_gain), axis=-1) * jnp.max(jnp.abs(k_gain), axis=-1)


def _attn_branches(with_ctx, bound_ref, run, k_ref, v_ref):
    def on_keys(rows):
        small = bound_ref[0] <= SAFE_LOG2_RANGE

        @pl.when(small)
        def _():
            run(k_ref[0:rows, :], v_ref[0:rows, :], False)

        @pl.when(jnp.logical_not(small))
        def _():
            run(k_ref[0:rows, :], v_ref[0:rows, :], True)

    if not with_ctx:
        on_keys(SEG_ROWS)
        return
    t = pl.program_id(2)

    @pl.when(t == 0)
    def _():
        on_keys(TILE)

    @pl.when(t > 0)
    def _():
        on_keys(SEG_ROWS)


def _attn_maps(with_ctx):
    if with_ctx:
        m = lambda b, h, t: (b * SEG_TILES + t, h)
        return SEG_TILES, m, m
    nq = SEG_TILES - 1
    return nq, (lambda b, h, t: (b * SEG_TILES + 1 + t, h)), (lambda b, h, t: (b * nq + t, h))


def _lane_masks():
    lane = np.arange(LANE)
    even = (lane // 32) % 2 == 0
    rows = [even, ~even, lane < 64, lane >= 64, lane == 64, lane == 0, lane == DIFF_V_DIM, lane < 0]
    return jnp.asarray(np.stack(rows).astype(np.float32))


def _mask_row(lm_ref, r):
    return lm_ref[r:r + 1, :].astype(BF16)


def _diff_attn_kernel(lam_init, with_ctx, bound_ref, q_ref, k_ref, v_ref, lam_ref, sg_ref, lm_ref, o_ref):
    def run(kk, vv, stabilise):
        q = q_ref[...]
        v1 = vv + _mask_row(lm_ref, 6)
        o1 = _dot(_exp_scores(_dot_nt(q, kk * _mask_row(lm_ref, 0)), stabilise), v1)
        o2 = _dot(_exp_scores(_dot_nt(q, kk * _mask_row(lm_ref, 1)), stabilise), v1)
        lam = _lambda_value(lam_ref, lam_init)
        o = o1 * (1.0 / o1[:, DIFF_V_DIM:DIFF_V_DIM + 1]) - o2 * (lam / o2[:, DIFF_V_DIM:DIFF_V_DIM + 1])
        o = jnp.where(lax.broadcasted_iota(I32, o.shape, 1) < DIFF_V_DIM, o, 0.0)
        ms = jnp.sum(o * o, axis=-1, keepdims=True) * (1.0 / DIFF_V_DIM)
        o_ref[...] = (o * lax.rsqrt(ms + EPS) * sg_ref[...] * (1.0 - lam_init)).astype(BF16)

    _attn_branches(with_ctx, bound_ref, run, k_ref, v_ref)


def _diff_attention(bound, dq, dk, dv, lam_pad, subln, lam_init, nt, with_ctx):
    nb = nt // SEG_ROWS
    nq, q_map, o_map = _attn_maps(with_ctx)
    kv_map = lambda b, h, t: (b, h)
    const = lambda b, h, t: (0, 0)
    return pl.pallas_call(
        functools.partial(_diff_attn_kernel, lam_init, with_ctx),
        grid=(nb, DIFF_HEADS, nq),
        in_specs=[pl.BlockSpec(memory_space=pltpu.SMEM),
                  pl.BlockSpec((TILE, LANE), q_map), pl.BlockSpec((SEG_ROWS, LANE), kv_map),
                  pl.BlockSpec((SEG_ROWS, LANE), kv_map), pl.BlockSpec((8, LANE), const),
                  pl.BlockSpec((1, LANE), const), pl.BlockSpec((8, LANE), const)],
        out_specs=pl.BlockSpec((TILE, LANE), o_map),
        out_shape=jax.ShapeDtypeStruct((nb * nq * TILE, 512), BF16),
        compiler_params=_cparams(("arbitrary", "arbitrary", "arbitrary")),
        name="diff_attn",
    )(bound, dq, dk, dv, lam_pad, subln, _lane_masks())


def _gqa_attn_kernel(with_ctx, bound_ref, q_ref, k_ref, v_ref, lm_ref, o_ref):
    def run(kk, vv, stabilise):
        q = q_ref[...]
        va = vv * _mask_row(lm_ref, 2) + _mask_row(lm_ref, 4)
        vb = vv * _mask_row(lm_ref, 3) + _mask_row(lm_ref, 5)
        oa = _dot(_exp_scores(_dot_nt(q, kk * _mask_row(lm_ref, 0)), stabilise), va)
        ob = _dot(_exp_scores(_dot_nt(q, kk * _mask_row(lm_ref, 1)), stabilise), vb)
        lane = lax.broadcasted_iota(I32, oa.shape, 1)
        o_ref[...] = jnp.where(lane < GQA_HEAD_DIM, oa * (1.0 / oa[:, GQA_HEAD_DIM:GQA_HEAD_DIM + 1]),
                               ob * (1.0 / ob[:, 0:1])).astype(BF16)

    _attn_branches(with_ctx, bound_ref, run, k_ref, v_ref)


def _gqa_attention(bound, gq, gk, gv, nt, with_ctx):
    nb = nt // SEG_ROWS
    nq, q_map, o_map = _attn_maps(with_ctx)
    kv_map = lambda b, p, t: (b, 0)
    return pl.pallas_call(
        functools.partial(_gqa_attn_kernel, with_ctx),
        grid=(nb, 3, nq),
        in_specs=[pl.BlockSpec(memory_space=pltpu.SMEM),
                  pl.BlockSpec((TILE, LANE), q_map), pl.BlockSpec((SEG_ROWS, LANE), kv_map),
                  pl.BlockSpec((SEG_ROWS, LANE), kv_map), pl.BlockSpec((8, LANE), lambda b, p, t: (0, 0))],
        out_specs=pl.BlockSpec((TILE, LANE), o_map),
        out_shape=jax.ShapeDtypeStruct((nb * nq * TILE, 384), BF16),
        compiler_params=_cparams(("arbitrary", "arbitrary", "arbitrary")),
        name="gqa_attn",
    )(bound, gq, gk, gv, _lane_masks())


def _mlstm_kernel(mqk_ref, mv_ref, mg_ref, mgt_ref, cw_ref, cb_ref, hf_ref, hb_ref, qk_s):
    L = MLSTM_CHUNK
    w0, w1, w2, cb = cw_ref[0:1, :], cw_ref[1:2, :], cw_ref[2:3, :], cb_ref[...]
    rid = lax.broadcasted_iota(I32, (TILE, 512), 0)
    kscale = jnp.where(lax.broadcasted_iota(I32, (1, 512), 1) < 256, 1.0, MLSTM_HEAD_DIM ** -0.5)
    zrow = jnp.zeros((1, 512), F32)
    for c in range(SEG_TILES):
        r0 = c * TILE
        xc = mqk_ref[r0:r0 + TILE, :]
        prev = zrow if c in (0, 1) else mqk_ref[r0 - 1:r0, :]
        nxt = zrow if c in (0, SEG_TILES - 1) else mqk_ref[r0 + TILE:r0 + TILE + 1, :]
        up = jnp.where(rid == 0, prev, pltpu.roll(xc, 1, 0))
        dn = jnp.where(rid == TILE - 1, nxt, pltpu.roll(xc, TILE - 1, 0))
        y = w0 * up + w1 * xc + w2 * dn + cb
        qk_s[r0:r0 + TILE, :] = _silu(y) * kscale

    ti = lax.broadcasted_iota(I32, (L, L), 0)
    si = lax.broadcasted_iota(I32, (L, L), 1)
    tri_le = (si <= ti)
    m_le = tri_le.astype(BF16)
    m_ge = (si >= ti).astype(BF16)
    row2 = lax.broadcasted_iota(I32, (2 * L, L), 0)
    trow = row2 % L
    scol = lax.broadcasted_iota(I32, (2 * L, L), 1)
    top2 = row2 < L
    mask_f = scol <= trow
    mask_b = scol >= trow
    lane_lo = lax.broadcasted_iota(I32, (L, LANE), 1) < L
    lane_lo2 = lax.broadcasted_iota(I32, (2 * L, LANE), 1) < L
    rr = lax.broadcasted_iota(I32, (LANE, LANE), 0)
    cc = lax.broadcasted_iota(I32, (LANE, LANE), 1)
    blockdiag = (rr < L) == (cc < L)
    rows_lo = lax.broadcasted_iota(I32, (LANE, 1), 0) < L
    lane1 = lax.broadcasted_iota(I32, (1, LANE), 1) < L
    top_col = lax.broadcasted_iota(I32, (2 * L, 1), 0) < L

    def chain(c, is_fwd, p, gcol, bcol_all, grow, brow_all, state, out_ref):
        ct, nvec, m0, m1 = state
        h0, h1 = 2 * p, 2 * p + 1
        gi, gf = (0, 4) if is_fwd else (8, 12)
        r0 = pl.multiple_of(c * L, L)
        q128 = qk_s[pl.ds(r0, L), p * LANE:(p + 1) * LANE]
        k128 = qk_s[pl.ds(r0, L), 256 + p * LANE:256 + (p + 1) * LANE]
        v128 = mv_ref[pl.ds(r0, L), p * LANE:(p + 1) * LANE]

        def stack_cols(arr, j0, j1):
            return jnp.concatenate([arr[:, j0:j0 + 1], arr[:, j1:j1 + 1]], axis=0)

        def stack_rows(arr, j0, j1):
            return jnp.where(top2, arr[j0:j0 + 1, :], arr[j1:j1 + 1, :])

        bcol = stack_cols(bcol_all, gf + h0, gf + h1)
        licol = stack_cols(gcol, gi + h0, gi + h1)
        brow = stack_rows(brow_all, gf + h0, gf + h1)
        lirow = stack_rows(grow, gi + h0, gi + h1)
        mcol = jnp.where(top_col, m0, m1)
        dlog = jnp.where(mask_f if is_fwd else mask_b, bcol - brow + lirow, -jnp.inf)
        inter = bcol + mcol
        mt = jnp.maximum(inter, jnp.max(dlog, axis=-1, keepdims=True))
        qb = q128.astype(BF16)
        zq = jnp.zeros_like(qb)
        qstack = jnp.concatenate([jnp.where(lane_lo, qb, zq), jnp.where(lane_lo, zq, qb)], axis=0)
        kb = k128.astype(BF16)
        w = jnp.exp(dlog - mt) * _dot_nt(qstack, kb)
        a = jnp.exp(inter - mt)
        intra = _dot(w.astype(BF16), v128)
        num_intra = jnp.where(lane_lo, intra[:L], intra[L:])
        a128 = jnp.where(lane_lo, a[:L], a[L:])
        num = num_intra + a128 * _dot(qb, ct.astype(BF16))
        sumw = jnp.sum(w, axis=-1, keepdims=True)
        qn = q128 * nvec
        qn0 = jnp.sum(jnp.where(lane_lo, qn, 0.0), axis=-1, keepdims=True)
        qn1 = jnp.sum(jnp.where(lane_lo, 0.0, qn), axis=-1, keepdims=True)
        den = jnp.where(lane_lo, sumw[:L] + a[:L] * qn0, sumw[L:] + a[L:] * qn1)
        mt128 = jnp.where(lane_lo, mt[:L], mt[L:])
        out_ref[pl.ds(r0, L), p * LANE:(p + 1) * LANE] = num / jnp.maximum(jnp.abs(den), jnp.exp(-mt128))
        e0 = (L - 1) if is_fwd else 0
        bend0 = bcol[e0:e0 + 1, :]
        bend1 = bcol[L + e0:L + e0 + 1, :]
        bend = jnp.where(top_col, bend0, bend1)
        g = bend - bcol + licol
        m0n = jnp.maximum(bend0 + m0, jnp.max(g[:L], axis=0, keepdims=True))
        m1n = jnp.maximum(bend1 + m1, jnp.max(g[L:], axis=0, keepdims=True))
        ws = jnp.exp(g - jnp.where(top_col, m0n, m1n))
        ae0 = jnp.exp(bend0 + m0 - m0n)
        ae1 = jnp.exp(bend1 + m1 - m1n)
        ws128 = jnp.where(lane_lo, ws[:L], ws[L:])
        vw = (v128.astype(F32) * ws128).astype(BF16)
        upd = _dot_tn(kb, vw)
        ct_new = jnp.where(rows_lo, ae0, ae1) * ct + jnp.where(blockdiag, upd, 0.0)
        n_new = jnp.where(lane1, ae0, ae1) * nvec + jnp.sum(k128 * ws128, axis=0, keepdims=True)
        return ct_new, n_new, m0n, m1n

    def body(i, carry):
        cf = i
        cbk = jnp.where(i < CTX_CHUNKS, CTX_CHUNKS - 1 - i, N_CHUNKS + CTX_CHUNKS - 1 - i)
        new = []
        for d, (c, out_ref) in enumerate(((cf, hf_ref), (cbk, hb_ref))):
            is_fwd = d == 0
            r0 = pl.multiple_of(c * L, L)
            gcol = mg_ref[pl.ds(r0, L), :]
            grow = mgt_ref[0, c]
            lf_col = _log_sigmoid(gcol)
            lf_row = _log_sigmoid(grow)
            if is_fwd:
                bcol_all = _exact_by_dot_f32(m_le, lf_col)
                brow_all = _dot_f32_by_exact(lf_row, m_ge)
            else:
                bcol_all = _exact_by_dot_f32(m_ge, lf_col)
                brow_all = _dot_f32_by_exact(lf_row, m_le)
            for p in range(2):
                st = carry[d * 2 + p]
                new.append(chain(c, is_fwd, p, gcol, bcol_all, grow, brow_all, st, out_ref))
        return tuple(new)

    z = (jnp.zeros((LANE, LANE), F32), jnp.zeros((1, LANE), F32), jnp.zeros((1, 1), F32), jnp.zeros((1, 1), F32))
    lax.fori_loop(0, N_CHUNKS, body, (z, z, z, z))


def _mlstm(mqk, mv, mg, mgt, conv_w, conv_b, nt):
    nb = nt // SEG_ROWS
    blk = lambda w: pl.BlockSpec((SEG_ROWS, w), lambda b: (b, 0))
    return pl.pallas_call(
        _mlstm_kernel,
        grid=(nb,),
        in_specs=[blk(512), blk(256), blk(LANE),
                  pl.BlockSpec((1, N_CHUNKS, 16, MLSTM_CHUNK), lambda b: (b, 0, 0, 0)),
                  pl.BlockSpec((8, 512), lambda b: (0, 0)), pl.BlockSpec((1, 512), lambda b: (0, 0))],
        out_specs=[blk(256), blk(256)],
        out_shape=[jax.ShapeDtypeStruct((nt, 256), F32), jax.ShapeDtypeStruct((nt, 256), F32)],
        scratch_shapes=[pltpu.VMEM((SEG_ROWS, 512), F32)],
        compiler_params=_cparams(("arbitrary",)),
        name="mlstm",
    )(mqk, mv, mg, mgt, conv_w, conv_b)


def _outproj_kernel(do_ref, go_ref, hf_ref, hb_ref, mo_ref, x_ref, mod_ref, n2_ref, hn_ref, wo_ref, wr_hi_ref, wr_lo_ref,
                    rb_ref, hm_ref, xm_ref, f_ref, ri_ref, rg_ref, cnt_ref, carry):
    i = pl.program_id(0)

    @pl.when(i == 0)
    def _():
        carry[...] = jnp.zeros_like(carry)

    hsum = hf_ref[...] + hb_ref[...]
    ssq = _dot_f32_by_exact(hsum * hsum, hm_ref[...])
    ml = hsum * lax.rsqrt(ssq * (1.0 / MLSTM_HEAD_DIM) + EPS) * hn_ref[...] * _sigmoid(mo_ref[...])
    acc = _dot(do_ref[...], wo_ref[0:512, :])
    acc += _dot(go_ref[...], wo_ref[512:896, :])
    acc += _dot(ml.astype(BF16), wo_ref[896:MIX_ROWS, :])
    g1 = mod_ref[0, :, 2 * D_MODEL:3 * D_MODEL]
    sh2 = mod_ref[0, :, 3 * D_MODEL:4 * D_MODEL]
    sc2 = mod_ref[0, :, 4 * D_MODEL:5 * D_MODEL]
    x = x_ref[...] + g1 * acc
    xm_ref[...] = x
    xn = x * lax.rsqrt(jnp.mean(x * x, axis=-1, keepdims=True) + EPS) * n2_ref[...]
    f = xn * (1.0 + sc2) + sh2
    f_ref[...] = f
    f1, f2, _ = _split3(f)
    logits = _dot(f1, wr_hi_ref[...]) + _dot(f2, wr_hi_ref[...]) + _dot(f1, wr_lo_ref[...]) + rb_ref[...]
    lane = lax.broadcasted_iota(I32, logits.shape, 1)
    neg = jnp.float32(-jnp.inf)
    big = jnp.int32(1 << 20)
    is_g = lane < MOE_GROUPS
    lg = jnp.where(is_g, logits, neg)
    gmax = jnp.max(lg, axis=-1, keepdims=True)
    g_top = 1.0 / jnp.sum(jnp.exp(lg - gmax), axis=-1, keepdims=True)
    g_idx = jnp.min(jnp.where(lg == gmax, lane, big), axis=-1, keepdims=True)
    in_grp = (lane >= MOE_GROUPS) & (lane < MOE_GROUPS + MOE_EXPERTS) & ((lane - MOE_GROUPS) // MOE_EPG == g_idx)
    le = jnp.where(in_grp, logits, neg)
    v1 = jnp.max(le, axis=-1, keepdims=True)
    l1 = jnp.min(jnp.where(le == v1, lane, big), axis=-1, keepdims=True)
    le2 = jnp.where(lane == l1, neg, le)
    v2 = jnp.max(le2, axis=-1, keepdims=True)
    l2 = jnp.min(jnp.where(le2 == v2, lane, big), axis=-1, keepdims=True)
    ex = jnp.exp(v2 - v1)
    gate1 = g_top / (1.0 + ex)
    gate2 = gate1 * ex
    oh1 = (lane == l1)
    oh2 = (lane == l2)
    both = oh1.astype(BF16) + oh2.astype(BF16)
    ri_ = lax.broadcasted_iota(I32, (TILE, TILE), 0)
    ci_ = lax.broadcasted_iota(I32, (TILE, TILE), 1)
    before = _dot((ci_ < ri_).astype(BF16), both) + carry[0:1, :]
    rank1 = jnp.sum(jnp.where(oh1, before, 0.0), axis=-1, keepdims=True)
    rank2 = jnp.sum(jnp.where(oh2, before, 0.0), axis=-1, keepdims=True)
    carry[0:1, :] = carry[0:1, :] + jnp.sum(both.astype(F32), axis=0, keepdims=True)
    cnt_ref[...] = jnp.broadcast_to(carry[0:1, :], cnt_ref.shape)
    ri = jnp.where(lane == 0, l1 - MOE_GROUPS, jnp.where(lane == 1, l2 - MOE_GROUPS,
         jnp.where(lane == 2, rank1.astype(I32), jnp.where(lane == 3, rank2.astype(I32), 0))))
    ri_ref[...] = ri
    rg_ref[...] = jnp.where(lane == 0, gate1, jnp.where(lane == 1, gate2, 0.0))


def _outproj(do, go, hf, hb, mo, x, mod_l, n2g, hn, w_out_l, wr_hi, wr_lo, rb, head_mat, nt, with_ctx):
    nb = nt // SEG_ROWS
    nq, off = (SEG_TILES, 0) if with_ctx else (SEG_TILES - 1, 1)
    n_steps = nb * nq
    n_out = n_steps * TILE
    in_map = lambda i: (_seg_tile(i, nq, off), 0)
    out_map = lambda i: (i, 0)

    def full(shape):
        return pl.BlockSpec(shape, lambda i: (0,) * len(shape))

    in_specs = [pl.BlockSpec((TILE, 512), out_map), pl.BlockSpec((TILE, 384), out_map), pl.BlockSpec((TILE, 256), in_map),
                pl.BlockSpec((TILE, 256), in_map), pl.BlockSpec((TILE, 256), in_map), pl.BlockSpec((TILE, D_MODEL), in_map),
                pl.BlockSpec((1, 1, 6 * D_MODEL), lambda i: (_mod_row(i, nq, off), 0, 0)),
                full((1, D_MODEL)), full((1, 256)), full((MIX_ROWS, D_MODEL)), full((D_MODEL, LANE)), full((D_MODEL, LANE)),
                full((1, LANE)), full((256, 256))]
    out_specs = [pl.BlockSpec((TILE, D_MODEL), out_map), pl.BlockSpec((TILE, D_MODEL), out_map),
                 pl.BlockSpec((TILE, LANE), out_map), pl.BlockSpec((TILE, LANE), out_map), pl.BlockSpec((8, LANE), lambda i: (0, 0))]
    out_shape = [jax.ShapeDtypeStruct((n_out, D_MODEL), F32), jax.ShapeDtypeStruct((n_out, D_MODEL), F32),
                 jax.ShapeDtypeStruct((n_out, LANE), I32), jax.ShapeDtypeStruct((n_out, LANE), F32),
                 jax.ShapeDtypeStruct((8, LANE), F32)]
    return pl.pallas_call(
        _outproj_kernel,
        grid=(n_steps,),
        in_specs=in_specs,
        out_specs=out_specs,
        out_shape=out_shape,
        scratch_shapes=[pltpu.VMEM((8, LANE), F32)],
        compiler_params=_cparams(("arbitrary",)),
        name="outproj_route",
    )(do, go, hf, hb, mo, x, mod_l, n2g, hn, w_out_l, wr_hi, wr_lo, rb, head_mat)


DMA_UNROLL = 8


def _for_rows(n, fn):
    groups = lax.shift_right_logical(n, int(math.log2(DMA_UNROLL)))

    def group(g, _):
        for u in range(DMA_UNROLL):
            fn(g * DMA_UNROLL + u)
        return 0

    def single(r, _):
        fn(r)
        return 0

    lax.fori_loop(0, groups, group, 0)
    lax.fori_loop(groups * DMA_UNROLL, n, single, 0)


def _expert_kernel(be_ref, nv_ref, src_ref, srcn_ref, dst_ref, f_hbm, w1_ref, w3_ref, w2_ref, y_hbm,
                   xbuf, ybuf, w1s, w3s, w2s, gsem, ssem):
    i = pl.program_id(0)
    nv = nv_ref[i]
    nv_next = nv_ref[i + 1]
    slot = i % 2

    def gather_copy(idx_ref, s, r):
        return pltpu.make_async_copy(f_hbm.at[pl.ds(idx_ref[0, 0, r], 1)], xbuf.at[s, pl.ds(r, 1)], gsem.at[s])

    def scatter_copy(r):
        pair = dst_ref[0, 0, r]
        return pltpu.make_async_copy(ybuf.at[pl.ds(r, 1)],
                                     y_hbm.at[pair & 1, pl.ds(lax.shift_right_logical(pair, 1), 1)], ssem)

    def wait_rows(n, copy_of):
        whole = lax.shift_right_logical(n, 3) * 8

        @pl.when(whole > 0)
        def _():
            copy_of(pl.multiple_of(whole, 8)).wait()

        def single(r, _):
            copy_of(1).wait()
            return 0

        lax.fori_loop(whole, n, single, 0)

    def gather_wait(n):
        wait_rows(n, lambda m: pltpu.make_async_copy(f_hbm.at[pl.ds(0, m)], xbuf.at[slot, pl.ds(0, m)], gsem.at[slot]))

    def scatter_wait(n):
        wait_rows(n, lambda m: pltpu.make_async_copy(ybuf.at[pl.ds(0, m)], y_hbm.at[0, pl.ds(0, m)], ssem))

    @pl.when(i == 0)
    def _():
        xbuf[...] = jnp.zeros_like(xbuf)
        _for_rows(nv, lambda r: gather_copy(src_ref, 0, r).start())

    @pl.when(nv_next > 0)
    def _():
        _for_rows(nv_next, lambda r: gather_copy(srcn_ref, 1 - slot, r).start())

    @pl.when((nv > 0) & ((i == 0) | (be_ref[i] != be_ref[jnp.maximum(i - 1, 0)])))
    def _():
        w1s[...] = w1_ref[0].astype(BF16)
        w3s[...] = w3_ref[0].astype(BF16)
        w2s[...] = w2_ref[0].astype(BF16)

    @pl.when(nv > 0)
    def _():
        gather_wait(nv)
        xb = xbuf[slot].astype(BF16)
        hh = _silu(_dot(xb, w1s[...])) * _dot(xb, w3s[...])
        y = _dot(hh.astype(BF16), w2s[...])

        @pl.when(i > 0)
        def _():
            scatter_wait(nv_ref[jnp.maximum(i - 1, 0)])

        ybuf[...] = y
        _for_rows(nv, lambda r: scatter_copy(r).start())

        @pl.when(nv_next == 0)
        def _():
            scatter_wait(nv)


def _experts(f, row_src, row_dst, blk_expert, blk_valid, w1, w3, w2, n_tok):
    n_blk = row_src.shape[0]
    idx_spec = lambda fn: pl.BlockSpec((1, 1, MOE_TILE), fn, memory_space=pltpu.SMEM)
    w_spec = lambda shape: pl.BlockSpec((1,) + shape, lambda i, be, nv: (be[i], 0, 0))
    grid_spec = pltpu.PrefetchScalarGridSpec(
        num_scalar_prefetch=2,
        grid=(n_blk,),
        in_specs=[idx_spec(lambda i, be, nv: (i, 0, 0)),
                  idx_spec(lambda i, be, nv: (jnp.minimum(i + 1, n_blk - 1), 0, 0)),
                  idx_spec(lambda i, be, nv: (i, 0, 0)),
                  pl.BlockSpec(memory_space=pl.ANY),
                  w_spec((D_MODEL, MOE_HIDDEN)), w_spec((D_MODEL, MOE_HIDDEN)), w_spec((MOE_HIDDEN, D_MODEL))],
        out_specs=pl.BlockSpec(memory_space=pl.ANY),
        scratch_shapes=[pltpu.VMEM((2, MOE_TILE, D_MODEL), F32), pltpu.VMEM((MOE_TILE, D_MODEL), F32),
                        pltpu.VMEM((D_MODEL, MOE_HIDDEN), BF16), pltpu.VMEM((D_MODEL, MOE_HIDDEN), BF16),
                        pltpu.VMEM((MOE_HIDDEN, D_MODEL), BF16),
                        pltpu.SemaphoreType.DMA((2,)), pltpu.SemaphoreType.DMA(())],
    )
    return pl.pallas_call(
        _expert_kernel,
        grid_spec=grid_spec,
        out_shape=jax.ShapeDtypeStruct((2, n_tok, D_MODEL), F32),
        compiler_params=_cparams(("arbitrary",)),
        name="experts",
    )(blk_expert, blk_valid, row_src, row_src, row_dst, f, w1, w3, w2)


def _route_plan(ri, counts, n_tok):
    n_blk = (2 * n_tok) // MOE_TILE + MOE_EXPERTS
    p_rows = n_blk * MOE_TILE
    cnt = counts[0, MOE_GROUPS:MOE_GROUPS + MOE_EXPERTS].astype(I32)
    padded = (cnt + MOE_TILE - 1) // MOE_TILE * MOE_TILE
    pad_end = jnp.cumsum(padded)
    pad_start = pad_end - padded
    dest = pad_start[ri[:, 0:2]] + ri[:, 2:4]
    row_dst = jnp.zeros((p_rows,), I32).at[dest.reshape(-1)].set(jnp.arange(2 * n_tok, dtype=I32))
    row_src = row_dst // 2
    blk_start = jnp.arange(n_blk + 1, dtype=I32) * MOE_TILE
    blk_expert = jnp.minimum(jnp.sum(pad_end[None, :] <= blk_start[:, None], axis=-1), MOE_EXPERTS - 1).astype(I32)
    in_expert = blk_start - pad_start[blk_expert]
    blk_valid = jnp.where(blk_start < pad_end[-1], jnp.clip(cnt[blk_expert] - in_expert, 0, MOE_TILE), 0).astype(I32)
    return (row_src.reshape(n_blk, 1, MOE_TILE), row_dst.reshape(n_blk, 1, MOE_TILE), blk_expert[:n_blk], blk_valid)


def _final_kernel(x_ref, y_ref, rg_ref, mod_ref, o_ref):
    g2 = mod_ref[0, :, 5 * D_MODEL:6 * D_MODEL]
    rg = rg_ref[...]
    o_ref[...] = x_ref[...] + g2 * (rg[:, 0:1] * y_ref[0] + rg[:, 1:2] * y_ref[1])


def _final(x, y_flat, rg, mod_l, n_tok):
    nq = SEQ // TILE
    tile_map = lambda i: (i, 0)
    return pl.pallas_call(
        _final_kernel,
        grid=(n_tok // TILE,),
        in_specs=[pl.BlockSpec((TILE, D_MODEL), tile_map), pl.BlockSpec((2, TILE, D_MODEL), lambda i: (0, i, 0)),
                  pl.BlockSpec((TILE, LANE), tile_map), pl.BlockSpec((1, 1, 6 * D_MODEL), lambda i: (i // nq, 0, 0))],
        out_specs=pl.BlockSpec((TILE, D_MODEL), tile_map),
        out_shape=jax.ShapeDtypeStruct((n_tok, D_MODEL), F32),
        compiler_params=_cparams(("arbitrary",)),
        name="final_residual",
    )(x, y_flat, rg, mod_l)


def _diff_cols(w, heads):
    lead = w.shape[:-1]
    n = len(lead)
    w = w.reshape(lead + (heads, 2, 2, 2, 12))
    w = w.transpose(tuple(range(n)) + (n, n + 3, n + 1, n + 2, n + 4))
    w = w.reshape(lead + (heads, 4, 24))
    w = jnp.pad(w, [(0, 0)] * (n + 2) + [(0, 8)])
    return w.reshape(lead + (heads * LANE,))


def _gqa_cols(w, groups):
    lead = w.shape[:-1]
    n = len(lead)
    w = w.reshape(lead + (2, groups, 2, 2, 16))
    w = w.transpose(tuple(range(n)) + (n + 1, n + 3, n, n + 2, n + 4))
    return w.reshape(lead + (groups * LANE,))


def _in_weight(w_in):
    pad_last = lambda a, k: jnp.pad(a, [(0, 0)] * (a.ndim - 1) + [(0, k)])
    lead = w_in.shape[:-1]
    dv = pad_last(w_in[..., 768:1152].reshape(lead + (DIFF_HEADS, DIFF_V_DIM)), LANE - DIFF_V_DIM)
    parts = [_diff_cols(w_in[..., 0:384], DIFF_HEADS), _diff_cols(w_in[..., 384:768], DIFF_HEADS),
             dv.reshape(lead + (DIFF_HEADS * LANE,)), _gqa_cols(w_in[..., 1152:1536], 3), _gqa_cols(w_in[..., 1536:1664], 1),
             w_in[..., 1664:2816], pad_last(w_in[..., 2816:2832], LANE - 16)]
    return jnp.concatenate(parts, axis=-1).astype(BF16)


def _out_weight(w_out):
    nl = w_out.shape[0]
    diff = jnp.pad(w_out[:, 0:384].reshape(nl, DIFF_HEADS, DIFF_V_DIM, D_MODEL), ((0, 0), (0, 0), (0, LANE - DIFF_V_DIM), (0, 0)))
    gqa = w_out[:, 384:768].reshape(nl, 2, 3, GQA_HEAD_DIM, D_MODEL).transpose(0, 2, 1, 3, 4)
    return jnp.concatenate([diff.reshape(nl, 512, D_MODEL), gqa.reshape(nl, 384, D_MODEL), w_out[:, 768:]], axis=1).astype(BF16)


def kernel(x, c, ctx, c_ctx, norm1_g, norm2_g, w_mod, b_mod, w_in, w_out, diff_q_norm, diff_k_norm, diff_lambda, diff_subln, gqa_q_norm, gqa_k_norm, mlstm_conv_w, mlstm_conv_b, mlstm_gate_b, mlstm_head_norm, moe_wg, moe_bg, moe_we, moe_be, moe_w1, moe_w3, moe_w2):
    B = x.shape[0]
    nt = B * SEG_ROWS
    n_lat = B * SEQ

    w_in_r = _in_weight(w_in)
    w_out_r = _out_weight(w_out)
    twice = lambda g: jnp.concatenate([g, g], axis=-1)
    gq_d = _diff_cols(twice(diff_q_norm), 1) * (LOG2E * DIFF_HEAD_DIM ** -0.5)
    gk_d = _diff_cols(twice(diff_k_norm), 1)
    gq_g = _gqa_cols(twice(gqa_q_norm), 1) * (LOG2E * GQA_HEAD_DIM ** -0.5)
    gk_g = _gqa_cols(twice(gqa_k_norm), 1)
    bound_d = _score_bound(diff_q_norm, diff_k_norm, DIFF_HEAD_DIM).reshape(DEPTH, 1)
    bound_g = _score_bound(gqa_q_norm, gqa_k_norm, GQA_HEAD_DIM).reshape(DEPTH, 1)
    gains = jnp.concatenate([jnp.tile(gq_d, (1, 4)), jnp.tile(gk_d, (1, 4)), jnp.tile(gq_g, (1, 3)), gk_g], axis=1)
    gains = gains.reshape(DEPTH, 1, 1536)
    gate_b = jnp.pad(mlstm_gate_b, ((0, 0), (0, LANE - 16))).reshape(DEPTH, 1, LANE)
    subln = jnp.pad(diff_subln, ((0, 0), (0, LANE - DIFF_V_DIM))).reshape(DEPTH, 1, LANE)
    lam_pad = jnp.pad(diff_lambda, ((0, 0), (0, 4), (0, LANE - DIFF_HEAD_DIM)))
    conv_w = jnp.pad(mlstm_conv_w, ((0, 0), (0, 5), (0, 0)))
    conv_b = mlstm_conv_b.reshape(DEPTH, 1, 512)
    w_r = jnp.pad(jnp.concatenate([moe_wg, moe_we], axis=2), ((0, 0), (0, 0), (0, LANE - 36)))
    wr_hi = w_r.astype(BF16)
    wr_lo = (w_r - wr_hi.astype(F32)).astype(BF16)
    rb = jnp.pad(jnp.concatenate([moe_bg, moe_be], axis=1), ((0, 0), (0, LANE - 36))).reshape(DEPTH, 1, LANE)
    pair_mat = jnp.asarray(_PAIR_MAT, BF16)
    head_mat = jnp.asarray((np.arange(256)[:, None] // 64 == np.arange(256)[None, :] // 64).astype(np.float32), BF16)
    tabs = _rope_tables(12, 24) + _rope_tables(16, 32)

    xa = jnp.concatenate([ctx, x], axis=1).reshape(nt, D_MODEL)
    assert B <= 8, "row 8 of the modulation table is reserved for the context conditioning"
    cvec = jnp.concatenate([c, jnp.zeros((8 - B, D_MODEL), F32), c_ctx[None, :], jnp.zeros((7, D_MODEL), F32)], axis=0)
    mod = _modulation(cvec, w_mod, b_mod).reshape(DEPTH, 16, 1, 6 * D_MODEL)

    moe_in = None
    for l in range(DEPTH):
        last = l == DEPTH - 1
        lam_init = 0.8 - 0.6 * math.exp(-0.3 * l)
        xa, (dq, dk, dv, gq, gk, gv, mqk, mv, mo, mg) = _inproj(
            xa, moe_in, mod[l], norm1_g[l].reshape(1, D_MODEL), w_in_r[l], gains[l], gate_b[l], tabs, pair_mat, nt)
        do = _diff_attention(bound_d[l], dq, dk, dv, lam_pad[l], subln[l], lam_init, nt, with_ctx=not last)
        go = _gqa_attention(bound_g[l], gq, gk, gv, nt, with_ctx=not last)
        mgt = mg[:, :16].reshape(B, N_CHUNKS, MLSTM_CHUNK, 16).transpose(0, 1, 3, 2)
        hf, hb = _mlstm(mqk, mv, mg, mgt, conv_w[l], conv_b[l], nt)
        xm, f, ri, rg, counts = _outproj(do, go, hf, hb, mo, xa, mod[l], norm2_g[l].reshape(1, D_MODEL),
                                         mlstm_head_norm[l].reshape(1, 256), w_out_r[l], wr_hi[l], wr_lo[l], rb[l],
                                         head_mat, nt, with_ctx=not last)
        n_tok = n_lat if last else nt
        row_src, row_dst, blk_expert, blk_valid = _route_plan(ri, counts, n_tok)
        y = _experts(f, row_src, row_dst, blk_expert, blk_valid, moe_w1[l], moe_w3[l], moe_w2[l], n_tok)
        if last:
            out = _final(xm, y, rg, mod[l][:B], n_tok)
            return out.reshape(B, SEQ, D_MODEL)
        xa = xm
        moe_in = (y, rg, mod[l])
```

```python
import functools
import math

import numpy as np
import jax
import jax.numpy as jnp
from jax import lax
from jax.experimental import pallas as pl
from jax.experimental.pallas import tpu as pltpu

F32 = jnp.float32
BF16 = jnp.bfloat16
I32 = jnp.int32

D_MODEL = 1024
DEPTH = 4
GRID_W = 64
CTX_LEN = 256
SEQ = 2048
ROPE_THETA = 10000.0
EPS = 1e-6

DIFF_HEADS = 4
DIFF_HEAD_DIM = 48
DIFF_V_DIM = 96
GQA_Q_HEADS = 6
GQA_KV_HEADS = 2
GQA_HEAD_DIM = 64
MLSTM_HEADS = 4
MLSTM_HEAD_DIM = 64
MLSTM_CHUNK = 64
IN_WIDTH = 2832
MOE_GROUPS = 4
MOE_EPG = 8
MOE_EXPERTS = 32
MOE_HIDDEN = 512

LANE = 128
TILE = 256
SEG_TILES = (CTX_LEN + SEQ) // TILE
SEG_ROWS = CTX_LEN + SEQ
N_CHUNKS = SEG_ROWS // MLSTM_CHUNK
CTX_CHUNKS = CTX_LEN // MLSTM_CHUNK
MOE_TILE = 256
VMEM_LIMIT = 56 * 1024 * 1024

C_DQ, C_DK, C_DV, C_GQ, C_GK, C_GV, C_MQK, C_MV, C_MO, C_MG, C_END = (
    0, 512, 1024, 1536, 1920, 2048, 2176, 2688, 2944, 3200, 3328)
MIX_ROWS = 512 + 384 + 256


def _cparams(sem):
    return pltpu.CompilerParams(dimension_semantics=sem, vmem_limit_bytes=VMEM_LIMIT)


_LANE2 = np.arange(2 * LANE)
_PAIR_MAT = ((_LANE2[:, None] // LANE == _LANE2[None, :] // LANE)
             & ((_LANE2[:, None] // 32) % 2 == (_LANE2[None, :] // 32) % 2)).astype(np.float32)


def _rope_tables(nf, pad_from):
    t = jnp.arange(SEQ, dtype=I32)
    rows = (t // GRID_W).astype(F32)
    cols = (t % GRID_W).astype(F32)
    freqs = ROPE_THETA ** (-jnp.arange(nf, dtype=F32) / nf)
    lane = np.arange(LANE)
    i = lane % 32
    typ = (lane // 32) // 2
    use_rows = i < nf
    fidx = np.where(use_rows, i, i - nf)
    valid = i < pad_from
    fidx = np.where(valid, fidx, 0)
    ang = jnp.where(jnp.asarray(use_rows)[None, :], rows[:, None], cols[:, None]) * freqs[jnp.asarray(fidx)][None, :]
    cos = jnp.where(jnp.asarray(valid)[None, :], jnp.cos(ang), 1.0)
    sin = jnp.where(jnp.asarray(valid)[None, :], jnp.sin(ang), 0.0)
    sin = sin * jnp.asarray(np.where(typ == 0, -1.0, 1.0), F32)[None, :]
    cos = jnp.concatenate([jnp.ones((CTX_LEN, LANE), F32), cos], axis=0)
    sin = jnp.concatenate([jnp.zeros((CTX_LEN, LANE), F32), sin], axis=0)
    return cos, sin


def _dot(a, b):
    return jnp.dot(a, b, preferred_element_type=F32)


def _dot_nt(a, b):
    return lax.dot_general(a, b, (((1,), (1,)), ((), ())), preferred_element_type=F32)


def _dot_tn(a, b):
    return lax.dot_general(a, b, (((0,), (0,)), ((), ())), preferred_element_type=F32)


def _split3(x):
    x1 = x.astype(BF16)
    r1 = x - x1.astype(F32)
    x2 = r1.astype(BF16)
    x3 = (r1 - x2.astype(F32)).astype(BF16)
    return x1, x2, x3


def _dot_f32_by_exact(x, m):
    x1, x2, x3 = _split3(x)
    return _dot(x1, m) + _dot(x2, m) + _dot(x3, m)


def _dot_f32_by_exact2(x, m):
    x1 = x.astype(BF16)
    x2 = (x - x1.astype(F32)).astype(BF16)
    return _dot(x1, m) + _dot(x2, m)


def _exact_by_dot_f32(m, x):
    x1, x2, x3 = _split3(x)
    return _dot(m, x1) + _dot(m, x2) + _dot(m, x3)


def _sigmoid(x):
    return 1.0 / (1.0 + jnp.exp(-x))


def _silu(x):
    return x * _sigmoid(x)


def _log_sigmoid(x):
    return jnp.minimum(x, 0.0) - jnp.log1p(jnp.exp(-jnp.abs(x)))


def _seg_tile(i, nq, off):
    return (i // nq) * SEG_TILES + off + i % nq


def _mod_row(i, nq, off):
    return jnp.where((off + i % nq) == 0, 8, i // nq)


MOD_BN = 1536


def _mod_kernel(c_ref, w_ref, b_ref, o_ref):
    a = _silu(c_ref[...]).astype(BF16)
    o_ref[0] = _dot(a, w_ref[0].astype(BF16)) + b_ref[0]


def _modulation(cvec, w_mod, b_mod):
    nb = 6 * D_MODEL // MOD_BN
    return pl.pallas_call(
        _mod_kernel,
        grid=(DEPTH, nb),
        in_specs=[pl.BlockSpec((16, D_MODEL), lambda l, n: (0, 0)),
                  pl.BlockSpec((1, D_MODEL, MOD_BN), lambda l, n: (l, 0, n)),
                  pl.BlockSpec((1, 1, MOD_BN), lambda l, n: (l, 0, n))],
        out_specs=pl.BlockSpec((1, 16, MOD_BN), lambda l, n: (l, 0, n)),
        out_shape=jax.ShapeDtypeStruct((DEPTH, 16, 6 * D_MODEL), F32),
        compiler_params=_cparams(("arbitrary", "arbitrary")),
        name="modulation",
    )(cvec, w_mod, b_mod.reshape(DEPTH, 1, 6 * D_MODEL))


def _inproj_kernel(has_moe, *refs):
    if has_moe:
        (x_ref, y_ref, rg_ref, modp_ref, mod_ref, n1_ref, w_ref, gains_ref, gb_ref, cd_ref, sd_ref, cg_ref, sg_ref,
         pm_ref, xo_ref, dq_ref, dk_ref, dv_ref, gq_ref, gk_ref, gv_ref, mqk_ref, mv_ref, mo_ref, mg_ref) = refs
    else:
        (x_ref, mod_ref, n1_ref, w_ref, gains_ref, gb_ref, cd_ref, sd_ref, cg_ref, sg_ref,
         pm_ref, dq_ref, dk_ref, dv_ref, gq_ref, gk_ref, gv_ref, mqk_ref, mv_ref, mo_ref, mg_ref) = refs
    x = x_ref[...]
    if has_moe:
        g2 = modp_ref[0, :, 5 * D_MODEL:6 * D_MODEL]
        rg = rg_ref[...]
        x = x + g2 * (rg[:, 0:1] * y_ref[0] + rg[:, 1:2] * y_ref[1])
        xo_ref[...] = x
    sh = mod_ref[0, :, 0:D_MODEL]
    sc = mod_ref[0, :, D_MODEL:2 * D_MODEL]
    xn = x * lax.rsqrt(jnp.mean(x * x, axis=-1, keepdims=True) + EPS) * n1_ref[...]
    h = (xn * (1.0 + sc) + sh).astype(BF16)
    pm = pm_ref[...]

    def proj(a, b):
        return _dot_nt(h, w_ref[a:b, :])

    def qk_group(col, gain_col, inv_dim, cos, sin, outs):
        y = proj(col, col + 512)
        for half in range(2):
            yh = y[:, half * 256:(half + 1) * 256]
            yn = yh * lax.rsqrt(_dot_f32_by_exact2(yh * yh, pm) * inv_dim + EPS)
            yn = yn * gains_ref[:, gain_col + half * 256:gain_col + (half + 1) * 256]
            for j in range(2):
                yb = yn[:, j * LANE:(j + 1) * LANE]
                ref, off = outs[half * 2 + j]
                ref[:, off:off + LANE] = (yb * cos + pltpu.roll(yb, 64, 1) * sin).astype(BF16)

    cd, sd, cg, sg = cd_ref[...], sd_ref[...], cg_ref[...], sg_ref[...]
    qk_group(C_DQ, 0, 1.0 / DIFF_HEAD_DIM, cd, sd, [(dq_ref, b * LANE) for b in range(4)])
    qk_group(C_DK, 512, 1.0 / DIFF_HEAD_DIM, cd, sd, [(dk_ref, b * LANE) for b in range(4)])
    qk_group(C_GQ, 1024, 1.0 / GQA_HEAD_DIM, cg, sg, [(gq_ref, 0), (gq_ref, LANE), (gq_ref, 2 * LANE), (gk_ref, 0)])
    dv_ref[...] = proj(C_DV, C_GQ).astype(BF16)
    rest = proj(C_GV, C_END)
    gv_ref[...] = rest[:, 0:C_MQK - C_GV].astype(BF16)
    mqk_ref[...] = rest[:, C_MQK - C_GV:C_MV - C_GV]
    mv_ref[...] = rest[:, C_MV - C_GV:C_MO - C_GV].astype(BF16)
    mo_ref[...] = rest[:, C_MO - C_GV:C_MG - C_GV]
    mg_ref[...] = rest[:, C_MG - C_GV:C_END - C_GV] + gb_ref[...]


def _inproj(x, moe_in, mod_l, n1g, w_in_l, gains, gate_b, tabs, pair_mat, nt):
    has_moe = moe_in is not None
    n_tiles = nt // TILE
    tile_map = lambda i: (i, 0)
    mod_spec = pl.BlockSpec((1, 1, 6 * D_MODEL), lambda i: (_mod_row(i, SEG_TILES, 0), 0, 0))
    tab_spec = pl.BlockSpec((TILE, LANE), lambda i: (i % SEG_TILES, 0))

    def full(shape):
        return pl.BlockSpec(shape, lambda i: (0,) * len(shape))

    in_specs = [pl.BlockSpec((TILE, D_MODEL), tile_map)]
    args = [x]
    if has_moe:
        y_flat, rg, mod_prev = moe_in
        in_specs += [pl.BlockSpec((2, TILE, D_MODEL), lambda i: (0, i, 0)), pl.BlockSpec((TILE, LANE), tile_map), mod_spec]
        args += [y_flat, rg, mod_prev]
    in_specs += [mod_spec, full((1, D_MODEL)), full((C_END, D_MODEL)), full((1, 1536)), full((1, LANE)),
                 tab_spec, tab_spec, tab_spec, tab_spec, full((2 * LANE, 2 * LANE))]
    args += [mod_l, n1g, w_in_l, gains, gate_b, tabs[0], tabs[1], tabs[2], tabs[3], pair_mat]

    def o(width, dtype):
        return pl.BlockSpec((TILE, width), tile_map), jax.ShapeDtypeStruct((nt, width), dtype)

    outs = []
    if has_moe:
        outs.append(o(D_MODEL, F32))
    outs += [o(512, BF16), o(512, BF16), o(512, BF16), o(384, BF16), o(LANE, BF16), o(LANE, BF16),
             o(512, F32), o(256, BF16), o(256, F32), o(LANE, F32)]
    res = pl.pallas_call(
        functools.partial(_inproj_kernel, has_moe),
        grid=(n_tiles,),
        in_specs=in_specs,
        out_specs=[s for s, _ in outs],
        out_shape=[s for _, s in outs],
        compiler_params=_cparams(("arbitrary",)),
        name="inproj_moe" if has_moe else "inproj",
    )(*args)
    if has_moe:
        return res[0], res[1:]
    return x, res


def _lambda_value(lam_ref, lam_init):
    lam = lam_ref[...]
    s01 = jnp.sum(lam[0:1] * lam[1:2], axis=-1, keepdims=True)
    s23 = jnp.sum(lam[2:3] * lam[3:4], axis=-1, keepdims=True)
    return jnp.exp(s01) - jnp.exp(s23) + lam_init


LOG2E = 1.4426950408889634
SAFE_LOG2_RANGE = 60.0


def _exp_scores(s, stabilise):
    if stabilise:
        s = s - jnp.max(s, axis=-1, keepdims=True)
    return jnp.exp2(s).astype(BF16)


def _score_bound(q_gain, k_gain, head_dim):
    return (1.02 * LOG2E * math.sqrt(head_dim)) * jnp.max(jnp.abs(q_gain), axis=-1) * jnp.max(jnp.abs(k_gain), axis=-1)


def _attn_branches(with_ctx, bound_ref, run, k_ref, v_ref):
    def on_keys(rows):
        small = bound_ref[0] <= SAFE_LOG2_RANGE

        @pl.when(small)
        def _():
            run(k_ref[0:rows, :], v_ref[0:rows, :], False)

        @pl.when(jnp.logical_not(small))
        def _():
            run(k_ref[0:rows, :], v_ref[0:rows, :], True)

    if not with_ctx:
        on_keys(SEG_ROWS)
        return
    t = pl.program_id(2)

    @pl.when(t == 0)
    def _():
        on_keys(TILE)

    @pl.when(t > 0)
    def _():
        on_keys(SEG_ROWS)


def _attn_maps(with_ctx):
    if with_ctx:
        m = lambda b, h, t: (b * SEG_TILES + t, h)
        return SEG_TILES, m, m
    nq = SEG_TILES - 1
    return nq, (lambda b, h, t: (b * SEG_TILES + 1 + t, h)), (lambda b, h, t: (b * nq + t, h))


def _lane_masks():
    lane = np.arange(LANE)
    even = (lane // 32) % 2 == 0
    rows = [even, ~even, lane < 64, lane >= 64, lane == 64, lane == 0, lane == DIFF_V_DIM, lane < 0]
    return jnp.asarray(np.stack(rows).astype(np.float32))


def _mask_row(lm_ref, r):
    return lm_ref[r:r + 1, :].astype(BF16)


def _diff_attn_kernel(lam_init, with_ctx, bound_ref, q_ref, k_ref, v_ref, lam_ref, sg_ref, lm_ref, o_ref):
    def run(kk, vv, stabilise):
        q = q_ref[...]
        v1 = vv + _mask_row(lm_ref, 6)
        o1 = _dot(_exp_scores(_dot_nt(q, kk * _mask_row(lm_ref, 0)), stabilise), v1)
        o2 = _dot(_exp_scores(_dot_nt(q, kk * _mask_row(lm_ref, 1)), stabilise), v1)
        lam = _lambda_value(lam_ref, lam_init)
        o = o1 * (1.0 / o1[:, DIFF_V_DIM:DIFF_V_DIM + 1]) - o2 * (lam / o2[:, DIFF_V_DIM:DIFF_V_DIM + 1])
        o = jnp.where(lax.broadcasted_iota(I32, o.shape, 1) < DIFF_V_DIM, o, 0.0)
        ms = jnp.sum(o * o, axis=-1, keepdims=True) * (1.0 / DIFF_V_DIM)
        o_ref[...] = (o * lax.rsqrt(ms + EPS) * sg_ref[...] * (1.0 - lam_init)).astype(BF16)

    _attn_branches(with_ctx, bound_ref, run, k_ref, v_ref)


def _diff_attention(bound, dq, dk, dv, lam_pad, subln, lam_init, nt, with_ctx):
    nb = nt // SEG_ROWS
    nq, q_map, o_map = _attn_maps(with_ctx)
    kv_map = lambda b, h, t: (b, h)
    const = lambda b, h, t: (0, 0)
    return pl.pallas_call(
        functools.partial(_diff_attn_kernel, lam_init, with_ctx),
        grid=(nb, DIFF_HEADS, nq),
        in_specs=[pl.BlockSpec(memory_space=pltpu.SMEM),
                  pl.BlockSpec((TILE, LANE), q_map), pl.BlockSpec((SEG_ROWS, LANE), kv_map),
                  pl.BlockSpec((SEG_ROWS, LANE), kv_map), pl.BlockSpec((8, LANE), const),
                  pl.BlockSpec((1, LANE), const), pl.BlockSpec((8, LANE), const)],
        out_specs=pl.BlockSpec((TILE, LANE), o_map),
        out_shape=jax.ShapeDtypeStruct((nb * nq * TILE, 512), BF16),
        compiler_params=_cparams(("arbitrary", "arbitrary", "arbitrary")),
        name="diff_attn",
    )(bound, dq, dk, dv, lam_pad, subln, _lane_masks())


def _gqa_attn_kernel(with_ctx, bound_ref, q_ref, k_ref, v_ref, lm_ref, o_ref):
    def run(kk, vv, stabilise):
        q = q_ref[...]
        va = vv * _mask_row(lm_ref, 2) + _mask_row(lm_ref, 4)
        vb = vv * _mask_row(lm_ref, 3) + _mask_row(lm_ref, 5)
        oa = _dot(_exp_scores(_dot_nt(q, kk * _mask_row(lm_ref, 0)), stabilise), va)
        ob = _dot(_exp_scores(_dot_nt(q, kk * _mask_row(lm_ref, 1)), stabilise), vb)
        lane = lax.broadcasted_iota(I32, oa.shape, 1)
        o_ref[...] = jnp.where(lane < GQA_HEAD_DIM, oa * (1.0 / oa[:, GQA_HEAD_DIM:GQA_HEAD_DIM + 1]),
                               ob * (1.0 / ob[:, 0:1])).astype(BF16)

    _attn_branches(with_ctx, bound_ref, run, k_ref, v_ref)


def _gqa_attention(bound, gq, gk, gv, nt, with_ctx):
    nb = nt // SEG_ROWS
    nq, q_map, o_map = _attn_maps(with_ctx)
    kv_map = lambda b, p, t: (b, 0)
    return pl.pallas_call(
        functools.partial(_gqa_attn_kernel, with_ctx),
        grid=(nb, 3, nq),
        in_specs=[pl.BlockSpec(memory_space=pltpu.SMEM),
                  pl.BlockSpec((TILE, LANE), q_map), pl.BlockSpec((SEG_ROWS, LANE), kv_map),
                  pl.BlockSpec((SEG_ROWS, LANE), kv_map), pl.BlockSpec((8, LANE), lambda b, p, t: (0, 0))],
        out_specs=pl.BlockSpec((TILE, LANE), o_map),
        out_shape=jax.ShapeDtypeStruct((nb * nq * TILE, 384), BF16),
        compiler_params=_cparams(("arbitrary", "arbitrary", "arbitrary")),
        name="gqa_attn",
    )(bound, gq, gk, gv, _lane_masks())


def _mlstm_kernel(mqk_ref, mv_ref, mg_ref, mgt_ref, cw_ref, cb_ref, hf_ref, hb_ref, qk_s):
    L = MLSTM_CHUNK
    w0, w1, w2, cb = cw_ref[0:1, :], cw_ref[1:2, :], cw_ref[2:3, :], cb_ref[...]
    rid = lax.broadcasted_iota(I32, (TILE, 512), 0)
    kscale = jnp.where(lax.broadcasted_iota(I32, (1, 512), 1) < 256, 1.0, MLSTM_HEAD_DIM ** -0.5)
    zrow = jnp.zeros((1, 512), F32)
    for c in range(SEG_TILES):
        r0 = c * TILE
        xc = mqk_ref[r0:r0 + TILE, :]
        prev = zrow if c in (0, 1) else mqk_ref[r0 - 1:r0, :]
        nxt = zrow if c in (0, SEG_TILES - 1) else mqk_ref[r0 + TILE:r0 + TILE + 1, :]
        up = jnp.where(rid == 0, prev, pltpu.roll(xc, 1, 0))
        dn = jnp.where(rid == TILE - 1, nxt, pltpu.roll(xc, TILE - 1, 0))
        y = w0 * up + w1 * xc + w2 * dn + cb
        qk_s[r0:r0 + TILE, :] = _silu(y) * kscale

    ti = lax.broadcasted_iota(I32, (L, L), 0)
    si = lax.broadcasted_iota(I32, (L, L), 1)
    tri_le = (si <= ti)
    m_le = tri_le.astype(BF16)
    m_ge = (si >= ti).astype(BF16)
    row2 = lax.broadcasted_iota(I32, (2 * L, L), 0)
    trow = row2 % L
    scol = lax.broadcasted_iota(I32, (2 * L, L), 1)
    top2 = row2 < L
    mask_f = scol <= trow
    mask_b = scol >= trow
    lane_lo = lax.broadcasted_iota(I32, (L, LANE), 1) < L
    lane_lo256 = lax.broadcasted_iota(I32, (L, 2 * LANE), 1) % LANE < L
    rr = lax.broadcasted_iota(I32, (LANE, LANE), 0)
    cc = lax.broadcasted_iota(I32, (LANE, LANE), 1)
    blockdiag = (rr < L) == (cc < L)
    rows_lo = lax.broadcasted_iota(I32, (LANE, 1), 0) < L
    top_col = lax.broadcasted_iota(I32, (2 * L, 1), 0) < L

    def chain(c, is_fwd, p, gcol, bcol_all, grow, brow_all, state, out_ref):
        ct, nm, m0, m1 = state
        h0, h1 = 2 * p, 2 * p + 1
        gi, gf = (0, 4) if is_fwd else (8, 12)
        r0 = pl.multiple_of(c * L, L)
        q128 = qk_s[pl.ds(r0, L), p * LANE:(p + 1) * LANE]
        k128 = qk_s[pl.ds(r0, L), 256 + p * LANE:256 + (p + 1) * LANE]
        v128 = mv_ref[pl.ds(r0, L), p * LANE:(p + 1) * LANE]

        def stack_cols(arr, j0, j1):
            return jnp.concatenate([arr[:, j0:j0 + 1], arr[:, j1:j1 + 1]], axis=0)

        def stack_rows(arr, j0, j1):
            return jnp.where(top2, arr[j0:j0 + 1, :], arr[j1:j1 + 1, :])

        bcol = stack_cols(bcol_all, gf + h0, gf + h1)
        licol = stack_cols(gcol, gi + h0, gi + h1)
        crow = stack_rows(grow, gi + h0, gi + h1) - stack_rows(brow_all, gf + h0, gf + h1)
        cm = jnp.where(mask_f if is_fwd else mask_b, crow, -jnp.inf)
        mcol = jnp.where(top_col, m0, m1)
        u = jnp.maximum(mcol, jnp.max(cm, axis=-1, keepdims=True))
        qb = q128.astype(BF16)
        zq = jnp.zeros_like(qb)
        qstack = jnp.concatenate([jnp.where(lane_lo, qb, zq), jnp.where(lane_lo, zq, qb)], axis=0)
        kb = k128.astype(BF16)
        w = (jnp.exp(cm - u) * _dot_nt(qstack, kb)).astype(BF16)
        wv = _dot(w, jnp.concatenate([v128, jnp.ones_like(v128)], axis=1))
        wv = jnp.where(lane_lo256, wv[:L], wv[L:])
        qcn = _dot(qb, jnp.concatenate([ct, nm], axis=1).astype(BF16))
        a = jnp.exp(mcol - u)
        emt = jnp.exp(-(bcol + u))
        a128 = jnp.where(lane_lo, a[:L], a[L:])
        emt128 = jnp.where(lane_lo, emt[:L], emt[L:])
        num = wv[:, :LANE] + a128 * qcn[:, :LANE]
        den = wv[:, LANE:] + a128 * qcn[:, LANE:]
        out_ref[pl.ds(r0, L), p * LANE:(p + 1) * LANE] = num / jnp.maximum(jnp.abs(den), emt128)
        e0 = (L - 1) if is_fwd else 0
        bend0 = bcol[e0:e0 + 1, :]
        bend1 = bcol[L + e0:L + e0 + 1, :]
        bend = jnp.where(top_col, bend0, bend1)
        g = bend - bcol + licol
        m0n = jnp.maximum(bend0 + m0, jnp.max(g[:L], axis=0, keepdims=True))
        m1n = jnp.maximum(bend1 + m1, jnp.max(g[L:], axis=0, keepdims=True))
        ws = jnp.exp(g - jnp.where(top_col, m0n, m1n))
        ae0 = jnp.exp(bend0 + m0 - m0n)
        ae1 = jnp.exp(bend1 + m1 - m1n)
        ws128 = jnp.where(lane_lo, ws[:L], ws[L:])
        vw = jnp.concatenate([v128.astype(F32) * ws128, ws128], axis=1).astype(BF16)
        upd = _dot_tn(kb, vw)
        ae = jnp.where(rows_lo, ae0, ae1)
        ct_new = ae * ct + jnp.where(blockdiag, upd[:, :LANE], 0.0)
        nm_new = ae * nm + jnp.where(blockdiag, upd[:, LANE:], 0.0)
        return ct_new, nm_new, m0n, m1n

    def body(i, carry):
        cf = i
        cbk = jnp.where(i < CTX_CHUNKS, CTX_CHUNKS - 1 - i, N_CHUNKS + CTX_CHUNKS - 1 - i)
        new = []
        for d, (c, out_ref) in enumerate(((cf, hf_ref), (cbk, hb_ref))):
            is_fwd = d == 0
            r0 = pl.multiple_of(c * L, L)
            gcol = mg_ref[pl.ds(r0, L), :]
            grow = mgt_ref[0, c]
            lf_col = _log_sigmoid(gcol)
            lf_row = _log_sigmoid(grow)
            if is_fwd:
                bcol_all = _exact_by_dot_f32(m_le, lf_col)
                brow_all = _dot_f32_by_exact(lf_row, m_ge)
            else:
                bcol_all = _exact_by_dot_f32(m_ge, lf_col)
                brow_all = _dot_f32_by_exact(lf_row, m_le)
            for p in range(2):
                st = carry[d * 2 + p]
                new.append(chain(c, is_fwd, p, gcol, bcol_all, grow, brow_all, st, out_ref))
        return tuple(new)

    z = (jnp.zeros((LANE, LANE), F32), jnp.zeros((LANE, LANE), F32), jnp.zeros((1, 1), F32), jnp.zeros((1, 1), F32))
    lax.fori_loop(0, N_CHUNKS, body, (z, z, z, z), unroll=2)


def _mlstm(mqk, mv, mg, mgt, conv_w, conv_b, nt):
    nb = nt // SEG_ROWS
    blk = lambda w: pl.BlockSpec((SEG_ROWS, w), lambda b: (b, 0))
    return pl.pallas_call(
        _mlstm_kernel,
        grid=(nb,),
        in_specs=[blk(512), blk(256), blk(LANE),
                  pl.BlockSpec((1, N_CHUNKS, 16, MLSTM_CHUNK), lambda b: (b, 0, 0, 0)),
                  pl.BlockSpec((8, 512), lambda b: (0, 0)), pl.BlockSpec((1, 512), lambda b: (0, 0))],
        out_specs=[blk(256), blk(256)],
        out_shape=[jax.ShapeDtypeStruct((nt, 256), F32), jax.ShapeDtypeStruct((nt, 256), F32)],
        scratch_shapes=[pltpu.VMEM((SEG_ROWS, 512), F32)],
        compiler_params=_cparams(("arbitrary",)),
        name="mlstm",
    )(mqk, mv, mg, mgt, conv_w, conv_b)


def _outproj_kernel(do_ref, go_ref, hf_ref, hb_ref, mo_ref, x_ref, mod_ref, n2_ref, hn_ref, wo_ref, wr_hi_ref, wr_lo_ref,
                    rb_ref, hm_ref, xm_ref, f_ref, ri_ref, rg_ref, cnt_ref, carry):
    i = pl.program_id(0)

    @pl.when(i == 0)
    def _():
        carry[...] = jnp.zeros_like(carry)

    hsum = hf_ref[...] + hb_ref[...]
    ssq = _dot_f32_by_exact(hsum * hsum, hm_ref[...])
    ml = hsum * lax.rsqrt(ssq * (1.0 / MLSTM_HEAD_DIM) + EPS) * hn_ref[...] * _sigmoid(mo_ref[...])
    acc = _dot(do_ref[...], wo_ref[0:512, :])
    acc += _dot(go_ref[...], wo_ref[512:896, :])
    acc += _dot(ml.astype(BF16), wo_ref[896:MIX_ROWS, :])
    g1 = mod_ref[0, :, 2 * D_MODEL:3 * D_MODEL]
    sh2 = mod_ref[0, :, 3 * D_MODEL:4 * D_MODEL]
    sc2 = mod_ref[0, :, 4 * D_MODEL:5 * D_MODEL]
    x = x_ref[...] + g1 * acc
    xm_ref[...] = x
    xn = x * lax.rsqrt(jnp.mean(x * x, axis=-1, keepdims=True) + EPS) * n2_ref[...]
    f = xn * (1.0 + sc2) + sh2
    f_ref[...] = f
    f1, f2, _ = _split3(f)
    logits = _dot(f1, wr_hi_ref[...]) + _dot(f2, wr_hi_ref[...]) + _dot(f1, wr_lo_ref[...]) + rb_ref[...]
    lane = lax.broadcasted_iota(I32, logits.shape, 1)
    neg = jnp.float32(-jnp.inf)
    big = jnp.int32(1 << 20)
    is_g = lane < MOE_GROUPS
    lg = jnp.where(is_g, logits, neg)
    gmax = jnp.max(lg, axis=-1, keepdims=True)
    g_top = 1.0 / jnp.sum(jnp.exp(lg - gmax), axis=-1, keepdims=True)
    g_idx = jnp.min(jnp.where(lg == gmax, lane, big), axis=-1, keepdims=True)
    in_grp = (lane >= MOE_GROUPS) & (lane < MOE_GROUPS + MOE_EXPERTS) & ((lane - MOE_GROUPS) // MOE_EPG == g_idx)
    le = jnp.where(in_grp, logits, neg)
    v1 = jnp.max(le, axis=-1, keepdims=True)
    l1 = jnp.min(jnp.where(le == v1, lane, big), axis=-1, keepdims=True)
    le2 = jnp.where(lane == l1, neg, le)
    v2 = jnp.max(le2, axis=-1, keepdims=True)
    l2 = jnp.min(jnp.where(le2 == v2, lane, big), axis=-1, keepdims=True)
    ex = jnp.exp(v2 - v1)
    gate1 = g_top / (1.0 + ex)
    gate2 = gate1 * ex
    oh1 = (lane == l1)
    oh2 = (lane == l2)
    both = oh1.astype(BF16) + oh2.astype(BF16)
    ri_ = lax.broadcasted_iota(I32, (TILE, TILE), 0)
    ci_ = lax.broadcasted_iota(I32, (TILE, TILE), 1)
    before = _dot((ci_ < ri_).astype(BF16), both) + carry[0:1, :]
    rank1 = jnp.sum(jnp.where(oh1, before, 0.0), axis=-1, keepdims=True)
    rank2 = jnp.sum(jnp.where(oh2, before, 0.0), axis=-1, keepdims=True)
    carry[0:1, :] = carry[0:1, :] + jnp.sum(both.astype(F32), axis=0, keepdims=True)
    cnt_ref[...] = jnp.broadcast_to(carry[0:1, :], cnt_ref.shape)
    ri = jnp.where(lane == 0, l1 - MOE_GROUPS, jnp.where(lane == 1, l2 - MOE_GROUPS,
         jnp.where(lane == 2, rank1.astype(I32), jnp.where(lane == 3, rank2.astype(I32), 0))))
    ri_ref[...] = ri
    rg_ref[...] = jnp.where(lane == 0, gate1, jnp.where(lane == 1, gate2, 0.0))


def _outproj(do, go, hf, hb, mo, x, mod_l, n2g, hn, w_out_l, wr_hi, wr_lo, rb, head_mat, nt, with_ctx):
    nb = nt // SEG_ROWS
    nq, off = (SEG_TILES, 0) if with_ctx else (SEG_TILES - 1, 1)
    n_steps = nb * nq
    n_out = n_steps * TILE
    in_map = lambda i: (_seg_tile(i, nq, off), 0)
    out_map = lambda i: (i, 0)

    def full(shape):
        return pl.BlockSpec(shape, lambda i: (0,) * len(shape))

    in_specs = [pl.BlockSpec((TILE, 512), out_map), pl.BlockSpec((TILE, 384), out_map), pl.BlockSpec((TILE, 256), in_map),
                pl.BlockSpec((TILE, 256), in_map), pl.BlockSpec((TILE, 256), in_map), pl.BlockSpec((TILE, D_MODEL), in_map),
                pl.BlockSpec((1, 1, 6 * D_MODEL), lambda i: (_mod_row(i, nq, off), 0, 0)),
                full((1, D_MODEL)), full((1, 256)), full((MIX_ROWS, D_MODEL)), full((D_MODEL, LANE)), full((D_MODEL, LANE)),
                full((1, LANE)), full((256, 256))]
    out_specs = [pl.BlockSpec((TILE, D_MODEL), out_map), pl.BlockSpec((TILE, D_MODEL), out_map),
                 pl.BlockSpec((TILE, LANE), out_map), pl.BlockSpec((TILE, LANE), out_map), pl.BlockSpec((8, LANE), lambda i: (0, 0))]
    out_shape = [jax.ShapeDtypeStruct((n_out, D_MODEL), F32), jax.ShapeDtypeStruct((n_out, D_MODEL), F32),
                 jax.ShapeDtypeStruct((n_out, LANE), I32), jax.ShapeDtypeStruct((n_out, LANE), F32),
                 jax.ShapeDtypeStruct((8, LANE), F32)]
    return pl.pallas_call(
        _outproj_kernel,
        grid=(n_steps,),
        in_specs=in_specs,
        out_specs=out_specs,
        out_shape=out_shape,
        scratch_shapes=[pltpu.VMEM((8, LANE), F32)],
        compiler_params=_cparams(("arbitrary",)),
        name="outproj_route",
    )(do, go, hf, hb, mo, x, mod_l, n2g, hn, w_out_l, wr_hi, wr_lo, rb, head_mat)


DMA_UNROLL = 8


def _for_rows(n, fn):
    groups = lax.shift_right_logical(n, int(math.log2(DMA_UNROLL)))

    def group(g, _):
        for u in range(DMA_UNROLL):
            fn(g * DMA_UNROLL + u)
        return 0

    def single(r, _):
        fn(r)
        return 0

    lax.fori_loop(0, groups, group, 0)
    lax.fori_loop(groups * DMA_UNROLL, n, single, 0)


def _expert_kernel(be_ref, nv_ref, src_ref, srcn_ref, dst_ref, f_hbm, w1_ref, w3_ref, w2_ref, y_hbm,
                   xbuf, ybuf, w1s, w3s, w2s, gsem, ssem):
    i = pl.program_id(0)
    nv = nv_ref[i]
    nv_next = nv_ref[i + 1]
    slot = i % 2

    def gather_copy(idx_ref, s, r):
        return pltpu.make_async_copy(f_hbm.at[pl.ds(idx_ref[0, 0, r], 1)], xbuf.at[s, pl.ds(r, 1)], gsem.at[s])

    def scatter_copy(r):
        pair = dst_ref[0, 0, r]
        return pltpu.make_async_copy(ybuf.at[pl.ds(r, 1)],
                                     y_hbm.at[pair & 1, pl.ds(lax.shift_right_logical(pair, 1), 1)], ssem)

    def wait_rows(n, copy_of):
        whole = lax.shift_right_logical(n, 3) * 8

        @pl.when(whole > 0)
        def _():
            copy_of(pl.multiple_of(whole, 8)).wait()

        def single(r, _):
            copy_of(1).wait()
            return 0

        lax.fori_loop(whole, n, single, 0)

    def gather_wait(n):
        wait_rows(n, lambda m: pltpu.make_async_copy(f_hbm.at[pl.ds(0, m)], xbuf.at[slot, pl.ds(0, m)], gsem.at[slot]))

    def scatter_wait(n):
        wait_rows(n, lambda m: pltpu.make_async_copy(ybuf.at[pl.ds(0, m)], y_hbm.at[0, pl.ds(0, m)], ssem))

    @pl.when(i == 0)
    def _():
        xbuf[...] = jnp.zeros_like(xbuf)
        _for_rows(nv, lambda r: gather_copy(src_ref, 0, r).start())

    @pl.when(nv_next > 0)
    def _():
        _for_rows(nv_next, lambda r: gather_copy(srcn_ref, 1 - slot, r).start())

    @pl.when((nv > 0) & ((i == 0) | (be_ref[i] != be_ref[jnp.maximum(i - 1, 0)])))
    def _():
        w1s[...] = w1_ref[0].astype(BF16)
        w3s[...] = w3_ref[0].astype(BF16)
        w2s[...] = w2_ref[0].astype(BF16)

    @pl.when(nv > 0)
    def _():
        gather_wait(nv)
        xb = xbuf[slot].astype(BF16)
        hh = _silu(_dot(xb, w1s[...])) * _dot(xb, w3s[...])
        y = _dot(hh.astype(BF16), w2s[...])

        @pl.when(i > 0)
        def _():
            scatter_wait(nv_ref[jnp.maximum(i - 1, 0)])

        ybuf[...] = y
        _for_rows(nv, lambda r: scatter_copy(r).start())

        @pl.when(nv_next == 0)
        def _():
            scatter_wait(nv)


def _experts(f, row_src, row_dst, blk_expert, blk_valid, w1, w3, w2, n_tok):
    n_blk = row_src.shape[0]
    idx_spec = lambda fn: pl.BlockSpec((1, 1, MOE_TILE), fn, memory_space=pltpu.SMEM)
    w_spec = lambda shape: pl.BlockSpec((1,) + shape, lambda i, be, nv: (be[i], 0, 0))
    grid_spec = pltpu.PrefetchScalarGridSpec(
        num_scalar_prefetch=2,
        grid=(n_blk,),
        in_specs=[idx_spec(lambda i, be, nv: (i, 0, 0)),
                  idx_spec(lambda i, be, nv: (jnp.minimum(i + 1, n_blk - 1), 0, 0)),
                  idx_spec(lambda i, be, nv: (i, 0, 0)),
                  pl.BlockSpec(memory_space=pl.ANY),
                  w_spec((D_MODEL, MOE_HIDDEN)), w_spec((D_MODEL, MOE_HIDDEN)), w_spec((MOE_HIDDEN, D_MODEL))],
        out_specs=pl.BlockSpec(memory_space=pl.ANY),
        scratch_shapes=[pltpu.VMEM((2, MOE_TILE, D_MODEL), F32), pltpu.VMEM((MOE_TILE, D_MODEL), F32),
                        pltpu.VMEM((D_MODEL, MOE_HIDDEN), BF16), pltpu.VMEM((D_MODEL, MOE_HIDDEN), BF16),
                        pltpu.VMEM((MOE_HIDDEN, D_MODEL), BF16),
                        pltpu.SemaphoreType.DMA((2,)), pltpu.SemaphoreType.DMA(())],
    )
    return pl.pallas_call(
        _expert_kernel,
        grid_spec=grid_spec,
        out_shape=jax.ShapeDtypeStruct((2, n_tok, D_MODEL), F32),
        compiler_params=_cparams(("arbitrary",)),
        name="experts",
    )(blk_expert, blk_valid, row_src, row_src, row_dst, f, w1, w3, w2)


def _route_plan(ri, counts, n_tok):
    n_blk = (2 * n_tok) // MOE_TILE + MOE_EXPERTS
    p_rows = n_blk * MOE_TILE
    cnt = counts[0, MOE_GROUPS:MOE_GROUPS + MOE_EXPERTS].astype(I32)
    padded = (cnt + MOE_TILE - 1) // MOE_TILE * MOE_TILE
    pad_end = jnp.cumsum(padded)
    pad_start = pad_end - padded
    dest = pad_start[ri[:, 0:2]] + ri[:, 2:4]
    row_dst = jnp.zeros((p_rows,), I32).at[dest.reshape(-1)].set(jnp.arange(2 * n_tok, dtype=I32))
    row_src = row_dst // 2
    blk_start = jnp.arange(n_blk + 1, dtype=I32) * MOE_TILE
    blk_expert = jnp.minimum(jnp.sum(pad_end[None, :] <= blk_start[:, None], axis=-1), MOE_EXPERTS - 1).astype(I32)
    in_expert = blk_start - pad_start[blk_expert]
    blk_valid = jnp.where(blk_start < pad_end[-1], jnp.clip(cnt[blk_expert] - in_expert, 0, MOE_TILE), 0).astype(I32)
    return (row_src.reshape(n_blk, 1, MOE_TILE), row_dst.reshape(n_blk, 1, MOE_TILE), blk_expert[:n_blk], blk_valid)


def _final_kernel(x_ref, y_ref, rg_ref, mod_ref, o_ref):
    g2 = mod_ref[0, :, 5 * D_MODEL:6 * D_MODEL]
    rg = rg_ref[...]
    o_ref[...] = x_ref[...] + g2 * (rg[:, 0:1] * y_ref[0] + rg[:, 1:2] * y_ref[1])


def _final(x, y_flat, rg, mod_l, n_tok):
    nq = SEQ // TILE
    tile_map = lambda i: (i, 0)
    return pl.pallas_call(
        _final_kernel,
        grid=(n_tok // TILE,),
        in_specs=[pl.BlockSpec((TILE, D_MODEL), tile_map), pl.BlockSpec((2, TILE, D_MODEL), lambda i: (0, i, 0)),
                  pl.BlockSpec((TILE, LANE), tile_map), pl.BlockSpec((1, 1, 6 * D_MODEL), lambda i: (i // nq, 0, 0))],
        out_specs=pl.BlockSpec((TILE, D_MODEL), tile_map),
        out_shape=jax.ShapeDtypeStruct((n_tok, D_MODEL), F32),
        compiler_params=_cparams(("arbitrary",)),
        name="final_residual",
    )(x, y_flat, rg, mod_l)


def _diff_rows(w, heads):
    nl, _, k = w.shape
    w = w.reshape(nl, heads, 2, 2, 2, 12, k)
    w = w.transpose(0, 1, 4, 2, 3, 5, 6)
    w = jnp.pad(w.reshape(nl, heads, 4, 24, k), ((0, 0), (0, 0), (0, 0), (0, 8), (0, 0)))
    return w.reshape(nl, heads * LANE, k)


def _gqa_rows(w, groups):
    nl, _, k = w.shape
    w = w.reshape(nl, 2, groups, 2, 2, 16, k)
    w = w.transpose(0, 2, 4, 1, 3, 5, 6)
    return w.reshape(nl, groups * LANE, k)


def _in_weight(w_in):
    wt = jnp.swapaxes(w_in, 1, 2)
    nl = wt.shape[0]
    dv = jnp.pad(wt[:, 768:1152].reshape(nl, DIFF_HEADS, DIFF_V_DIM, D_MODEL), ((0, 0), (0, 0), (0, LANE - DIFF_V_DIM), (0, 0)))
    parts = [_diff_rows(wt[:, 0:384], DIFF_HEADS), _diff_rows(wt[:, 384:768], DIFF_HEADS),
             dv.reshape(nl, DIFF_HEADS * LANE, D_MODEL), _gqa_rows(wt[:, 1152:1536], 3), _gqa_rows(wt[:, 1536:1664], 1),
             wt[:, 1664:2816], jnp.pad(wt[:, 2816:2832], ((0, 0), (0, LANE - 16), (0, 0)))]
    return jnp.concatenate(parts, axis=1).astype(BF16)


def _lane_gain(g, rows_fn):
    return rows_fn(jnp.concatenate([g, g], axis=-1)[:, :, None], 1)[:, :, 0]


def _out_weight(w_out):
    nl = w_out.shape[0]
    diff = jnp.pad(w_out[:, 0:384].reshape(nl, DIFF_HEADS, DIFF_V_DIM, D_MODEL), ((0, 0), (0, 0), (0, LANE - DIFF_V_DIM), (0, 0)))
    gqa = w_out[:, 384:768].reshape(nl, 2, 3, GQA_HEAD_DIM, D_MODEL).transpose(0, 2, 1, 3, 4)
    return jnp.concatenate([diff.reshape(nl, 512, D_MODEL), gqa.reshape(nl, 384, D_MODEL), w_out[:, 768:]], axis=1).astype(BF16)


def kernel(x, c, ctx, c_ctx, norm1_g, norm2_g, w_mod, b_mod, w_in, w_out, diff_q_norm, diff_k_norm, diff_lambda, diff_subln, gqa_q_norm, gqa_k_norm, mlstm_conv_w, mlstm_conv_b, mlstm_gate_b, mlstm_head_norm, moe_wg, moe_bg, moe_we, moe_be, moe_w1, moe_w3, moe_w2):
    B = x.shape[0]
    nt = B * SEG_ROWS
    n_lat = B * SEQ

    w_in_r = _in_weight(w_in)
    w_out_r = _out_weight(w_out)
    gq_d = _lane_gain(diff_q_norm, _diff_rows) * (LOG2E * DIFF_HEAD_DIM ** -0.5)
    gk_d = _lane_gain(diff_k_norm, _diff_rows)
    gq_g = _lane_gain(gqa_q_norm, _gqa_rows) * (LOG2E * GQA_HEAD_DIM ** -0.5)
    gk_g = _lane_gain(gqa_k_norm, _gqa_rows)
    bound_d = _score_bound(diff_q_norm, diff_k_norm, DIFF_HEAD_DIM).reshape(DEPTH, 1)
    bound_g = _score_bound(gqa_q_norm, gqa_k_norm, GQA_HEAD_DIM).reshape(DEPTH, 1)
    gains = jnp.concatenate([jnp.tile(gq_d, (1, 4)), jnp.tile(gk_d, (1, 4)), jnp.tile(gq_g, (1, 3)), gk_g], axis=1)
    gains = gains.reshape(DEPTH, 1, 1536)
    gate_b = jnp.pad(mlstm_gate_b, ((0, 0), (0, LANE - 16))).reshape(DEPTH, 1, LANE)
    subln = jnp.pad(diff_subln, ((0, 0), (0, LANE - DIFF_V_DIM))).reshape(DEPTH, 1, LANE)
    lam_pad = jnp.pad(diff_lambda, ((0, 0), (0, 4), (0, LANE - DIFF_HEAD_DIM)))
    conv_w = jnp.pad(mlstm_conv_w, ((0, 0), (0, 5), (0, 0)))
    conv_b = mlstm_conv_b.reshape(DEPTH, 1, 512)
    w_r = jnp.pad(jnp.concatenate([moe_wg, moe_we], axis=2), ((0, 0), (0, 0), (0, LANE - 36)))
    wr_hi = w_r.astype(BF16)
    wr_lo = (w_r - wr_hi.astype(F32)).astype(BF16)
    rb = jnp.pad(jnp.concatenate([moe_bg, moe_be], axis=1), ((0, 0), (0, LANE - 36))).reshape(DEPTH, 1, LANE)
    pair_mat = jnp.asarray(_PAIR_MAT, BF16)
    head_mat = jnp.asarray((np.arange(256)[:, None] // 64 == np.arange(256)[None, :] // 64).astype(np.float32), BF16)
    tabs = _rope_tables(12, 24) + _rope_tables(16, 32)

    xa = jnp.concatenate([ctx, x], axis=1).reshape(nt, D_MODEL)
    assert B <= 8, "row 8 of the modulation table is reserved for the context conditioning"
    cvec = jnp.concatenate([c, jnp.zeros((8 - B, D_MODEL), F32), c_ctx[None, :], jnp.zeros((7, D_MODEL), F32)], axis=0)
    mod = _modulation(cvec, w_mod, b_mod).reshape(DEPTH, 16, 1, 6 * D_MODEL)

    moe_in = None
    for l in range(DEPTH):
        last = l == DEPTH - 1
        lam_init = 0.8 - 0.6 * math.exp(-0.3 * l)
        xa, (dq, dk, dv, gq, gk, gv, mqk, mv, mo, mg) = _inproj(
            xa, moe_in, mod[l], norm1_g[l].reshape(1, D_MODEL), w_in_r[l], gains[l], gate_b[l], tabs, pair_mat, nt)
        do = _diff_attention(bound_d[l], dq, dk, dv, lam_pad[l], subln[l], lam_init, nt, with_ctx=not last)
        go = _gqa_attention(bound_g[l], gq, gk, gv, nt, with_ctx=not last)
        mgt = mg[:, :16].reshape(B, N_CHUNKS, MLSTM_CHUNK, 16).transpose(0, 1, 3, 2)
        hf, hb = _mlstm(mqk, mv, mg, mgt, conv_w[l], conv_b[l], nt)
        xm, f, ri, rg, counts = _outproj(do, go, hf, hb, mo, xa, mod[l], norm2_g[l].reshape(1, D_MODEL),
                                         mlstm_head_norm[l].reshape(1, 256), w_out_r[l], wr_hi[l], wr_lo[l], rb[l],
                                         head_mat, nt, with_ctx=not last)
        n_tok = n_lat if last else nt
        row_src, row_dst, blk_expert, blk_valid = _route_plan(ri, counts, n_tok)
        y = _experts(f, row_src, row_dst, blk_expert, blk_valid, moe_w1[l], moe_w3[l], moe_w2[l], n_tok)
        if last:
            out = _final(xm, y, rg, mod[l][:B], n_tok)
            return out.reshape(B, SEQ, D_MODEL)
        xa = xm
        moe_in = (y, rg, mod[l])
```

```python
import functools
import math

import numpy as np
import jax
import jax.numpy as jnp
from jax import lax
from jax.experimental import pallas as pl
from jax.experimental.pallas import tpu as pltpu

F32 = jnp.float32
BF16 = jnp.bfloat16
I32 = jnp.int32

D_MODEL = 1024
DEPTH = 4
GRID_W = 64
CTX_LEN = 256
SEQ = 2048
ROPE_THETA = 10000.0
EPS = 1e-6

DIFF_HEADS = 4
DIFF_HEAD_DIM = 48
DIFF_V_DIM = 96
GQA_Q_HEADS = 6
GQA_KV_HEADS = 2
GQA_HEAD_DIM = 64
MLSTM_HEADS = 4
MLSTM_HEAD_DIM = 64
MLSTM_CHUNK = 64
IN_WIDTH = 2832
MOE_GROUPS = 4
MOE_EPG = 8
MOE_EXPERTS = 32
MOE_HIDDEN = 512

LANE = 128
TILE = 256
SEG_TILES = (CTX_LEN + SEQ) // TILE
SEG_ROWS = CTX_LEN + SEQ
N_CHUNKS = SEG_ROWS // MLSTM_CHUNK
CTX_CHUNKS = CTX_LEN // MLSTM_CHUNK
MOE_TILE = 256
VMEM_LIMIT = 56 * 1024 * 1024

C_DQ, C_DK, C_DV, C_GQ, C_GK, C_GV, C_MQK, C_MV, C_MO, C_MG, C_END = (
    0, 512, 1024, 1536, 1920, 2048, 2176, 2688, 2944, 3200, 3328)
MIX_ROWS = 512 + 384 + 256


def _cparams(sem):
    return pltpu.CompilerParams(dimension_semantics=sem, vmem_limit_bytes=VMEM_LIMIT)


_LANE2 = np.arange(2 * LANE)
_PAIR_MAT = ((_LANE2[:, None] // LANE == _LANE2[None, :] // LANE)
             & ((_LANE2[:, None] // 32) % 2 == (_LANE2[None, :] // 32) % 2)).astype(np.float32)


def _rope_tables(nf, pad_from):
    t = jnp.arange(SEQ, dtype=I32)
    rows = (t // GRID_W).astype(F32)
    cols = (t % GRID_W).astype(F32)
    freqs = ROPE_THETA ** (-jnp.arange(nf, dtype=F32) / nf)
    lane = np.arange(LANE)
    i = lane % 32
    typ = (lane // 32) // 2
    use_rows = i < nf
    fidx = np.where(use_rows, i, i - nf)
    valid = i < pad_from
    fidx = np.where(valid, fidx, 0)
    ang = jnp.where(jnp.asarray(use_rows)[None, :], rows[:, None], cols[:, None]) * freqs[jnp.asarray(fidx)][None, :]
    cos = jnp.where(jnp.asarray(valid)[None, :], jnp.cos(ang), 1.0)
    sin = jnp.where(jnp.asarray(valid)[None, :], jnp.sin(ang), 0.0)
    sin = sin * jnp.asarray(np.where(typ == 0, -1.0, 1.0), F32)[None, :]
    cos = jnp.concatenate([jnp.ones((CTX_LEN, LANE), F32), cos], axis=0)
    sin = jnp.concatenate([jnp.zeros((CTX_LEN, LANE), F32), sin], axis=0)
    return cos, sin


def _dot(a, b):
    return jnp.dot(a, b, preferred_element_type=F32)


def _dot_nt(a, b):
    return lax.dot_general(a, b, (((1,), (1,)), ((), ())), preferred_element_type=F32)


def _dot_tn(a, b):
    return lax.dot_general(a, b, (((0,), (0,)), ((), ())), preferred_element_type=F32)


def _split3(x):
    x1 = x.astype(BF16)
    r1 = x - x1.astype(F32)
    x2 = r1.astype(BF16)
    x3 = (r1 - x2.astype(F32)).astype(BF16)
    return x1, x2, x3


def _dot_f32_by_exact(x, m):
    x1, x2, x3 = _split3(x)
    return _dot(x1, m) + _dot(x2, m) + _dot(x3, m)


def _dot_f32_by_exact2(x, m):
    x1 = x.astype(BF16)
    x2 = (x - x1.astype(F32)).astype(BF16)
    return _dot(x1, m) + _dot(x2, m)


def _exact_by_dot_f32(m, x):
    x1, x2, x3 = _split3(x)
    return _dot(m, x1) + _dot(m, x2) + _dot(m, x3)


def _sigmoid(x):
    return 1.0 / (1.0 + jnp.exp(-x))


def _silu(x):
    return x * _sigmoid(x)


def _log_sigmoid(x):
    return jnp.minimum(x, 0.0) - jnp.log1p(jnp.exp(-jnp.abs(x)))


ROW_TILES = D_MODEL // LANE


def _store_row_tiles(ref, lead, rows, x):
    for c in range(ROW_TILES):
        ref[lead + (pl.ds(c, rows, stride=ROW_TILES), slice(None))] = x[:, c * LANE:(c + 1) * LANE]


def _load_row_tiles(ref, lead, rows):
    return jnp.concatenate([ref[lead + (pl.ds(c, rows, stride=ROW_TILES), slice(None))] for c in range(ROW_TILES)], axis=1)


def _seg_tile(i, nq, off):
    return (i // nq) * SEG_TILES + off + i % nq


def _mod_row(i, nq, off):
    return jnp.where((off + i % nq) == 0, 8, i // nq)


MOD_BN = 1536


def _mod_kernel(c_ref, w_ref, b_ref, o_ref):
    a = _silu(c_ref[...]).astype(BF16)
    o_ref[0] = _dot(a, w_ref[0].astype(BF16)) + b_ref[0]


def _modulation(cvec, w_mod, b_mod):
    nb = 6 * D_MODEL // MOD_BN
    return pl.pallas_call(
        _mod_kernel,
        grid=(DEPTH, nb),
        in_specs=[pl.BlockSpec((16, D_MODEL), lambda l, n: (0, 0)),
                  pl.BlockSpec((1, D_MODEL, MOD_BN), lambda l, n: (l, 0, n)),
                  pl.BlockSpec((1, 1, MOD_BN), lambda l, n: (l, 0, n))],
        out_specs=pl.BlockSpec((1, 16, MOD_BN), lambda l, n: (l, 0, n)),
        out_shape=jax.ShapeDtypeStruct((DEPTH, 16, 6 * D_MODEL), F32),
        compiler_params=_cparams(("arbitrary", "arbitrary")),
        name="modulation",
    )(cvec, w_mod, b_mod.reshape(DEPTH, 1, 6 * D_MODEL))


def _inproj_kernel(has_moe, *refs):
    if has_moe:
        (x_ref, y_ref, rg_ref, modp_ref, mod_ref, n1_ref, w_ref, gains_ref, gb_ref, cd_ref, sd_ref, cg_ref, sg_ref,
         pm_ref, xo_ref, dq_ref, dk_ref, dv_ref, gq_ref, gk_ref, gv_ref, mqk_ref, mv_ref, mo_ref, mg_ref) = refs
    else:
        (x_ref, mod_ref, n1_ref, w_ref, gains_ref, gb_ref, cd_ref, sd_ref, cg_ref, sg_ref,
         pm_ref, dq_ref, dk_ref, dv_ref, gq_ref, gk_ref, gv_ref, mqk_ref, mv_ref, mo_ref, mg_ref) = refs
    x = x_ref[...]
    if has_moe:
        g2 = modp_ref[0, :, 5 * D_MODEL:6 * D_MODEL]
        rg = rg_ref[...]
        x = x + g2 * (rg[:, 0:1] * _load_row_tiles(y_ref, (0,), TILE) + rg[:, 1:2] * _load_row_tiles(y_ref, (1,), TILE))
        xo_ref[...] = x
    sh = mod_ref[0, :, 0:D_MODEL]
    sc = mod_ref[0, :, D_MODEL:2 * D_MODEL]
    xn = x * lax.rsqrt(jnp.mean(x * x, axis=-1, keepdims=True) + EPS) * n1_ref[...]
    h = (xn * (1.0 + sc) + sh).astype(BF16)
    pm = pm_ref[...]

    def proj(a, b):
        return _dot_nt(h, w_ref[a:b, :])

    def qk_group(col, gain_col, inv_dim, cos, sin, outs):
        y = proj(col, col + 512)
        for half in range(2):
            yh = y[:, half * 256:(half + 1) * 256]
            yn = yh * lax.rsqrt(_dot_f32_by_exact2(yh * yh, pm) * inv_dim + EPS)
            yn = yn * gains_ref[:, gain_col + half * 256:gain_col + (half + 1) * 256]
            for j in range(2):
                yb = yn[:, j * LANE:(j + 1) * LANE]
                ref, off = outs[half * 2 + j]
                ref[:, off:off + LANE] = (yb * cos + pltpu.roll(yb, 64, 1) * sin).astype(BF16)

    cd, sd, cg, sg = cd_ref[...], sd_ref[...], cg_ref[...], sg_ref[...]
    qk_group(C_DQ, 0, 1.0 / DIFF_HEAD_DIM, cd, sd, [(dq_ref, b * LANE) for b in range(4)])
    qk_group(C_DK, 512, 1.0 / DIFF_HEAD_DIM, cd, sd, [(dk_ref, b * LANE) for b in range(4)])
    qk_group(C_GQ, 1024, 1.0 / GQA_HEAD_DIM, cg, sg, [(gq_ref, 0), (gq_ref, LANE), (gq_ref, 2 * LANE), (gk_ref, 0)])
    dv_ref[...] = proj(C_DV, C_GQ).astype(BF16)
    rest = proj(C_GV, C_END)
    gv_ref[...] = rest[:, 0:C_MQK - C_GV].astype(BF16)
    mqk_ref[...] = rest[:, C_MQK - C_GV:C_MV - C_GV]
    mv_ref[...] = rest[:, C_MV - C_GV:C_MO - C_GV].astype(BF16)
    mo_ref[...] = rest[:, C_MO - C_GV:C_MG - C_GV]
    mg_ref[...] = rest[:, C_MG - C_GV:C_END - C_GV] + gb_ref[...]


def _inproj(x, moe_in, mod_l, n1g, w_in_l, gains, gate_b, tabs, pair_mat, nt):
    has_moe = moe_in is not None
    n_tiles = nt // TILE
    tile_map = lambda i: (i, 0)
    mod_spec = pl.BlockSpec((1, 1, 6 * D_MODEL), lambda i: (_mod_row(i, SEG_TILES, 0), 0, 0))
    tab_spec = pl.BlockSpec((TILE, LANE), lambda i: (i % SEG_TILES, 0))

    def full(shape):
        return pl.BlockSpec(shape, lambda i: (0,) * len(shape))

    in_specs = [pl.BlockSpec((TILE, D_MODEL), tile_map)]
    args = [x]
    if has_moe:
        y_flat, rg, mod_prev = moe_in
        in_specs += [pl.BlockSpec((2, TILE * ROW_TILES, LANE), lambda i: (0, i, 0)), pl.BlockSpec((TILE, LANE), tile_map), mod_spec]
        args += [y_flat, rg, mod_prev]
    in_specs += [mod_spec, full((1, D_MODEL)), full((C_END, D_MODEL)), full((1, 1536)), full((1, LANE)),
                 tab_spec, tab_spec, tab_spec, tab_spec, full((2 * LANE, 2 * LANE))]
    args += [mod_l, n1g, w_in_l, gains, gate_b, tabs[0], tabs[1], tabs[2], tabs[3], pair_mat]

    def o(width, dtype):
        return pl.BlockSpec((TILE, width), tile_map), jax.ShapeDtypeStruct((nt, width), dtype)

    outs = []
    if has_moe:
        outs.append(o(D_MODEL, F32))
    outs += [o(512, BF16), o(512, BF16), o(512, BF16), o(384, BF16), o(LANE, BF16), o(LANE, BF16),
             o(512, F32), o(256, BF16), o(256, F32), o(LANE, F32)]
    res = pl.pallas_call(
        functools.partial(_inproj_kernel, has_moe),
        grid=(n_tiles,),
        in_specs=in_specs,
        out_specs=[s for s, _ in outs],
        out_shape=[s for _, s in outs],
        compiler_params=_cparams(("arbitrary",)),
        name="inproj_moe" if has_moe else "inproj",
    )(*args)
    if has_moe:
        return res[0], res[1:]
    return x, res


def _lambda_value(lam_ref, lam_init):
    lam = lam_ref[...]
    s01 = jnp.sum(lam[0:1] * lam[1:2], axis=-1, keepdims=True)
    s23 = jnp.sum(lam[2:3] * lam[3:4], axis=-1, keepdims=True)
    return jnp.exp(s01) - jnp.exp(s23) + lam_init


LOG2E = 1.4426950408889634
SAFE_LOG2_RANGE = 60.0


def _exp_scores(s, stabilise):
    if stabilise:
        s = s - jnp.max(s, axis=-1, keepdims=True)
    return jnp.exp2(s).astype(BF16)


def _score_bound(q_gain, k_gain, head_dim):
    return (1.02 * LOG2E * math.sqrt(head_dim)) * jnp.max(jnp.abs(q_gain), axis=-1) * jnp.max(jnp.abs(k_gain), axis=-1)


def _attn_branches(with_ctx, bound_ref, run, k_ref, v_ref):
    def on_keys(rows):
        small = bound_ref[0] <= SAFE_LOG2_RANGE

        @pl.when(small)
        def _():
            run(k_ref[0:rows, :], v_ref[0:rows, :], False)

        @pl.when(jnp.logical_not(small))
        def _():
            run(k_ref[0:rows, :], v_ref[0:rows, :], True)

    if not with_ctx:
        on_keys(SEG_ROWS)
        return
    t = pl.program_id(2)

    @pl.when(t == 0)
    def _():
        on_keys(TILE)

    @pl.when(t > 0)
    def _():
        on_keys(SEG_ROWS)


def _attn_maps(with_ctx):
    if with_ctx:
        m = lambda b, h, t: (b * SEG_TILES + t, h)
        return SEG_TILES, m, m
    nq = SEG_TILES - 1
    return nq, (lambda b, h, t: (b * SEG_TILES + 1 + t, h)), (lambda b, h, t: (b * nq + t, h))


def _lane_masks():
    lane = np.arange(LANE)
    even = (lane // 32) % 2 == 0
    rows = [even, ~even, lane < 64, lane >= 64, lane == 64, lane == 0, lane == DIFF_V_DIM, lane < 0]
    return jnp.asarray(np.stack(rows).astype(np.float32))


def _mask_row(lm_ref, r):
    return lm_ref[r:r + 1, :].astype(BF16)


def _diff_attn_kernel(lam_init, with_ctx, bound_ref, q_ref, k_ref, v_ref, lam_ref, sg_ref, lm_ref, o_ref):
    def run(kk, vv, stabilise):
        q = q_ref[...]
        v1 = vv + _mask_row(lm_ref, 6)
        o1 = _dot(_exp_scores(_dot_nt(q, kk * _mask_row(lm_ref, 0)), stabilise), v1)
        o2 = _dot(_exp_scores(_dot_nt(q, kk * _mask_row(lm_ref, 1)), stabilise), v1)
        lam = _lambda_value(lam_ref, lam_init)
        o = o1 * (1.0 / o1[:, DIFF_V_DIM:DIFF_V_DIM + 1]) - o2 * (lam / o2[:, DIFF_V_DIM:DIFF_V_DIM + 1])
        o = jnp.where(lax.broadcasted_iota(I32, o.shape, 1) < DIFF_V_DIM, o, 0.0)
        ms = jnp.sum(o * o, axis=-1, keepdims=True) * (1.0 / DIFF_V_DIM)
        o_ref[...] = (o * lax.rsqrt(ms + EPS) * sg_ref[...] * (1.0 - lam_init)).astype(BF16)

    _attn_branches(with_ctx, bound_ref, run, k_ref, v_ref)


def _diff_attention(bound, dq, dk, dv, lam_pad, subln, lam_init, nt, with_ctx):
    nb = nt // SEG_ROWS
    nq, q_map, o_map = _attn_maps(with_ctx)
    kv_map = lambda b, h, t: (b, h)
    const = lambda b, h, t: (0, 0)
    return pl.pallas_call(
        functools.partial(_diff_attn_kernel, lam_init, with_ctx),
        grid=(nb, DIFF_HEADS, nq),
        in_specs=[pl.BlockSpec(memory_space=pltpu.SMEM),
                  pl.BlockSpec((TILE, LANE), q_map), pl.BlockSpec((SEG_ROWS, LANE), kv_map),
                  pl.BlockSpec((SEG_ROWS, LANE), kv_map), pl.BlockSpec((8, LANE), const),
                  pl.BlockSpec((1, LANE), const), pl.BlockSpec((8, LANE), const)],
        out_specs=pl.BlockSpec((TILE, LANE), o_map),
        out_shape=jax.ShapeDtypeStruct((nb * nq * TILE, 512), BF16),
        compiler_params=_cparams(("arbitrary", "arbitrary", "arbitrary")),
        name="diff_attn",
    )(bound, dq, dk, dv, lam_pad, subln, _lane_masks())


def _gqa_attn_kernel(with_ctx, bound_ref, q_ref, k_ref, v_ref, lm_ref, o_ref):
    def run(kk, vv, stabilise):
        q = q_ref[...]
        va = vv * _mask_row(lm_ref, 2) + _mask_row(lm_ref, 4)
        vb = vv * _mask_row(lm_ref, 3) + _mask_row(lm_ref, 5)
        oa = _dot(_exp_scores(_dot_nt(q, kk * _mask_row(lm_ref, 0)), stabilise), va)
        ob = _dot(_exp_scores(_dot_nt(q, kk * _mask_row(lm_ref, 1)), stabilise), vb)
        lane = lax.broadcasted_iota(I32, oa.shape, 1)
        o_ref[...] = jnp.where(lane < GQA_HEAD_DIM, oa * (1.0 / oa[:, GQA_HEAD_DIM:GQA_HEAD_DIM + 1]),
                               ob * (1.0 / ob[:, 0:1])).astype(BF16)

    _attn_branches(with_ctx, bound_ref, run, k_ref, v_ref)


def _gqa_attention(bound, gq, gk, gv, nt, with_ctx):
    nb = nt // SEG_ROWS
    nq, q_map, o_map = _attn_maps(with_ctx)
    kv_map = lambda b, p, t: (b, 0)
    return pl.pallas_call(
        functools.partial(_gqa_attn_kernel, with_ctx),
        grid=(nb, 3, nq),
        in_specs=[pl.BlockSpec(memory_space=pltpu.SMEM),
                  pl.BlockSpec((TILE, LANE), q_map), pl.BlockSpec((SEG_ROWS, LANE), kv_map),
                  pl.BlockSpec((SEG_ROWS, LANE), kv_map), pl.BlockSpec((8, LANE), lambda b, p, t: (0, 0))],
        out_specs=pl.BlockSpec((TILE, LANE), o_map),
        out_shape=jax.ShapeDtypeStruct((nb * nq * TILE, 384), BF16),
        compiler_params=_cparams(("arbitrary", "arbitrary", "arbitrary")),
        name="gqa_attn",
    )(bound, gq, gk, gv, _lane_masks())


def _mlstm_kernel(mqk_ref, mv_ref, mg_ref, mgt_ref, cw_ref, cb_ref, hf_ref, hb_ref, qk_s):
    L = MLSTM_CHUNK
    w0, w1, w2, cb = cw_ref[0:1, :], cw_ref[1:2, :], cw_ref[2:3, :], cb_ref[...]
    rid = lax.broadcasted_iota(I32, (TILE, 512), 0)
    kscale = jnp.where(lax.broadcasted_iota(I32, (1, 512), 1) < 256, 1.0, MLSTM_HEAD_DIM ** -0.5)
    zrow = jnp.zeros((1, 512), F32)
    for c in range(SEG_TILES):
        r0 = c * TILE
        xc = mqk_ref[r0:r0 + TILE, :]
        prev = zrow if c in (0, 1) else mqk_ref[r0 - 1:r0, :]
        nxt = zrow if c in (0, SEG_TILES - 1) else mqk_ref[r0 + TILE:r0 + TILE + 1, :]
        up = jnp.where(rid == 0, prev, pltpu.roll(xc, 1, 0))
        dn = jnp.where(rid == TILE - 1, nxt, pltpu.roll(xc, TILE - 1, 0))
        y = w0 * up + w1 * xc + w2 * dn + cb
        qk_s[r0:r0 + TILE, :] = _silu(y) * kscale

    ti = lax.broadcasted_iota(I32, (L, L), 0)
    si = lax.broadcasted_iota(I32, (L, L), 1)
    tri_le = (si <= ti)
    m_le = tri_le.astype(BF16)
    m_ge = (si >= ti).astype(BF16)
    row2 = lax.broadcasted_iota(I32, (2 * L, L), 0)
    trow = row2 % L
    scol = lax.broadcasted_iota(I32, (2 * L, L), 1)
    top2 = row2 < L
    mask_f = scol <= trow
    mask_b = scol >= trow
    lane_lo = lax.broadcasted_iota(I32, (L, LANE), 1) < L
    lane_lo256 = lax.broadcasted_iota(I32, (L, 2 * LANE), 1) % LANE < L
    rr = lax.broadcasted_iota(I32, (LANE, LANE), 0)
    cc = lax.broadcasted_iota(I32, (LANE, LANE), 1)
    blockdiag = (rr < L) == (cc < L)
    rows_lo = lax.broadcasted_iota(I32, (LANE, 1), 0) < L
    top_col = lax.broadcasted_iota(I32, (2 * L, 1), 0) < L

    def chain(c, is_fwd, p, gcol, bcol_all, grow, brow_all, state, out_ref):
        ct, nm, m0, m1 = state
        h0, h1 = 2 * p, 2 * p + 1
        gi, gf = (0, 4) if is_fwd else (8, 12)
        r0 = pl.multiple_of(c * L, L)
        q128 = qk_s[pl.ds(r0, L), p * LANE:(p + 1) * LANE]
        k128 = qk_s[pl.ds(r0, L), 256 + p * LANE:256 + (p + 1) * LANE]
        v128 = mv_ref[pl.ds(r0, L), p * LANE:(p + 1) * LANE]

        def stack_cols(arr, j0, j1):
            return jnp.concatenate([arr[:, j0:j0 + 1], arr[:, j1:j1 + 1]], axis=0)

        def stack_rows(arr, j0, j1):
            return jnp.where(top2, arr[j0:j0 + 1, :], arr[j1:j1 + 1, :])

        bcol = stack_cols(bcol_all, gf + h0, gf + h1)
        licol = stack_cols(gcol, gi + h0, gi + h1)
        crow = stack_rows(grow, gi + h0, gi + h1) - stack_rows(brow_all, gf + h0, gf + h1)
        cm = jnp.where(mask_f if is_fwd else mask_b, crow, -jnp.inf)
        mcol = jnp.where(top_col, m0, m1)
        u = jnp.maximum(mcol, jnp.max(cm, axis=-1, keepdims=True))
        qb = q128.astype(BF16)
        zq = jnp.zeros_like(qb)
        qstack = jnp.concatenate([jnp.where(lane_lo, qb, zq), jnp.where(lane_lo, zq, qb)], axis=0)
        kb = k128.astype(BF16)
        w = (jnp.exp(cm - u) * _dot_nt(qstack, kb)).astype(BF16)
        wv = _dot(w, jnp.concatenate([v128, jnp.ones_like(v128)], axis=1))
        wv = jnp.where(lane_lo256, wv[:L], wv[L:])
        qcn = _dot(qb, jnp.concatenate([ct, nm], axis=1).astype(BF16))
        a = jnp.exp(mcol - u)
        emt = jnp.exp(-(bcol + u))
        a128 = jnp.where(lane_lo, a[:L], a[L:])
        emt128 = jnp.where(lane_lo, emt[:L], emt[L:])
        num = wv[:, :LANE] + a128 * qcn[:, :LANE]
        den = wv[:, LANE:] + a128 * qcn[:, LANE:]
        out_ref[pl.ds(r0, L), p * LANE:(p + 1) * LANE] = num / jnp.maximum(jnp.abs(den), emt128)
        e0 = (L - 1) if is_fwd else 0
        bend0 = bcol[e0:e0 + 1, :]
        bend1 = bcol[L + e0:L + e0 + 1, :]
        bend = jnp.where(top_col, bend0, bend1)
        g = bend - bcol + licol
        m0n = jnp.maximum(bend0 + m0, jnp.max(g[:L], axis=0, keepdims=True))
        m1n = jnp.maximum(bend1 + m1, jnp.max(g[L:], axis=0, keepdims=True))
        ws = jnp.exp(g - jnp.where(top_col, m0n, m1n))
        ae0 = jnp.exp(bend0 + m0 - m0n)
        ae1 = jnp.exp(bend1 + m1 - m1n)
        ws128 = jnp.where(lane_lo, ws[:L], ws[L:])
        vw = jnp.concatenate([v128.astype(F32) * ws128, ws128], axis=1).astype(BF16)
        upd = _dot_tn(kb, vw)
        ae = jnp.where(rows_lo, ae0, ae1)
        ct_new = ae * ct + jnp.where(blockdiag, upd[:, :LANE], 0.0)
        nm_new = ae * nm + jnp.where(blockdiag, upd[:, LANE:], 0.0)
        return ct_new, nm_new, m0n, m1n

    def body(i, carry):
        cf = i
        cbk = jnp.where(i < CTX_CHUNKS, CTX_CHUNKS - 1 - i, N_CHUNKS + CTX_CHUNKS - 1 - i)
        new = []
        for d, (c, out_ref) in enumerate(((cf, hf_ref), (cbk, hb_ref))):
            is_fwd = d == 0
            r0 = pl.multiple_of(c * L, L)
            gcol = mg_ref[pl.ds(r0, L), :]
            grow = mgt_ref[0, c]
            lf_col = _log_sigmoid(gcol)
            lf_row = _log_sigmoid(grow)
            if is_fwd:
                bcol_all = _exact_by_dot_f32(m_le, lf_col)
                brow_all = _dot_f32_by_exact(lf_row, m_ge)
            else:
                bcol_all = _exact_by_dot_f32(m_ge, lf_col)
                brow_all = _dot_f32_by_exact(lf_row, m_le)
            for p in range(2):
                st = carry[d * 2 + p]
                new.append(chain(c, is_fwd, p, gcol, bcol_all, grow, brow_all, st, out_ref))
        return tuple(new)

    z = (jnp.zeros((LANE, LANE), F32), jnp.zeros((LANE, LANE), F32), jnp.zeros((1, 1), F32), jnp.zeros((1, 1), F32))
    lax.fori_loop(0, N_CHUNKS, body, (z, z, z, z), unroll=2)


def _mlstm(mqk, mv, mg, mgt, conv_w, conv_b, nt):
    nb = nt // SEG_ROWS
    blk = lambda w: pl.BlockSpec((SEG_ROWS, w), lambda b: (b, 0))
    return pl.pallas_call(
        _mlstm_kernel,
        grid=(nb,),
        in_specs=[blk(512), blk(256), blk(LANE),
                  pl.BlockSpec((1, N_CHUNKS, 16, MLSTM_CHUNK), lambda b: (b, 0, 0, 0)),
                  pl.BlockSpec((8, 512), lambda b: (0, 0)), pl.BlockSpec((1, 512), lambda b: (0, 0))],
        out_specs=[blk(256), blk(256)],
        out_shape=[jax.ShapeDtypeStruct((nt, 256), F32), jax.ShapeDtypeStruct((nt, 256), F32)],
        scratch_shapes=[pltpu.VMEM((SEG_ROWS, 512), F32)],
        compiler_params=_cparams(("arbitrary",)),
        name="mlstm",
    )(mqk, mv, mg, mgt, conv_w, conv_b)


def _outproj_kernel(do_ref, go_ref, hf_ref, hb_ref, mo_ref, x_ref, mod_ref, n2_ref, hn_ref, wo_ref, wr_hi_ref, wr_lo_ref,
                    rb_ref, hm_ref, xm_ref, f_ref, ri_ref, rg_ref, cnt_ref, carry):
    i = pl.program_id(0)

    @pl.when(i == 0)
    def _():
        carry[...] = jnp.zeros_like(carry)

    hsum = hf_ref[...] + hb_ref[...]
    ssq = _dot_f32_by_exact(hsum * hsum, hm_ref[...])
    ml = hsum * lax.rsqrt(ssq * (1.0 / MLSTM_HEAD_DIM) + EPS) * hn_ref[...] * _sigmoid(mo_ref[...])
    acc = _dot(do_ref[...], wo_ref[0:512, :])
    acc += _dot(go_ref[...], wo_ref[512:896, :])
    acc += _dot(ml.astype(BF16), wo_ref[896:MIX_ROWS, :])
    g1 = mod_ref[0, :, 2 * D_MODEL:3 * D_MODEL]
    sh2 = mod_ref[0, :, 3 * D_MODEL:4 * D_MODEL]
    sc2 = mod_ref[0, :, 4 * D_MODEL:5 * D_MODEL]
    x = x_ref[...] + g1 * acc
    xm_ref[...] = x
    xn = x * lax.rsqrt(jnp.mean(x * x, axis=-1, keepdims=True) + EPS) * n2_ref[...]
    f = xn * (1.0 + sc2) + sh2
    _store_row_tiles(f_ref, (), TILE, f)
    f1, f2, _ = _split3(f)
    logits = _dot(f1, wr_hi_ref[...]) + _dot(f2, wr_hi_ref[...]) + _dot(f1, wr_lo_ref[...]) + rb_ref[...]
    lane = lax.broadcasted_iota(I32, logits.shape, 1)
    neg = jnp.float32(-jnp.inf)
    big = jnp.int32(1 << 20)
    is_g = lane < MOE_GROUPS
    lg = jnp.where(is_g, logits, neg)
    gmax = jnp.max(lg, axis=-1, keepdims=True)
    g_top = 1.0 / jnp.sum(jnp.exp(lg - gmax), axis=-1, keepdims=True)
    g_idx = jnp.min(jnp.where(lg == gmax, lane, big), axis=-1, keepdims=True)
    in_grp = (lane >= MOE_GROUPS) & (lane < MOE_GROUPS + MOE_EXPERTS) & ((lane - MOE_GROUPS) // MOE_EPG == g_idx)
    le = jnp.where(in_grp, logits, neg)
    v1 = jnp.max(le, axis=-1, keepdims=True)
    l1 = jnp.min(jnp.where(le == v1, lane, big), axis=-1, keepdims=True)
    le2 = jnp.where(lane == l1, neg, le)
    v2 = jnp.max(le2, axis=-1, keepdims=True)
    l2 = jnp.min(jnp.where(le2 == v2, lane, big), axis=-1, keepdims=True)
    ex = jnp.exp(v2 - v1)
    gate1 = g_top / (1.0 + ex)
    gate2 = gate1 * ex
    oh1 = (lane == l1)
    oh2 = (lane == l2)
    both = oh1.astype(BF16) + oh2.astype(BF16)
    ri_ = lax.broadcasted_iota(I32, (TILE, TILE), 0)
    ci_ = lax.broadcasted_iota(I32, (TILE, TILE), 1)
    before = _dot((ci_ < ri_).astype(BF16), both) + carry[0:1, :]
    rank1 = jnp.sum(jnp.where(oh1, before, 0.0), axis=-1, keepdims=True)
    rank2 = jnp.sum(jnp.where(oh2, before, 0.0), axis=-1, keepdims=True)
    carry[0:1, :] = carry[0:1, :] + jnp.sum(both.astype(F32), axis=0, keepdims=True)
    cnt_ref[...] = jnp.broadcast_to(carry[0:1, :], cnt_ref.shape)
    ri = jnp.where(lane == 0, l1 - MOE_GROUPS, jnp.where(lane == 1, l2 - MOE_GROUPS,
         jnp.where(lane == 2, rank1.astype(I32), jnp.where(lane == 3, rank2.astype(I32), 0))))
    ri_ref[...] = ri
    rg_ref[...] = jnp.where(lane == 0, gate1, jnp.where(lane == 1, gate2, 0.0))


def _outproj(do, go, hf, hb, mo, x, mod_l, n2g, hn, w_out_l, wr_hi, wr_lo, rb, head_mat, nt, with_ctx):
    nb = nt // SEG_ROWS
    nq, off = (SEG_TILES, 0) if with_ctx else (SEG_TILES - 1, 1)
    n_steps = nb * nq
    n_out = n_steps * TILE
    in_map = lambda i: (_seg_tile(i, nq, off), 0)
    out_map = lambda i: (i, 0)

    def full(shape):
        return pl.BlockSpec(shape, lambda i: (0,) * len(shape))

    in_specs = [pl.BlockSpec((TILE, 512), out_map), pl.BlockSpec((TILE, 384), out_map), pl.BlockSpec((TILE, 256), in_map),
                pl.BlockSpec((TILE, 256), in_map), pl.BlockSpec((TILE, 256), in_map), pl.BlockSpec((TILE, D_MODEL), in_map),
                pl.BlockSpec((1, 1, 6 * D_MODEL), lambda i: (_mod_row(i, nq, off), 0, 0)),
                full((1, D_MODEL)), full((1, 256)), full((MIX_ROWS, D_MODEL)), full((D_MODEL, LANE)), full((D_MODEL, LANE)),
                full((1, LANE)), full((256, 256))]
    out_specs = [pl.BlockSpec((TILE, D_MODEL), out_map), pl.BlockSpec((TILE * ROW_TILES, LANE), out_map),
                 pl.BlockSpec((TILE, LANE), out_map), pl.BlockSpec((TILE, LANE), out_map), pl.BlockSpec((8, LANE), lambda i: (0, 0))]
    out_shape = [jax.ShapeDtypeStruct((n_out, D_MODEL), F32), jax.ShapeDtypeStruct((n_out * ROW_TILES, LANE), F32),
                 jax.ShapeDtypeStruct((n_out, LANE), I32), jax.ShapeDtypeStruct((n_out, LANE), F32),
                 jax.ShapeDtypeStruct((8, LANE), F32)]
    return pl.pallas_call(
        _outproj_kernel,
        grid=(n_steps,),
        in_specs=in_specs,
        out_specs=out_specs,
        out_shape=out_shape,
        scratch_shapes=[pltpu.VMEM((8, LANE), F32)],
        compiler_params=_cparams(("arbitrary",)),
        name="outproj_route",
    )(do, go, hf, hb, mo, x, mod_l, n2g, hn, w_out_l, wr_hi, wr_lo, rb, head_mat)


DMA_UNROLL = 8


def _for_rows(n, fn):
    groups = lax.shift_right_logical(n, int(math.log2(DMA_UNROLL)))

    def group(g, _):
        for u in range(DMA_UNROLL):
            fn(g * DMA_UNROLL + u, u % 2)
        return 0

    def single(r, _):
        fn(r, 0)
        return 0

    lax.fori_loop(0, groups, group, 0)
    lax.fori_loop(groups * DMA_UNROLL, n, single, 0)


def _expert_kernel(be_ref, nv_ref, src_ref, srcn_ref, dst_ref, f_hbm, w1_ref, w3_ref, w2_ref, y_hbm,
                   xbuf, ybuf, w1s, w3s, w2s, gsem, ssem):
    i = pl.program_id(0)
    nv = nv_ref[i]
    nv_next = nv_ref[i + 1]
    slot = i % 2

    def tile(r):
        return pl.ds(pl.multiple_of(r * ROW_TILES, ROW_TILES), ROW_TILES)

    def gather_copy(idx_ref, s, r):
        return pltpu.make_async_copy(f_hbm.at[tile(idx_ref[0, 0, r])], xbuf.at[s, tile(r)], gsem.at[s])

    def scatter_copy(r):
        pair = dst_ref[0, 0, r]
        return pltpu.make_async_copy(ybuf.at[tile(r)], y_hbm.at[pair & 1, tile(lax.shift_right_logical(pair, 1))], ssem)

    def rows(n):
        return pl.ds(0, pl.multiple_of(n * ROW_TILES, ROW_TILES))

    def gather_wait(n):
        pltpu.make_async_copy(f_hbm.at[rows(n)], xbuf.at[slot, rows(n)], gsem.at[slot]).wait()

    def scatter_wait(n):
        pltpu.make_async_copy(ybuf.at[rows(n)], y_hbm.at[0, rows(n)], ssem).wait()

    @pl.when(i == 0)
    def _():
        xbuf[...] = jnp.zeros_like(xbuf)
        _for_rows(nv, lambda r, pr: gather_copy(src_ref, 0, r).start(priority=pr))

    @pl.when(nv_next > 0)
    def _():
        _for_rows(nv_next, lambda r, pr: gather_copy(srcn_ref, 1 - slot, r).start(priority=pr))

    @pl.when((nv > 0) & ((i == 0) | (be_ref[i] != be_ref[jnp.maximum(i - 1, 0)])))
    def _():
        w1s[...] = w1_ref[0, 0].astype(BF16)
        w3s[...] = w3_ref[0, 0].astype(BF16)
        w2s[...] = w2_ref[0, 0].astype(BF16)

    @pl.when(nv > 0)
    def _():
        gather_wait(nv)
        xb = _load_row_tiles(xbuf, (slot,), MOE_TILE).astype(BF16)
        hh = _silu(_dot(xb, w1s[...])) * _dot(xb, w3s[...])
        y = _dot(hh.astype(BF16), w2s[...])

        @pl.when(i > 0)
        def _():
            scatter_wait(nv_ref[jnp.maximum(i - 1, 0)])

        _store_row_tiles(ybuf, (), MOE_TILE, y)
        _for_rows(nv, lambda r, pr: scatter_copy(r).start(priority=pr))

        @pl.when(nv_next == 0)
        def _():
            scatter_wait(nv)


def _experts(f, row_src, row_dst, blk_expert, blk_valid, w1, w3, w2, layer, n_tok):
    n_blk = row_src.shape[0]
    idx_spec = lambda fn: pl.BlockSpec((1, 1, MOE_TILE), fn, memory_space=pltpu.SMEM)
    w_spec = lambda shape: pl.BlockSpec((1, 1) + shape, lambda i, be, nv: (layer, be[i], 0, 0))
    grid_spec = pltpu.PrefetchScalarGridSpec(
        num_scalar_prefetch=2,
        grid=(n_blk,),
        in_specs=[idx_spec(lambda i, be, nv: (i, 0, 0)),
                  idx_spec(lambda i, be, nv: (jnp.minimum(i + 1, n_blk - 1), 0, 0)),
                  idx_spec(lambda i, be, nv: (i, 0, 0)),
                  pl.BlockSpec(memory_space=pl.ANY),
                  w_spec((D_MODEL, MOE_HIDDEN)), w_spec((D_MODEL, MOE_HIDDEN)), w_spec((MOE_HIDDEN, D_MODEL))],
        out_specs=pl.BlockSpec(memory_space=pl.ANY),
        scratch_shapes=[pltpu.VMEM((2, MOE_TILE * ROW_TILES, LANE), F32), pltpu.VMEM((MOE_TILE * ROW_TILES, LANE), F32),
                        pltpu.VMEM((D_MODEL, MOE_HIDDEN), BF16), pltpu.VMEM((D_MODEL, MOE_HIDDEN), BF16),
                        pltpu.VMEM((MOE_HIDDEN, D_MODEL), BF16),
                        pltpu.SemaphoreType.DMA((2,)), pltpu.SemaphoreType.DMA(())],
    )
    return pl.pallas_call(
        _expert_kernel,
        grid_spec=grid_spec,
        out_shape=jax.ShapeDtypeStruct((2, n_tok * ROW_TILES, LANE), F32),
        compiler_params=_cparams(("arbitrary",)),
        name="experts",
    )(blk_expert, blk_valid, row_src, row_src, row_dst, f, w1, w3, w2)


def _route_plan(ri, counts, n_tok):
    n_blk = (2 * n_tok) // MOE_TILE + MOE_EXPERTS
    p_rows = n_blk * MOE_TILE
    cnt = counts[0, MOE_GROUPS:MOE_GROUPS + MOE_EXPERTS].astype(I32)
    padded = (cnt + MOE_TILE - 1) // MOE_TILE * MOE_TILE
    pad_end = jnp.cumsum(padded)
    pad_start = pad_end - padded
    dest = pad_start[ri[:, 0:2]] + ri[:, 2:4]
    row_dst = jnp.zeros((p_rows,), I32).at[dest.reshape(-1)].set(jnp.arange(2 * n_tok, dtype=I32))
    row_src = row_dst // 2
    blk_start = jnp.arange(n_blk + 1, dtype=I32) * MOE_TILE
    blk_expert = jnp.minimum(jnp.sum(pad_end[None, :] <= blk_start[:, None], axis=-1), MOE_EXPERTS - 1).astype(I32)
    in_expert = blk_start - pad_start[blk_expert]
    blk_valid = jnp.where(blk_start < pad_end[-1], jnp.clip(cnt[blk_expert] - in_expert, 0, MOE_TILE), 0).astype(I32)
    return (row_src.reshape(n_blk, 1, MOE_TILE), row_dst.reshape(n_blk, 1, MOE_TILE), blk_expert[:n_blk], blk_valid)


def _final_kernel(x_ref, y_ref, rg_ref, mod_ref, o_ref):
    g2 = mod_ref[0, :, 5 * D_MODEL:6 * D_MODEL]
    rg = rg_ref[...]
    o_ref[...] = x_ref[...] + g2 * (rg[:, 0:1] * _load_row_tiles(y_ref, (0,), TILE) + rg[:, 1:2] * _load_row_tiles(y_ref, (1,), TILE))


def _final(x, y_flat, rg, mod_l, n_tok):
    nq = SEQ // TILE
    tile_map = lambda i: (i, 0)
    return pl.pallas_call(
        _final_kernel,
        grid=(n_tok // TILE,),
        in_specs=[pl.BlockSpec((TILE, D_MODEL), tile_map), pl.BlockSpec((2, TILE * ROW_TILES, LANE), lambda i: (0, i, 0)),
                  pl.BlockSpec((TILE, LANE), tile_map), pl.BlockSpec((1, 1, 6 * D_MODEL), lambda i: (i // nq, 0, 0))],
        out_specs=pl.BlockSpec((TILE, D_MODEL), tile_map),
        out_shape=jax.ShapeDtypeStruct((n_tok, D_MODEL), F32),
        compiler_params=_cparams(("arbitrary",)),
        name="final_residual",
    )(x, y_flat, rg, mod_l)


def _diff_rows(w, heads):
    nl, _, k = w.shape
    w = w.reshape(nl, heads, 2, 2, 2, 12, k)
    w = w.transpose(0, 1, 4, 2, 3, 5, 6)
    w = jnp.pad(w.reshape(nl, heads, 4, 24, k), ((0, 0), (0, 0), (0, 0), (0, 8), (0, 0)))
    return w.reshape(nl, heads * LANE, k)


def _gqa_rows(w, groups):
    nl, _, k = w.shape
    w = w.reshape(nl, 2, groups, 2, 2, 16, k)
    w = w.transpose(0, 2, 4, 1, 3, 5, 6)
    return w.reshape(nl, groups * LANE, k)


def _in_weight(w_in):
    wt = jnp.swapaxes(w_in, 1, 2)
    nl = wt.shape[0]
    dv = jnp.pad(wt[:, 768:1152].reshape(nl, DIFF_HEADS, DIFF_V_DIM, D_MODEL), ((0, 0), (0, 0), (0, LANE - DIFF_V_DIM), (0, 0)))
    parts = [_diff_rows(wt[:, 0:384], DIFF_HEADS), _diff_rows(wt[:, 384:768], DIFF_HEADS),
             dv.reshape(nl, DIFF_HEADS * LANE, D_MODEL), _gqa_rows(wt[:, 1152:1536], 3), _gqa_rows(wt[:, 1536:1664], 1),
             wt[:, 1664:2816], jnp.pad(wt[:, 2816:2832], ((0, 0), (0, LANE - 16), (0, 0)))]
    return jnp.concatenate(parts, axis=1).astype(BF16)


def _lane_gain(g, rows_fn):
    return rows_fn(jnp.concatenate([g, g], axis=-1)[:, :, None], 1)[:, :, 0]


def _out_weight(w_out):
    nl = w_out.shape[0]
    diff = jnp.pad(w_out[:, 0:384].reshape(nl, DIFF_HEADS, DIFF_V_DIM, D_MODEL), ((0, 0), (0, 0), (0, LANE - DIFF_V_DIM), (0, 0)))
    gqa = w_out[:, 384:768].reshape(nl, 2, 3, GQA_HEAD_DIM, D_MODEL).transpose(0, 2, 1, 3, 4)
    return jnp.concatenate([diff.reshape(nl, 512, D_MODEL), gqa.reshape(nl, 384, D_MODEL), w_out[:, 768:]], axis=1).astype(BF16)


def kernel(x, c, ctx, c_ctx, norm1_g, norm2_g, w_mod, b_mod, w_in, w_out, diff_q_norm, diff_k_norm, diff_lambda, diff_subln, gqa_q_norm, gqa_k_norm, mlstm_conv_w, mlstm_conv_b, mlstm_gate_b, mlstm_head_norm, moe_wg, moe_bg, moe_we, moe_be, moe_w1, moe_w3, moe_w2):
    B = x.shape[0]
    nt = B * SEG_ROWS
    n_lat = B * SEQ

    w_in_r = _in_weight(w_in)
    w_out_r = _out_weight(w_out)
    gq_d = _lane_gain(diff_q_norm, _diff_rows) * (LOG2E * DIFF_HEAD_DIM ** -0.5)
    gk_d = _lane_gain(diff_k_norm, _diff_rows)
    gq_g = _lane_gain(gqa_q_norm, _gqa_rows) * (LOG2E * GQA_HEAD_DIM ** -0.5)
    gk_g = _lane_gain(gqa_k_norm, _gqa_rows)
    bound_d = _score_bound(diff_q_norm, diff_k_norm, DIFF_HEAD_DIM).reshape(DEPTH, 1)
    bound_g = _score_bound(gqa_q_norm, gqa_k_norm, GQA_HEAD_DIM).reshape(DEPTH, 1)
    gains = jnp.concatenate([jnp.tile(gq_d, (1, 4)), jnp.tile(gk_d, (1, 4)), jnp.tile(gq_g, (1, 3)), gk_g], axis=1)
    gains = gains.reshape(DEPTH, 1, 1536)
    gate_b = jnp.pad(mlstm_gate_b, ((0, 0), (0, LANE - 16))).reshape(DEPTH, 1, LANE)
    subln = jnp.pad(diff_subln, ((0, 0), (0, LANE - DIFF_V_DIM))).reshape(DEPTH, 1, LANE)
    lam_pad = jnp.pad(diff_lambda, ((0, 0), (0, 4), (0, LANE - DIFF_HEAD_DIM)))
    conv_w = jnp.pad(mlstm_conv_w, ((0, 0), (0, 5), (0, 0)))
    conv_b = mlstm_conv_b.reshape(DEPTH, 1, 512)
    w_r = jnp.pad(jnp.concatenate([moe_wg, moe_we], axis=2), ((0, 0), (0, 0), (0, LANE - 36)))
    wr_hi = w_r.astype(BF16)
    wr_lo = (w_r - wr_hi.astype(F32)).astype(BF16)
    rb = jnp.pad(jnp.concatenate([moe_bg, moe_be], axis=1), ((0, 0), (0, LANE - 36))).reshape(DEPTH, 1, LANE)
    pair_mat = jnp.asarray(_PAIR_MAT, BF16)
    head_mat = jnp.asarray((np.arange(256)[:, None] // 64 == np.arange(256)[None, :] // 64).astype(np.float32), BF16)
    tabs = _rope_tables(12, 24) + _rope_tables(16, 32)

    xa = jnp.concatenate([ctx, x], axis=1).reshape(nt, D_MODEL)
    assert B <= 8, "row 8 of the modulation table is reserved for the context conditioning"
    cvec = jnp.concatenate([c, jnp.zeros((8 - B, D_MODEL), F32), c_ctx[None, :], jnp.zeros((7, D_MODEL), F32)], axis=0)
    mod = _modulation(cvec, w_mod, b_mod).reshape(DEPTH, 16, 1, 6 * D_MODEL)

    moe_in = None
    for l in range(DEPTH):
        last = l == DEPTH - 1
        lam_init = 0.8 - 0.6 * math.exp(-0.3 * l)
        xa, (dq, dk, dv, gq, gk, gv, mqk, mv, mo, mg) = _inproj(
            xa, moe_in, mod[l], norm1_g[l].reshape(1, D_MODEL), w_in_r[l], gains[l], gate_b[l], tabs, pair_mat, nt)
        do = _diff_attention(bound_d[l], dq, dk, dv, lam_pad[l], subln[l], lam_init, nt, with_ctx=not last)
        go = _gqa_attention(bound_g[l], gq, gk, gv, nt, with_ctx=not last)
        mgt = mg[:, :16].reshape(B, N_CHUNKS, MLSTM_CHUNK, 16).transpose(0, 1, 3, 2)
        hf, hb = _mlstm(mqk, mv, mg, mgt, conv_w[l], conv_b[l], nt)
        xm, f, ri, rg, counts = _outproj(do, go, hf, hb, mo, xa, mod[l], norm2_g[l].reshape(1, D_MODEL),
                                         mlstm_head_norm[l].reshape(1, 256), w_out_r[l], wr_hi[l], wr_lo[l], rb[l],
                                         head_mat, nt, with_ctx=not last)
        n_tok = n_lat if last else nt
        row_src, row_dst, blk_expert, blk_valid = _route_plan(ri, counts, n_tok)
        y = _experts(f, row_src, row_dst, blk_expert, blk_valid, moe_w1, moe_w3, moe_w2, l, n_tok)
        if last:
            out = _final(xm, y, rg, mod[l][:B], n_tok)
            return out.reshape(B, SEQ, D_MODEL)
        xa = xm
        moe_in = (y, rg, mod[l])
```

```python
import functools
import math

import numpy as np
import jax
import jax.numpy as jnp
from jax import lax
from jax.experimental import pallas as pl
from jax.experimental.pallas import tpu as pltpu

F32 = jnp.float32
BF16 = jnp.bfloat16
I32 = jnp.int32

D_MODEL = 1024
DEPTH = 4
GRID_W = 64
CTX_LEN = 256
SEQ = 2048
ROPE_THETA = 10000.0
EPS = 1e-6

DIFF_HEADS = 4
DIFF_HEAD_DIM = 48
DIFF_V_DIM = 96
GQA_Q_HEADS = 6
GQA_KV_HEADS = 2
GQA_HEAD_DIM = 64
MLSTM_HEADS = 4
MLSTM_HEAD_DIM = 64
MLSTM_CHUNK = 64
IN_WIDTH = 2832
MOE_GROUPS = 4
MOE_EPG = 8
MOE_EXPERTS = 32
MOE_HIDDEN = 512

LANE = 128
TILE = 256
SEG_TILES = (CTX_LEN + SEQ) // TILE
SEG_ROWS = CTX_LEN + SEQ
N_CHUNKS = SEG_ROWS // MLSTM_CHUNK
CTX_CHUNKS = CTX_LEN // MLSTM_CHUNK
MOE_TILE = 256
VMEM_LIMIT = 56 * 1024 * 1024

C_DQ, C_DK, C_DV, C_GQ, C_GK, C_GV, C_MQK, C_MV, C_MO, C_MG, C_END = (
    0, 512, 1024, 1536, 1920, 2048, 2176, 2688, 2944, 3200, 3328)
MIX_ROWS = 512 + 384 + 256


def _cparams(sem):
    return pltpu.CompilerParams(dimension_semantics=sem, vmem_limit_bytes=VMEM_LIMIT)


_LANE2 = np.arange(2 * LANE)
_PAIR_MAT = ((_LANE2[:, None] // LANE == _LANE2[None, :] // LANE)
             & ((_LANE2[:, None] // 32) % 2 == (_LANE2[None, :] // 32) % 2)).astype(np.float32)


def _rope_tables(nf, pad_from):
    t = jnp.arange(SEQ, dtype=I32)
    rows = (t // GRID_W).astype(F32)
    cols = (t % GRID_W).astype(F32)
    freqs = ROPE_THETA ** (-jnp.arange(nf, dtype=F32) / nf)
    lane = np.arange(LANE)
    i = lane % 32
    typ = (lane // 32) // 2
    use_rows = i < nf
    fidx = np.where(use_rows, i, i - nf)
    valid = i < pad_from
    fidx = np.where(valid, fidx, 0)
    ang = jnp.where(jnp.asarray(use_rows)[None, :], rows[:, None], cols[:, None]) * freqs[jnp.asarray(fidx)][None, :]
    cos = jnp.where(jnp.asarray(valid)[None, :], jnp.cos(ang), 1.0)
    sin = jnp.where(jnp.asarray(valid)[None, :], jnp.sin(ang), 0.0)
    sin = sin * jnp.asarray(np.where(typ == 0, -1.0, 1.0), F32)[None, :]
    cos = jnp.concatenate([jnp.ones((CTX_LEN, LANE), F32), cos], axis=0)
    sin = jnp.concatenate([jnp.zeros((CTX_LEN, LANE), F32), sin], axis=0)
    return cos, sin


def _dot(a, b):
    return jnp.dot(a, b, preferred_element_type=F32)


def _dot_nt(a, b):
    return lax.dot_general(a, b, (((1,), (1,)), ((), ())), preferred_element_type=F32)


def _dot_tn(a, b):
    return lax.dot_general(a, b, (((0,), (0,)), ((), ())), preferred_element_type=F32)


def _split3(x):
    x1 = x.astype(BF16)
    r1 = x - x1.astype(F32)
    x2 = r1.astype(BF16)
    x3 = (r1 - x2.astype(F32)).astype(BF16)
    return x1, x2, x3


def _dot_f32_by_exact(x, m):
    x1, x2, x3 = _split3(x)
    return _dot(x1, m) + _dot(x2, m) + _dot(x3, m)


def _dot_f32_by_exact2(x, m):
    x1 = x.astype(BF16)
    x2 = (x - x1.astype(F32)).astype(BF16)
    return _dot(x1, m) + _dot(x2, m)


def _exact_by_dot_f32(m, x):
    x1, x2, x3 = _split3(x)
    return _dot(m, x1) + _dot(m, x2) + _dot(m, x3)


def _sigmoid(x):
    return 1.0 / (1.0 + jnp.exp(-x))


def _silu(x):
    return x * _sigmoid(x)


def _log_sigmoid(x):
    return jnp.minimum(x, 0.0) - jnp.log1p(jnp.exp(-jnp.abs(x)))


ROW_TILES = D_MODEL // LANE


def _store_row_tiles(ref, lead, rows, x):
    for c in range(ROW_TILES):
        ref[lead + (pl.ds(c, rows, stride=ROW_TILES), slice(None))] = x[:, c * LANE:(c + 1) * LANE]


def _load_row_tiles(ref, lead, rows):
    return jnp.concatenate([ref[lead + (pl.ds(c, rows, stride=ROW_TILES), slice(None))] for c in range(ROW_TILES)], axis=1)


def _seg_tile(i, nq, off):
    return (i // nq) * SEG_TILES + off + i % nq


def _mod_row(i, nq, off):
    return jnp.where((off + i % nq) == 0, 8, i // nq)


MOD_BN = 1536


def _mod_kernel(c_ref, w_ref, b_ref, o_ref):
    a = _silu(c_ref[...]).astype(BF16)
    o_ref[0] = _dot(a, w_ref[0].astype(BF16)) + b_ref[0]


def _modulation(cvec, w_mod, b_mod):
    nb = 6 * D_MODEL // MOD_BN
    return pl.pallas_call(
        _mod_kernel,
        grid=(DEPTH, nb),
        in_specs=[pl.BlockSpec((16, D_MODEL), lambda l, n: (0, 0)),
                  pl.BlockSpec((1, D_MODEL, MOD_BN), lambda l, n: (l, 0, n)),
                  pl.BlockSpec((1, 1, MOD_BN), lambda l, n: (l, 0, n))],
        out_specs=pl.BlockSpec((1, 16, MOD_BN), lambda l, n: (l, 0, n)),
        out_shape=jax.ShapeDtypeStruct((DEPTH, 16, 6 * D_MODEL), F32),
        compiler_params=_cparams(("arbitrary", "arbitrary")),
        name="modulation",
    )(cvec, w_mod, b_mod.reshape(DEPTH, 1, 6 * D_MODEL))


def _inproj_kernel(has_moe, *refs):
    if has_moe:
        (x_ref, y_ref, rg_ref, modp_ref, mod_ref, n1_ref, w_ref, gains_ref, gb_ref, cd_ref, sd_ref, cg_ref, sg_ref,
         pm_ref, xo_ref, dq_ref, dk_ref, dv_ref, gq_ref, gk_ref, gv_ref, mqk_ref, mv_ref, mo_ref, mg_ref) = refs
    else:
        (x_ref, mod_ref, n1_ref, w_ref, gains_ref, gb_ref, cd_ref, sd_ref, cg_ref, sg_ref,
         pm_ref, dq_ref, dk_ref, dv_ref, gq_ref, gk_ref, gv_ref, mqk_ref, mv_ref, mo_ref, mg_ref) = refs
    x = x_ref[...]
    if has_moe:
        g2 = modp_ref[0, :, 5 * D_MODEL:6 * D_MODEL]
        rg = rg_ref[...]
        x = x + g2 * (rg[:, 0:1] * _load_row_tiles(y_ref, (0,), TILE) + rg[:, 1:2] * _load_row_tiles(y_ref, (1,), TILE))
        xo_ref[...] = x
    sh = mod_ref[0, :, 0:D_MODEL]
    sc = mod_ref[0, :, D_MODEL:2 * D_MODEL]
    xn = x * lax.rsqrt(jnp.mean(x * x, axis=-1, keepdims=True) + EPS) * n1_ref[...]
    h = (xn * (1.0 + sc) + sh).astype(BF16)
    pm = pm_ref[...]

    def proj(a, b):
        return _dot_nt(h, w_ref[a:b, :])

    def qk_group(col, gain_col, inv_dim, cos, sin, outs):
        y = proj(col, col + 512)
        for half in range(2):
            yh = y[:, half * 256:(half + 1) * 256]
            yn = yh * lax.rsqrt(_dot_f32_by_exact2(yh * yh, pm) * inv_dim + EPS)
            yn = yn * gains_ref[:, gain_col + half * 256:gain_col + (half + 1) * 256]
            for j in range(2):
                yb = yn[:, j * LANE:(j + 1) * LANE]
                ref, off = outs[half * 2 + j]
                ref[:, off:off + LANE] = (yb * cos + pltpu.roll(yb, 64, 1) * sin).astype(BF16)

    cd, sd, cg, sg = cd_ref[...], sd_ref[...], cg_ref[...], sg_ref[...]
    qk_group(C_DQ, 0, 1.0 / DIFF_HEAD_DIM, cd, sd, [(dq_ref, b * LANE) for b in range(4)])
    qk_group(C_DK, 512, 1.0 / DIFF_HEAD_DIM, cd, sd, [(dk_ref, b * LANE) for b in range(4)])
    qk_group(C_GQ, 1024, 1.0 / GQA_HEAD_DIM, cg, sg, [(gq_ref, 0), (gq_ref, LANE), (gq_ref, 2 * LANE), (gk_ref, 0)])
    dv_ref[...] = proj(C_DV, C_GQ).astype(BF16)
    rest = proj(C_GV, C_END)
    gv_ref[...] = rest[:, 0:C_MQK - C_GV].astype(BF16)
    mqk_ref[...] = rest[:, C_MQK - C_GV:C_MV - C_GV]
    mv_ref[...] = rest[:, C_MV - C_GV:C_MO - C_GV].astype(BF16)
    mo_ref[...] = rest[:, C_MO - C_GV:C_MG - C_GV]
    mg_ref[...] = rest[:, C_MG - C_GV:C_END - C_GV] + gb_ref[...]


def _inproj(x, moe_in, mod_l, n1g, w_in_l, gains, gate_b, tabs, pair_mat, nt):
    has_moe = moe_in is not None
    n_tiles = nt // TILE
    tile_map = lambda i: (i, 0)
    mod_spec = pl.BlockSpec((1, 1, 6 * D_MODEL), lambda i: (_mod_row(i, SEG_TILES, 0), 0, 0))
    tab_spec = pl.BlockSpec((TILE, LANE), lambda i: (i % SEG_TILES, 0))

    def full(shape):
        return pl.BlockSpec(shape, lambda i: (0,) * len(shape))

    in_specs = [pl.BlockSpec((TILE, D_MODEL), tile_map)]
    args = [x]
    if has_moe:
        y_flat, rg, mod_prev = moe_in
        in_specs += [pl.BlockSpec((2, TILE * ROW_TILES, LANE), lambda i: (0, i, 0)), pl.BlockSpec((TILE, LANE), tile_map), mod_spec]
        args += [y_flat, rg, mod_prev]
    in_specs += [mod_spec, full((1, D_MODEL)), full((C_END, D_MODEL)), full((1, 1536)), full((1, LANE)),
                 tab_spec, tab_spec, tab_spec, tab_spec, full((2 * LANE, 2 * LANE))]
    args += [mod_l, n1g, w_in_l, gains, gate_b, tabs[0], tabs[1], tabs[2], tabs[3], pair_mat]

    def o(width, dtype):
        return pl.BlockSpec((TILE, width), tile_map), jax.ShapeDtypeStruct((nt, width), dtype)

    outs = []
    if has_moe:
        outs.append(o(D_MODEL, F32))
    outs += [o(512, BF16), o(512, BF16), o(512, BF16), o(384, BF16), o(LANE, BF16), o(LANE, BF16),
             o(512, F32), o(256, BF16), o(256, F32), o(LANE, F32)]
    res = pl.pallas_call(
        functools.partial(_inproj_kernel, has_moe),
        grid=(n_tiles,),
        in_specs=in_specs,
        out_specs=[s for s, _ in outs],
        out_shape=[s for _, s in outs],
        compiler_params=_cparams(("arbitrary",)),
        name="inproj_moe" if has_moe else "inproj",
    )(*args)
    if has_moe:
        return res[0], res[1:]
    return x, res


def _lambda_value(lam_ref, lam_init):
    lam = lam_ref[...]
    s01 = jnp.sum(lam[0:1] * lam[1:2], axis=-1, keepdims=True)
    s23 = jnp.sum(lam[2:3] * lam[3:4], axis=-1, keepdims=True)
    return jnp.exp(s01) - jnp.exp(s23) + lam_init


LOG2E = 1.4426950408889634
SAFE_LOG2_RANGE = 60.0


def _exp_scores(s, stabilise):
    if stabilise:
        s = s - jnp.max(s, axis=-1, keepdims=True)
    return jnp.exp2(s).astype(BF16)


def _pv(q, kk, kmask, vv, stabilise):
    return _dot(_exp_scores(_dot_nt(q, kk * kmask), stabilise), vv)


def _score_bound(q_gain, k_gain, head_dim):
    return (1.02 * LOG2E * math.sqrt(head_dim)) * jnp.max(jnp.abs(q_gain), axis=-1) * jnp.max(jnp.abs(k_gain), axis=-1)


def _attn_branches(with_ctx, bound_ref, run, k_ref, v_ref):
    def on_keys(rows):
        small = bound_ref[0] <= SAFE_LOG2_RANGE

        @pl.when(small)
        def _():
            run(k_ref[0:rows, :], v_ref[0:rows, :], False)

        @pl.when(jnp.logical_not(small))
        def _():
            run(k_ref[0:rows, :], v_ref[0:rows, :], True)

    if not with_ctx:
        on_keys(SEG_ROWS)
        return
    t = pl.program_id(2)

    @pl.when(t == 0)
    def _():
        on_keys(TILE)

    @pl.when(t > 0)
    def _():
        on_keys(SEG_ROWS)


def _attn_maps(with_ctx):
    if with_ctx:
        m = lambda b, h, t: (b * SEG_TILES + t, h)
        return SEG_TILES, m, m
    nq = SEG_TILES - 1
    return nq, (lambda b, h, t: (b * SEG_TILES + 1 + t, h)), (lambda b, h, t: (b * nq + t, h))


def _lane_masks():
    lane = np.arange(LANE)
    even = (lane // 32) % 2 == 0
    rows = [even, ~even, lane < 64, lane >= 64, lane == 64, lane == 0, lane == DIFF_V_DIM, lane < 0]
    return jnp.asarray(np.stack(rows).astype(np.float32))


def _mask_row(lm_ref, r):
    return lm_ref[r:r + 1, :].astype(BF16)


def _diff_attn_kernel(lam_init, with_ctx, bound_ref, q_ref, k_ref, v_ref, lam_ref, sg_ref, lm_ref, o_ref):
    def run(kk, vv, stabilise):
        q = q_ref[...]
        v1 = vv + _mask_row(lm_ref, 6)
        o1 = _pv(q, kk, _mask_row(lm_ref, 0), v1, stabilise)
        o2 = _pv(q, kk, _mask_row(lm_ref, 1), v1, stabilise)
        lam = _lambda_value(lam_ref, lam_init)
        o = o1 * (1.0 / o1[:, DIFF_V_DIM:DIFF_V_DIM + 1]) - o2 * (lam / o2[:, DIFF_V_DIM:DIFF_V_DIM + 1])
        o = jnp.where(lax.broadcasted_iota(I32, o.shape, 1) < DIFF_V_DIM, o, 0.0)
        ms = jnp.sum(o * o, axis=-1, keepdims=True) * (1.0 / DIFF_V_DIM)
        o_ref[...] = (o * lax.rsqrt(ms + EPS) * sg_ref[...] * (1.0 - lam_init)).astype(BF16)

    _attn_branches(with_ctx, bound_ref, run, k_ref, v_ref)


def _diff_attention(bound, dq, dk, dv, lam_pad, subln, lam_init, nt, with_ctx):
    nb = nt // SEG_ROWS
    nq, q_map, o_map = _attn_maps(with_ctx)
    kv_map = lambda b, h, t: (b, h)
    const = lambda b, h, t: (0, 0)
    return pl.pallas_call(
        functools.partial(_diff_attn_kernel, lam_init, with_ctx),
        grid=(nb, DIFF_HEADS, nq),
        in_specs=[pl.BlockSpec(memory_space=pltpu.SMEM),
                  pl.BlockSpec((TILE, LANE), q_map), pl.BlockSpec((SEG_ROWS, LANE), kv_map),
                  pl.BlockSpec((SEG_ROWS, LANE), kv_map), pl.BlockSpec((8, LANE), const),
                  pl.BlockSpec((1, LANE), const), pl.BlockSpec((8, LANE), const)],
        out_specs=pl.BlockSpec((TILE, LANE), o_map),
        out_shape=jax.ShapeDtypeStruct((nb * nq * TILE, 512), BF16),
        compiler_params=_cparams(("arbitrary", "arbitrary", "arbitrary")),
        name="diff_attn",
    )(bound, dq, dk, dv, lam_pad, subln, _lane_masks())


def _gqa_attn_kernel(with_ctx, bound_ref, q_ref, k_ref, v_ref, lm_ref, o_ref):
    def run(kk, vv, stabilise):
        q = q_ref[...]
        va = vv * _mask_row(lm_ref, 2) + _mask_row(lm_ref, 4)
        vb = vv * _mask_row(lm_ref, 3) + _mask_row(lm_ref, 5)
        oa = _pv(q, kk, _mask_row(lm_ref, 0), va, stabilise)
        ob = _pv(q, kk, _mask_row(lm_ref, 1), vb, stabilise)
        lane = lax.broadcasted_iota(I32, oa.shape, 1)
        o_ref[...] = jnp.where(lane < GQA_HEAD_DIM, oa * (1.0 / oa[:, GQA_HEAD_DIM:GQA_HEAD_DIM + 1]),
                               ob * (1.0 / ob[:, 0:1])).astype(BF16)

    _attn_branches(with_ctx, bound_ref, run, k_ref, v_ref)


def _gqa_attention(bound, gq, gk, gv, nt, with_ctx):
    nb = nt // SEG_ROWS
    nq, q_map, o_map = _attn_maps(with_ctx)
    kv_map = lambda b, p, t: (b, 0)
    return pl.pallas_call(
        functools.partial(_gqa_attn_kernel, with_ctx),
        grid=(nb, 3, nq),
        in_specs=[pl.BlockSpec(memory_space=pltpu.SMEM),
                  pl.BlockSpec((TILE, LANE), q_map), pl.BlockSpec((SEG_ROWS, LANE), kv_map),
                  pl.BlockSpec((SEG_ROWS, LANE), kv_map), pl.BlockSpec((8, LANE), lambda b, p, t: (0, 0))],
        out_specs=pl.BlockSpec((TILE, LANE), o_map),
        out_shape=jax.ShapeDtypeStruct((nb * nq * TILE, 384), BF16),
        compiler_params=_cparams(("arbitrary", "arbitrary", "arbitrary")),
        name="gqa_attn",
    )(bound, gq, gk, gv, _lane_masks())


def _mlstm_kernel(nbk, mqk_ref, mv_ref, mg_ref, mgt_ref, cw_ref, cb_ref, h_ref, qk_s):
    L = MLSTM_CHUNK
    h_ref[...] = jnp.zeros_like(h_ref)
    w0, w1, w2, cb = cw_ref[0:1, :], cw_ref[1:2, :], cw_ref[2:3, :], cb_ref[...]
    rid = lax.broadcasted_iota(I32, (TILE, 512), 0)
    kscale = jnp.where(lax.broadcasted_iota(I32, (1, 512), 1) < 256, 1.0, MLSTM_HEAD_DIM ** -0.5)
    zrow = jnp.zeros((1, 512), F32)
    for bc in range(nbk * SEG_TILES):
        c = bc % SEG_TILES
        r0 = bc * TILE
        xc = mqk_ref[r0:r0 + TILE, :]
        prev = zrow if c in (0, 1) else mqk_ref[r0 - 1:r0, :]
        nxt = zrow if c in (0, SEG_TILES - 1) else mqk_ref[r0 + TILE:r0 + TILE + 1, :]
        up = jnp.where(rid == 0, prev, pltpu.roll(xc, 1, 0))
        dn = jnp.where(rid == TILE - 1, nxt, pltpu.roll(xc, TILE - 1, 0))
        y = w0 * up + w1 * xc + w2 * dn + cb
        qk_s[r0:r0 + TILE, :] = (_silu(y) * kscale).astype(BF16)

    ti = lax.broadcasted_iota(I32, (L, L), 0)
    si = lax.broadcasted_iota(I32, (L, L), 1)
    tri_le = (si <= ti)
    m_le = tri_le.astype(BF16)
    m_ge = (si >= ti).astype(BF16)
    row2 = lax.broadcasted_iota(I32, (2 * L, L), 0)
    trow = row2 % L
    scol = lax.broadcasted_iota(I32, (2 * L, L), 1)
    top2 = row2 < L
    mask_f = scol <= trow
    mask_b = scol >= trow
    lane_lo = lax.broadcasted_iota(I32, (L, LANE), 1) < L
    lane_lo256 = lax.broadcasted_iota(I32, (L, 2 * LANE), 1) % LANE < L
    rr = lax.broadcasted_iota(I32, (LANE, LANE), 0)
    cc = lax.broadcasted_iota(I32, (LANE, LANE), 1)
    blockdiag = (rr < L) == (cc < L)
    rows_lo = lax.broadcasted_iota(I32, (LANE, 1), 0) < L
    top_col = lax.broadcasted_iota(I32, (2 * L, 1), 0) < L

    def chain(bb, c, is_fwd, p, gcol, bcol_all, grow, brow_all, state):
        ct, nm, m0, m1 = state
        h0, h1 = 2 * p, 2 * p + 1
        gi, gf = (0, 4) if is_fwd else (8, 12)
        r0 = pl.multiple_of(bb * SEG_ROWS + c * L, L)
        qb = qk_s[pl.ds(r0, L), p * LANE:(p + 1) * LANE]
        kb = qk_s[pl.ds(r0, L), 256 + p * LANE:256 + (p + 1) * LANE]
        v128 = mv_ref[pl.ds(r0, L), p * LANE:(p + 1) * LANE]

        def stack_cols(arr, j0, j1):
            return jnp.concatenate([arr[:, j0:j0 + 1], arr[:, j1:j1 + 1]], axis=0)

        def stack_rows(arr, j0, j1):
            return jnp.where(top2, arr[j0:j0 + 1, :], arr[j1:j1 + 1, :])

        bcol = stack_cols(bcol_all, gf + h0, gf + h1)
        licol = stack_cols(gcol, gi + h0, gi + h1)
        crow = stack_rows(grow, gi + h0, gi + h1) - stack_rows(brow_all, gf + h0, gf + h1)
        cm = jnp.where(mask_f if is_fwd else mask_b, crow, -jnp.inf)
        mcol = jnp.where(top_col, m0, m1)
        u = jnp.maximum(mcol, jnp.max(cm, axis=-1, keepdims=True))
        zq = jnp.zeros_like(qb)
        qstack = jnp.concatenate([jnp.where(lane_lo, qb, zq), jnp.where(lane_lo, zq, qb)], axis=0)
        w = (jnp.exp(cm - u) * _dot_nt(qstack, kb)).astype(BF16)
        wv = _dot(w, jnp.concatenate([v128, jnp.ones_like(v128)], axis=1))
        wv = jnp.where(lane_lo256, wv[:L], wv[L:])
        qcn = _dot(qb, jnp.concatenate([ct, nm], axis=1).astype(BF16))
        a = jnp.exp(mcol - u)
        emt = jnp.exp(-(bcol + u))
        a128 = jnp.where(lane_lo, a[:L], a[L:])
        emt128 = jnp.where(lane_lo, emt[:L], emt[L:])
        num = wv[:, :LANE] + a128 * qcn[:, :LANE]
        den = wv[:, LANE:] + a128 * qcn[:, LANE:]
        h_ref[pl.ds(r0, L), p * LANE:(p + 1) * LANE] += num / jnp.maximum(jnp.abs(den), emt128)
        e0 = (L - 1) if is_fwd else 0
        bend0 = bcol[e0:e0 + 1, :]
        bend1 = bcol[L + e0:L + e0 + 1, :]
        bend = jnp.where(top_col, bend0, bend1)
        g = bend - bcol + licol
        m0n = jnp.maximum(bend0 + m0, jnp.max(g[:L], axis=0, keepdims=True))
        m1n = jnp.maximum(bend1 + m1, jnp.max(g[L:], axis=0, keepdims=True))
        ws = jnp.exp(g - jnp.where(top_col, m0n, m1n))
        ae0 = jnp.exp(bend0 + m0 - m0n)
        ae1 = jnp.exp(bend1 + m1 - m1n)
        ws128 = jnp.where(lane_lo, ws[:L], ws[L:])
        vw = jnp.concatenate([v128.astype(F32) * ws128, ws128], axis=1).astype(BF16)
        upd = _dot_tn(kb, vw)
        ae = jnp.where(rows_lo, ae0, ae1)
        ct_new = ae * ct + jnp.where(blockdiag, upd[:, :LANE], 0.0)
        nm_new = ae * nm + jnp.where(blockdiag, upd[:, LANE:], 0.0)
        return ct_new, nm_new, m0n, m1n

    def body(i, carry):
        cf = i
        cbk = jnp.where(i < CTX_CHUNKS, CTX_CHUNKS - 1 - i, N_CHUNKS + CTX_CHUNKS - 1 - i)
        new = []
        for bb in range(nbk):
            for d, c in enumerate((cf, cbk)):
                is_fwd = d == 0
                r0 = pl.multiple_of(bb * SEG_ROWS + c * L, L)
                gcol = mg_ref[pl.ds(r0, L), :]
                grow = mgt_ref[bb, c]
                lf_col = _log_sigmoid(gcol)
                lf_row = _log_sigmoid(grow)
                if is_fwd:
                    bcol_all = _exact_by_dot_f32(m_le, lf_col)
                    brow_all = _dot_f32_by_exact(lf_row, m_ge)
                else:
                    bcol_all = _exact_by_dot_f32(m_ge, lf_col)
                    brow_all = _dot_f32_by_exact(lf_row, m_le)
                for p in range(2):
                    st = carry[(bb * 2 + d) * 2 + p]
                    new.append(chain(bb, c, is_fwd, p, gcol, bcol_all, grow, brow_all, st))
        return tuple(new)

    z = (jnp.zeros((LANE, LANE), F32), jnp.zeros((LANE, LANE), F32), jnp.zeros((1, 1), F32), jnp.zeros((1, 1), F32))
    lax.fori_loop(0, N_CHUNKS, body, (z,) * (4 * nbk), unroll=2)


def _mlstm(mqk, mv, mg, mgt, conv_w, conv_b, nt):
    nb = nt // SEG_ROWS
    nbk = 1
    blk = lambda w: pl.BlockSpec((nbk * SEG_ROWS, w), lambda g: (g, 0))
    return pl.pallas_call(
        functools.partial(_mlstm_kernel, nbk),
        grid=(nb // nbk,),
        in_specs=[blk(512), blk(256), blk(LANE),
                  pl.BlockSpec((nbk, N_CHUNKS, 16, MLSTM_CHUNK), lambda g: (g, 0, 0, 0)),
                  pl.BlockSpec((8, 512), lambda g: (0, 0)), pl.BlockSpec((1, 512), lambda g: (0, 0))],
        out_specs=blk(256),
        out_shape=jax.ShapeDtypeStruct((nt, 256), F32),
        scratch_shapes=[pltpu.VMEM((nbk * SEG_ROWS, 512), BF16)],
        compiler_params=_cparams(("arbitrary",)),
        name="mlstm",
    )(mqk, mv, mg, mgt, conv_w, conv_b)


def _outproj_kernel(do_ref, go_ref, hs_ref, mo_ref, x_ref, mod_ref, n2_ref, hn_ref, wo_ref, wr_hi_ref, wr_lo_ref,
                    rb_ref, hm_ref, xm_ref, f_ref, ri_ref, rg_ref, cnt_ref, carry):
    i = pl.program_id(0)

    @pl.when(i == 0)
    def _():
        carry[...] = jnp.zeros_like(carry)

    hsum = hs_ref[...]
    ssq = _dot_f32_by_exact(hsum * hsum, hm_ref[...])
    ml = hsum * lax.rsqrt(ssq * (1.0 / MLSTM_HEAD_DIM) + EPS) * hn_ref[...] * _sigmoid(mo_ref[...])
    acc = _dot(do_ref[...], wo_ref[0:512, :])
    acc += _dot(go_ref[...], wo_ref[512:896, :])
    acc += _dot(ml.astype(BF16), wo_ref[896:MIX_ROWS, :])
    g1 = mod_ref[0, :, 2 * D_MODEL:3 * D_MODEL]
    sh2 = mod_ref[0, :, 3 * D_MODEL:4 * D_MODEL]
    sc2 = mod_ref[0, :, 4 * D_MODEL:5 * D_MODEL]
    x = x_ref[...] + g1 * acc
    xm_ref[...] = x
    xn = x * lax.rsqrt(jnp.mean(x * x, axis=-1, keepdims=True) + EPS) * n2_ref[...]
    f = xn * (1.0 + sc2) + sh2
    _store_row_tiles(f_ref, (), TILE, f)
    f1, f2, _ = _split3(f)
    logits = _dot(f1, wr_hi_ref[...]) + _dot(f2, wr_hi_ref[...]) + _dot(f1, wr_lo_ref[...]) + rb_ref[...]
    lane = lax.broadcasted_iota(I32, logits.shape, 1)
    neg = jnp.float32(-jnp.inf)
    big = jnp.int32(1 << 20)
    is_g = lane < MOE_GROUPS
    lg = jnp.where(is_g, logits, neg)
    gmax = jnp.max(lg, axis=-1, keepdims=True)
    g_top = 1.0 / jnp.sum(jnp.exp(lg - gmax), axis=-1, keepdims=True)
    g_idx = jnp.min(jnp.where(lg == gmax, lane, big), axis=-1, keepdims=True)
    in_grp = (lane >= MOE_GROUPS) & (lane < MOE_GROUPS + MOE_EXPERTS) & ((lane - MOE_GROUPS) // MOE_EPG == g_idx)
    le = jnp.where(in_grp, logits, neg)
    v1 = jnp.max(le, axis=-1, keepdims=True)
    l1 = jnp.min(jnp.where(le == v1, lane, big), axis=-1, keepdims=True)
    le2 = jnp.where(lane == l1, neg, le)
    v2 = jnp.max(le2, axis=-1, keepdims=True)
    l2 = jnp.min(jnp.where(le2 == v2, lane, big), axis=-1, keepdims=True)
    ex = jnp.exp(v2 - v1)
    gate1 = g_top / (1.0 + ex)
    gate2 = gate1 * ex
    oh1 = (lane == l1)
    oh2 = (lane == l2)
    both = oh1.astype(BF16) + oh2.astype(BF16)
    ri_ = lax.broadcasted_iota(I32, (TILE, TILE), 0)
    ci_ = lax.broadcasted_iota(I32, (TILE, TILE), 1)
    before = _dot((ci_ < ri_).astype(BF16), both) + carry[0:1, :]
    rank1 = jnp.sum(jnp.where(oh1, before, 0.0), axis=-1, keepdims=True)
    rank2 = jnp.sum(jnp.where(oh2, before, 0.0), axis=-1, keepdims=True)
    carry[0:1, :] = carry[0:1, :] + jnp.sum(both.astype(F32), axis=0, keepdims=True)
    cnt_ref[...] = jnp.broadcast_to(carry[0:1, :], cnt_ref.shape)
    ri = jnp.where(lane == 0, l1 - MOE_GROUPS, jnp.where(lane == 1, l2 - MOE_GROUPS,
         jnp.where(lane == 2, rank1.astype(I32), jnp.where(lane == 3, rank2.astype(I32), 0))))
    ri_ref[...] = ri
    rg_ref[...] = jnp.where(lane == 0, gate1, jnp.where(lane == 1, gate2, 0.0))


def _outproj(do, go, hs, mo, x, mod_l, n2g, hn, w_out_l, wr_hi, wr_lo, rb, head_mat, nt, with_ctx):
    nb = nt // SEG_ROWS
    nq, off = (SEG_TILES, 0) if with_ctx else (SEG_TILES - 1, 1)
    n_steps = nb * nq
    n_out = n_steps * TILE
    in_map = lambda i: (_seg_tile(i, nq, off), 0)
    out_map = lambda i: (i, 0)

    def full(shape):
        return pl.BlockSpec(shape, lambda i: (0,) * len(shape))

    in_specs = [pl.BlockSpec((TILE, 512), out_map), pl.BlockSpec((TILE, 384), out_map), pl.BlockSpec((TILE, 256), in_map),
                pl.BlockSpec((TILE, 256), in_map), pl.BlockSpec((TILE, D_MODEL), in_map),
                pl.BlockSpec((1, 1, 6 * D_MODEL), lambda i: (_mod_row(i, nq, off), 0, 0)),
                full((1, D_MODEL)), full((1, 256)), full((MIX_ROWS, D_MODEL)), full((D_MODEL, LANE)), full((D_MODEL, LANE)),
                full((1, LANE)), full((256, 256))]
    out_specs = [pl.BlockSpec((TILE, D_MODEL), out_map), pl.BlockSpec((TILE * ROW_TILES, LANE), out_map),
                 pl.BlockSpec((TILE, LANE), out_map), pl.BlockSpec((TILE, LANE), out_map), pl.BlockSpec((8, LANE), lambda i: (0, 0))]
    out_shape = [jax.ShapeDtypeStruct((n_out, D_MODEL), F32), jax.ShapeDtypeStruct((n_out * ROW_TILES, LANE), F32),
                 jax.ShapeDtypeStruct((n_out, LANE), I32), jax.ShapeDtypeStruct((n_out, LANE), F32),
                 jax.ShapeDtypeStruct((8, LANE), F32)]
    return pl.pallas_call(
        _outproj_kernel,
        grid=(n_steps,),
        in_specs=in_specs,
        out_specs=out_specs,
        out_shape=out_shape,
        scratch_shapes=[pltpu.VMEM((8, LANE), F32)],
        compiler_params=_cparams(("arbitrary",)),
        name="outproj_route",
    )(do, go, hs, mo, x, mod_l, n2g, hn, w_out_l, wr_hi, wr_lo, rb, head_mat)


DMA_UNROLL = 8


def _for_rows(n, fn):
    groups = lax.shift_right_logical(n, int(math.log2(DMA_UNROLL)))

    def group(g, _):
        for u in range(DMA_UNROLL):
            fn(g * DMA_UNROLL + u, u % 2)
        return 0

    def single(r, _):
        fn(r, 0)
        return 0

    lax.fori_loop(0, groups, group, 0)
    lax.fori_loop(groups * DMA_UNROLL, n, single, 0)


def _expert_kernel(be_ref, nv_ref, src_ref, srcn_ref, dst_ref, f_hbm, w1_ref, w3_ref, w2_ref, y_hbm,
                   xbuf, ybuf, w1s, w3s, w2s, gsem, ssem):
    i = pl.program_id(0)
    nv = nv_ref[i]
    nv_next = nv_ref[i + 1]
    slot = i % 2

    def tile(r):
        return pl.ds(pl.multiple_of(r * ROW_TILES, ROW_TILES), ROW_TILES)

    def gather_copy(idx_ref, s, r):
        return pltpu.make_async_copy(f_hbm.at[tile(idx_ref[0, 0, r])], xbuf.at[s, tile(r)], gsem.at[s])

    def scatter_copy(r):
        return pltpu.make_async_copy(ybuf.at[tile(r)], y_hbm.at[tile(dst_ref[0, 0, r])], ssem)

    def rows(n):
        return pl.ds(0, pl.multiple_of(n * ROW_TILES, ROW_TILES))

    def gather_wait(n):
        pltpu.make_async_copy(f_hbm.at[rows(n)], xbuf.at[slot, rows(n)], gsem.at[slot]).wait()

    def scatter_wait(n):
        pltpu.make_async_copy(ybuf.at[rows(n)], y_hbm.at[rows(n)], ssem).wait()

    @pl.when(i == 0)
    def _():
        xbuf[...] = jnp.zeros_like(xbuf)
        _for_rows(nv, lambda r, pr: gather_copy(src_ref, 0, r).start(priority=pr))

    @pl.when(nv_next > 0)
    def _():
        _for_rows(nv_next, lambda r, pr: gather_copy(srcn_ref, 1 - slot, r).start(priority=pr))

    @pl.when((nv > 0) & ((i == 0) | (be_ref[i] != be_ref[jnp.maximum(i - 1, 0)])))
    def _():
        w1s[...] = w1_ref[0, 0].astype(BF16)
        w3s[...] = w3_ref[0, 0].astype(BF16)
        w2s[...] = w2_ref[0, 0].astype(BF16)

    @pl.when(nv > 0)
    def _():
        gather_wait(nv)
        xb = _load_row_tiles(xbuf, (slot,), MOE_TILE).astype(BF16)
        hh = _silu(_dot(xb, w1s[...])) * _dot(xb, w3s[...])
        y = _dot(hh.astype(BF16), w2s[...])

        @pl.when(i > 0)
        def _():
            scatter_wait(nv_ref[jnp.maximum(i - 1, 0)])

        _store_row_tiles(ybuf, (), MOE_TILE, y)
        _for_rows(nv, lambda r, pr: scatter_copy(r).start(priority=pr))

        @pl.when(nv_next == 0)
        def _():
            scatter_wait(nv)


def _experts(f, row_src, row_dst, blk_expert, blk_valid, w1, w3, w2, layer, n_tok):
    n_blk = row_src.shape[0]
    idx_spec = lambda fn: pl.BlockSpec((1, 1, MOE_TILE), fn, memory_space=pltpu.SMEM)
    w_spec = lambda shape: pl.BlockSpec((1, 1) + shape, lambda i, be, nv: (layer, be[i], 0, 0))
    grid_spec = pltpu.PrefetchScalarGridSpec(
        num_scalar_prefetch=2,
        grid=(n_blk,),
        in_specs=[idx_spec(lambda i, be, nv: (i, 0, 0)),
                  idx_spec(lambda i, be, nv: (jnp.minimum(i + 1, n_blk - 1), 0, 0)),
                  idx_spec(lambda i, be, nv: (i, 0, 0)),
                  pl.BlockSpec(memory_space=pl.ANY),
                  w_spec((D_MODEL, MOE_HIDDEN)), w_spec((D_MODEL, MOE_HIDDEN)), w_spec((MOE_HIDDEN, D_MODEL))],
        out_specs=pl.BlockSpec(memory_space=pl.ANY),
        scratch_shapes=[pltpu.VMEM((2, MOE_TILE * ROW_TILES, LANE), F32), pltpu.VMEM((MOE_TILE * ROW_TILES, LANE), F32),
                        pltpu.VMEM((D_MODEL, MOE_HIDDEN), BF16), pltpu.VMEM((D_MODEL, MOE_HIDDEN), BF16),
                        pltpu.VMEM((MOE_HIDDEN, D_MODEL), BF16),
                        pltpu.SemaphoreType.DMA((2,)), pltpu.SemaphoreType.DMA(())],
    )
    return pl.pallas_call(
        _expert_kernel,
        grid_spec=grid_spec,
        out_shape=jax.ShapeDtypeStruct((2 * n_tok * ROW_TILES, LANE), F32),
        compiler_params=_cparams(("arbitrary",)),
        name="experts",
    )(blk_expert, blk_valid, row_src, row_src, row_dst, f, w1, w3, w2).reshape(2, n_tok * ROW_TILES, LANE)


def _dest_kernel(ri_ref, ps_ref, o_ref):
    ri = ri_ref[...]
    lane = lax.broadcasted_iota(I32, ri.shape, 1)
    ps = ps_ref[...]

    def dest(k):
        start = jnp.sum(jnp.where(lane == ri[:, k:k + 1], ps, 0.0), axis=-1, keepdims=True)
        return start.astype(I32) + ri[:, 2 + k:3 + k]

    o_ref[...] = jnp.where(lane == 0, dest(0), jnp.where(lane == 1, dest(1), 0))


def _pair_dest(ri, pad_start, n_tok):
    ps = jnp.pad(pad_start.astype(F32), (0, LANE - MOE_EXPERTS)).reshape(1, LANE)
    return pl.pallas_call(
        _dest_kernel,
        grid=(n_tok // TILE,),
        in_specs=[pl.BlockSpec((TILE, LANE), lambda i: (i, 0)), pl.BlockSpec((1, LANE), lambda i: (0, 0))],
        out_specs=pl.BlockSpec((TILE, LANE), lambda i: (i, 0)),
        out_shape=jax.ShapeDtypeStruct((n_tok, LANE), I32),
        compiler_params=_cparams(("arbitrary",)),
        name="pair_dest",
    )(ri, ps)


def _route_plan(ri, counts, n_tok):
    n_blk = (2 * n_tok) // MOE_TILE + MOE_EXPERTS
    p_rows = n_blk * MOE_TILE
    cnt = counts[0, MOE_GROUPS:MOE_GROUPS + MOE_EXPERTS].astype(I32)
    padded = (cnt + MOE_TILE - 1) // MOE_TILE * MOE_TILE
    pad_end = jnp.cumsum(padded)
    pad_start = pad_end - padded
    dest = _pair_dest(ri, pad_start, n_tok)[:, 0:2]
    pair = jnp.zeros((p_rows,), I32).at[dest.reshape(-1)].set(jnp.arange(2 * n_tok, dtype=I32))
    row_src = pair // 2
    row_dst = (pair % 2) * n_tok + row_src
    blk_start = jnp.arange(n_blk + 1, dtype=I32) * MOE_TILE
    blk_expert = jnp.minimum(jnp.sum(pad_end[None, :] <= blk_start[:, None], axis=-1), MOE_EXPERTS - 1).astype(I32)
    in_expert = blk_start - pad_start[blk_expert]
    blk_valid = jnp.where(blk_start < pad_end[-1], jnp.clip(cnt[blk_expert] - in_expert, 0, MOE_TILE), 0).astype(I32)
    return (row_src.reshape(n_blk, 1, MOE_TILE), row_dst.reshape(n_blk, 1, MOE_TILE), blk_expert[:n_blk], blk_valid)


def _final_kernel(x_ref, y_ref, rg_ref, mod_ref, o_ref):
    g2 = mod_ref[0, :, 5 * D_MODEL:6 * D_MODEL]
    rg = rg_ref[...]
    o_ref[...] = x_ref[...] + g2 * (rg[:, 0:1] * _load_row_tiles(y_ref, (0,), TILE) + rg[:, 1:2] * _load_row_tiles(y_ref, (1,), TILE))


def _final(x, y_flat, rg, mod_l, n_tok):
    nq = SEQ // TILE
    tile_map = lambda i: (i, 0)
    return pl.pallas_call(
        _final_kernel,
        grid=(n_tok // TILE,),
        in_specs=[pl.BlockSpec((TILE, D_MODEL), tile_map), pl.BlockSpec((2, TILE * ROW_TILES, LANE), lambda i: (0, i, 0)),
                  pl.BlockSpec((TILE, LANE), tile_map), pl.BlockSpec((1, 1, 6 * D_MODEL), lambda i: (i // nq, 0, 0))],
        out_specs=pl.BlockSpec((TILE, D_MODEL), tile_map),
        out_shape=jax.ShapeDtypeStruct((n_tok, D_MODEL), F32),
        compiler_params=_cparams(("arbitrary",)),
        name="final_residual",
    )(x, y_flat, rg, mod_l)


def _diff_rows(w, heads):
    nl, _, k = w.shape
    w = w.reshape(nl, heads, 2, 2, 2, 12, k)
    w = w.transpose(0, 1, 4, 2, 3, 5, 6)
    w = jnp.pad(w.reshape(nl, heads, 4, 24, k), ((0, 0), (0, 0), (0, 0), (0, 8), (0, 0)))
    return w.reshape(nl, heads * LANE, k)


def _gqa_rows(w, groups):
    nl, _, k = w.shape
    w = w.reshape(nl, 2, groups, 2, 2, 16, k)
    w = w.transpose(0, 2, 4, 1, 3, 5, 6)
    return w.reshape(nl, groups * LANE, k)


def _in_weight(w_in):
    wt = jnp.swapaxes(w_in, 1, 2)
    nl = wt.shape[0]
    dv = jnp.pad(wt[:, 768:1152].reshape(nl, DIFF_HEADS, DIFF_V_DIM, D_MODEL), ((0, 0), (0, 0), (0, LANE - DIFF_V_DIM), (0, 0)))
    parts = [_diff_rows(wt[:, 0:384], DIFF_HEADS), _diff_rows(wt[:, 384:768], DIFF_HEADS),
             dv.reshape(nl, DIFF_HEADS * LANE, D_MODEL), _gqa_rows(wt[:, 1152:1536], 3), _gqa_rows(wt[:, 1536:1664], 1),
             wt[:, 1664:2816], jnp.pad(wt[:, 2816:2832], ((0, 0), (0, LANE - 16), (0, 0)))]
    return jnp.concatenate(parts, axis=1).astype(BF16)


def _lane_gain(g, rows_fn):
    return rows_fn(jnp.concatenate([g, g], axis=-1)[:, :, None], 1)[:, :, 0]


def _out_weight(w_out):
    nl = w_out.shape[0]
    diff = jnp.pad(w_out[:, 0:384].reshape(nl, DIFF_HEADS, DIFF_V_DIM, D_MODEL), ((0, 0), (0, 0), (0, LANE - DIFF_V_DIM), (0, 0)))
    gqa = w_out[:, 384:768].reshape(nl, 2, 3, GQA_HEAD_DIM, D_MODEL).transpose(0, 2, 1, 3, 4)
    return jnp.concatenate([diff.reshape(nl, 512, D_MODEL), gqa.reshape(nl, 384, D_MODEL), w_out[:, 768:]], axis=1).astype(BF16)


def kernel(x, c, ctx, c_ctx, norm1_g, norm2_g, w_mod, b_mod, w_in, w_out, diff_q_norm, diff_k_norm, diff_lambda, diff_subln, gqa_q_norm, gqa_k_norm, mlstm_conv_w, mlstm_conv_b, mlstm_gate_b, mlstm_head_norm, moe_wg, moe_bg, moe_we, moe_be, moe_w1, moe_w3, moe_w2):
    B = x.shape[0]
    nt = B * SEG_ROWS
    n_lat = B * SEQ

    w_in_r = _in_weight(w_in)
    w_out_r = _out_weight(w_out)
    gq_d = _lane_gain(diff_q_norm, _diff_rows) * (LOG2E * DIFF_HEAD_DIM ** -0.5)
    gk_d = _lane_gain(diff_k_norm, _diff_rows)
    gq_g = _lane_gain(gqa_q_norm, _gqa_rows) * (LOG2E * GQA_HEAD_DIM ** -0.5)
    gk_g = _lane_gain(gqa_k_norm, _gqa_rows)
    bound_d = _score_bound(diff_q_norm, diff_k_norm, DIFF_HEAD_DIM).reshape(DEPTH, 1)
    bound_g = _score_bound(gqa_q_norm, gqa_k_norm, GQA_HEAD_DIM).reshape(DEPTH, 1)
    gains = jnp.concatenate([jnp.tile(gq_d, (1, 4)), jnp.tile(gk_d, (1, 4)), jnp.tile(gq_g, (1, 3)), gk_g], axis=1)
    gains = gains.reshape(DEPTH, 1, 1536)
    gate_b = jnp.pad(mlstm_gate_b, ((0, 0), (0, LANE - 16))).reshape(DEPTH, 1, LANE)
    subln = jnp.pad(diff_subln, ((0, 0), (0, LANE - DIFF_V_DIM))).reshape(DEPTH, 1, LANE)
    lam_pad = jnp.pad(diff_lambda, ((0, 0), (0, 4), (0, LANE - DIFF_HEAD_DIM)))
    conv_w = jnp.pad(mlstm_conv_w, ((0, 0), (0, 5), (0, 0)))
    conv_b = mlstm_conv_b.reshape(DEPTH, 1, 512)
    w_r = jnp.pad(jnp.concatenate([moe_wg, moe_we], axis=2), ((0, 0), (0, 0), (0, LANE - 36)))
    wr_hi = w_r.astype(BF16)
    wr_lo = (w_r - wr_hi.astype(F32)).astype(BF16)
    rb = jnp.pad(jnp.concatenate([moe_bg, moe_be], axis=1), ((0, 0), (0, LANE - 36))).reshape(DEPTH, 1, LANE)
    pair_mat = jnp.asarray(_PAIR_MAT, BF16)
    head_mat = jnp.asarray((np.arange(256)[:, None] // 64 == np.arange(256)[None, :] // 64).astype(np.float32), BF16)
    tabs = _rope_tables(12, 24) + _rope_tables(16, 32)

    xa = jnp.concatenate([ctx, x], axis=1).reshape(nt, D_MODEL)
    assert B <= 8, "row 8 of the modulation table is reserved for the context conditioning"
    cvec = jnp.concatenate([c, jnp.zeros((8 - B, D_MODEL), F32), c_ctx[None, :], jnp.zeros((7, D_MODEL), F32)], axis=0)
    mod = _modulation(cvec, w_mod, b_mod).reshape(DEPTH, 16, 1, 6 * D_MODEL)

    moe_in = None
    for l in range(DEPTH):
        last = l == DEPTH - 1
        lam_init = 0.8 - 0.6 * math.exp(-0.3 * l)
        xa, (dq, dk, dv, gq, gk, gv, mqk, mv, mo, mg) = _inproj(
            xa, moe_in, mod[l], norm1_g[l].reshape(1, D_MODEL), w_in_r[l], gains[l], gate_b[l], tabs, pair_mat, nt)
        do = _diff_attention(bound_d[l], dq, dk, dv, lam_pad[l], subln[l], lam_init, nt, with_ctx=not last)
        go = _gqa_attention(bound_g[l], gq, gk, gv, nt, with_ctx=not last)
        mgt = mg[:, :16].reshape(B, N_CHUNKS, MLSTM_CHUNK, 16).transpose(0, 1, 3, 2)
        hs = _mlstm(mqk, mv, mg, mgt, conv_w[l], conv_b[l], nt)
        xm, f, ri, rg, counts = _outproj(do, go, hs, mo, xa, mod[l], norm2_g[l].reshape(1, D_MODEL),
                                         mlstm_head_norm[l].reshape(1, 256), w_out_r[l], wr_hi[l], wr_lo[l], rb[l],
                                         head_mat, nt, with_ctx=not last)
        n_tok = n_lat if last else nt
        row_src, row_dst, blk_expert, blk_valid = _route_plan(ri, counts, n_tok)
        y = _experts(f, row_src, row_dst, blk_expert, blk_valid, moe_w1, moe_w3, moe_w2, l, n_tok)
        if last:
            out = _final(xm, y, rg, mod[l][:B], n_tok)
            return out.reshape(B, SEQ, D_MODEL)
        xa = xm
        moe_in = (y, rg, mod[l])
```

```python
import functools
import math

import numpy as np
import jax
import jax.numpy as jnp
from jax import lax
from jax.experimental import pallas as pl
from jax.experimental.pallas import tpu as pltpu

F32 = jnp.float32
BF16 = jnp.bfloat16
I32 = jnp.int32

D_MODEL = 1024
DEPTH = 4
GRID_W = 64
CTX_LEN = 256
SEQ = 2048
ROPE_THETA = 10000.0
EPS = 1e-6

DIFF_HEADS = 4
DIFF_HEAD_DIM = 48
DIFF_V_DIM = 96
GQA_Q_HEADS = 6
GQA_KV_HEADS = 2
GQA_HEAD_DIM = 64
MLSTM_HEADS = 4
MLSTM_HEAD_DIM = 64
MLSTM_CHUNK = 64
IN_WIDTH = 2832
MOE_GROUPS = 4
MOE_EPG = 8
MOE_EXPERTS = 32
MOE_HIDDEN = 512

LANE = 128
TILE = 256
SEG_TILES = (CTX_LEN + SEQ) // TILE
SEG_ROWS = CTX_LEN + SEQ
N_CHUNKS = SEG_ROWS // MLSTM_CHUNK
CTX_CHUNKS = CTX_LEN // MLSTM_CHUNK
MOE_TILE = 256
VMEM_LIMIT = 56 * 1024 * 1024

C_DQ, C_DK, C_DV, C_GQ, C_GK, C_GV, C_MQK, C_MV, C_MO, C_MG, C_END = (
    0, 512, 1024, 1536, 1920, 2048, 2176, 2688, 2944, 3200, 3328)
MIX_ROWS = 512 + 384 + 256


def _cparams(sem):
    return pltpu.CompilerParams(dimension_semantics=sem, vmem_limit_bytes=VMEM_LIMIT)


_LANE2 = np.arange(2 * LANE)
_PAIR_MAT = ((_LANE2[:, None] // LANE == _LANE2[None, :] // LANE)
             & ((_LANE2[:, None] // 32) % 2 == (_LANE2[None, :] // 32) % 2)).astype(np.float32)


def _rope_tables(nf, pad_from):
    t = jnp.arange(SEQ, dtype=I32)
    rows = (t // GRID_W).astype(F32)
    cols = (t % GRID_W).astype(F32)
    freqs = ROPE_THETA ** (-jnp.arange(nf, dtype=F32) / nf)
    lane = np.arange(LANE)
    i = lane % 32
    typ = (lane // 32) // 2
    use_rows = i < nf
    fidx = np.where(use_rows, i, i - nf)
    valid = i < pad_from
    fidx = np.where(valid, fidx, 0)
    ang = jnp.where(jnp.asarray(use_rows)[None, :], rows[:, None], cols[:, None]) * freqs[jnp.asarray(fidx)][None, :]
    cos = jnp.where(jnp.asarray(valid)[None, :], jnp.cos(ang), 1.0)
    sin = jnp.where(jnp.asarray(valid)[None, :], jnp.sin(ang), 0.0)
    sin = sin * jnp.asarray(np.where(typ == 0, -1.0, 1.0), F32)[None, :]
    cos = jnp.concatenate([jnp.ones((CTX_LEN, LANE), F32), cos], axis=0)
    sin = jnp.concatenate([jnp.zeros((CTX_LEN, LANE), F32), sin], axis=0)
    return cos, sin


def _dot(a, b):
    return jnp.dot(a, b, preferred_element_type=F32)


def _dot_nt(a, b):
    return lax.dot_general(a, b, (((1,), (1,)), ((), ())), preferred_element_type=F32)


def _dot_tn(a, b):
    return lax.dot_general(a, b, (((0,), (0,)), ((), ())), preferred_element_type=F32)


def _split3(x):
    x1 = x.astype(BF16)
    r1 = x - x1.astype(F32)
    x2 = r1.astype(BF16)
    x3 = (r1 - x2.astype(F32)).astype(BF16)
    return x1, x2, x3


def _dot_f32_by_exact(x, m):
    x1, x2, x3 = _split3(x)
    return _dot(x1, m) + _dot(x2, m) + _dot(x3, m)


def _dot_f32_by_exact2(x, m):
    x1 = x.astype(BF16)
    x2 = (x - x1.astype(F32)).astype(BF16)
    return _dot(x1, m) + _dot(x2, m)


def _exact_by_dot_f32(m, x):
    x1, x2, x3 = _split3(x)
    return _dot(m, x1) + _dot(m, x2) + _dot(m, x3)


def _sigmoid(x):
    return 1.0 / (1.0 + jnp.exp(-x))


def _silu(x):
    return x * _sigmoid(x)


def _log_sigmoid(x):
    return jnp.minimum(x, 0.0) - jnp.log1p(jnp.exp(-jnp.abs(x)))


ROW_TILES = D_MODEL // LANE


def _store_row_tiles(ref, lead, rows, x):
    for c in range(ROW_TILES):
        ref[lead + (pl.ds(c, rows, stride=ROW_TILES), slice(None))] = x[:, c * LANE:(c + 1) * LANE]


def _load_row_tiles(ref, lead, rows):
    return jnp.concatenate([ref[lead + (pl.ds(c, rows, stride=ROW_TILES), slice(None))] for c in range(ROW_TILES)], axis=1)


def _seg_tile(i, nq, off):
    return (i // nq) * SEG_TILES + off + i % nq


def _mod_row(i, nq, off):
    return jnp.where((off + i % nq) == 0, 8, i // nq)


MOD_BN = 1536


def _mod_kernel(c_ref, w_ref, b_ref, o_ref):
    a = _silu(c_ref[...]).astype(BF16)
    o_ref[0] = _dot(a, w_ref[0].astype(BF16)) + b_ref[0]


def _modulation(cvec, w_mod, b_mod):
    nb = 6 * D_MODEL // MOD_BN
    return pl.pallas_call(
        _mod_kernel,
        grid=(DEPTH, nb),
        in_specs=[pl.BlockSpec((16, D_MODEL), lambda l, n: (0, 0)),
                  pl.BlockSpec((1, D_MODEL, MOD_BN), lambda l, n: (l, 0, n)),
                  pl.BlockSpec((1, 1, MOD_BN), lambda l, n: (l, 0, n))],
        out_specs=pl.BlockSpec((1, 16, MOD_BN), lambda l, n: (l, 0, n)),
        out_shape=jax.ShapeDtypeStruct((DEPTH, 16, 6 * D_MODEL), F32),
        compiler_params=_cparams(("arbitrary", "arbitrary")),
        name="modulation",
    )(cvec, w_mod, b_mod.reshape(DEPTH, 1, 6 * D_MODEL))


def _inproj_kernel(has_moe, *refs):
    if has_moe:
        (x_ref, y_ref, rg_ref, modp_ref, mod_ref, n1_ref, w_ref, gains_ref, gb_ref, cd_ref, sd_ref, cg_ref, sg_ref,
         pm_ref, xo_ref, dq_ref, dk_ref, dv_ref, gq_ref, gk_ref, gv_ref, mqk_ref, mv_ref, mo_ref, mg_ref) = refs
    else:
        (x_ref, mod_ref, n1_ref, w_ref, gains_ref, gb_ref, cd_ref, sd_ref, cg_ref, sg_ref,
         pm_ref, dq_ref, dk_ref, dv_ref, gq_ref, gk_ref, gv_ref, mqk_ref, mv_ref, mo_ref, mg_ref) = refs
    x = x_ref[...]
    if has_moe:
        g2 = modp_ref[0, :, 5 * D_MODEL:6 * D_MODEL]
        rg = rg_ref[...]
        x = x + g2 * (rg[:, 0:1] * _load_row_tiles(y_ref, (0,), TILE) + rg[:, 1:2] * _load_row_tiles(y_ref, (1,), TILE))
        xo_ref[...] = x
    sh = mod_ref[0, :, 0:D_MODEL]
    sc = mod_ref[0, :, D_MODEL:2 * D_MODEL]
    xn = x * lax.rsqrt(jnp.mean(x * x, axis=-1, keepdims=True) + EPS) * n1_ref[...]
    h = (xn * (1.0 + sc) + sh).astype(BF16)
    pm = pm_ref[...]

    def proj(a, b):
        return _dot_nt(h, w_ref[a:b, :])

    def qk_group(col, gain_col, inv_dim, cos, sin, outs):
        y = proj(col, col + 512)
        for half in range(2):
            yh = y[:, half * 256:(half + 1) * 256]
            yn = yh * lax.rsqrt(_dot_f32_by_exact2(yh * yh, pm) * inv_dim + EPS)
            yn = yn * gains_ref[:, gain_col + half * 256:gain_col + (half + 1) * 256]
            for j in range(2):
                yb = yn[:, j * LANE:(j + 1) * LANE]
                ref, off = outs[half * 2 + j]
                ref[:, off:off + LANE] = (yb * cos + pltpu.roll(yb, 64, 1) * sin).astype(BF16)

    cd, sd, cg, sg = cd_ref[...], sd_ref[...], cg_ref[...], sg_ref[...]
    qk_group(C_DQ, 0, 1.0 / DIFF_HEAD_DIM, cd, sd, [(dq_ref, b * LANE) for b in range(4)])
    qk_group(C_DK, 512, 1.0 / DIFF_HEAD_DIM, cd, sd, [(dk_ref, b * LANE) for b in range(4)])
    qk_group(C_GQ, 1024, 1.0 / GQA_HEAD_DIM, cg, sg, [(gq_ref, 0), (gq_ref, LANE), (gq_ref, 2 * LANE), (gk_ref, 0)])
    dv_ref[...] = proj(C_DV, C_GQ).astype(BF16)
    rest = proj(C_GV, C_END)
    gv_ref[...] = rest[:, 0:C_MQK - C_GV].astype(BF16)
    mqk_ref[...] = rest[:, C_MQK - C_GV:C_MV - C_GV]
    mv_ref[...] = rest[:, C_MV - C_GV:C_MO - C_GV].astype(BF16)
    mo_ref[...] = rest[:, C_MO - C_GV:C_MG - C_GV]
    mg_ref[...] = rest[:, C_MG - C_GV:C_END - C_GV] + gb_ref[...]


def _inproj(x, moe_in, mod_l, n1g, w_in_l, gains, gate_b, tabs, pair_mat, nt):
    has_moe = moe_in is not None
    n_tiles = nt // TILE
    tile_map = lambda i: (i, 0)
    mod_spec = pl.BlockSpec((1, 1, 6 * D_MODEL), lambda i: (_mod_row(i, SEG_TILES, 0), 0, 0))
    tab_spec = pl.BlockSpec((TILE, LANE), lambda i: (i % SEG_TILES, 0))

    def full(shape):
        return pl.BlockSpec(shape, lambda i: (0,) * len(shape))

    in_specs = [pl.BlockSpec((TILE, D_MODEL), tile_map)]
    args = [x]
    if has_moe:
        y_flat, rg, mod_prev = moe_in
        in_specs += [pl.BlockSpec((2, TILE * ROW_TILES, LANE), lambda i: (0, i, 0)), pl.BlockSpec((TILE, LANE), tile_map), mod_spec]
        args += [y_flat, rg, mod_prev]
    in_specs += [mod_spec, full((1, D_MODEL)), full((C_END, D_MODEL)), full((1, 1536)), full((1, LANE)),
                 tab_spec, tab_spec, tab_spec, tab_spec, full((2 * LANE, 2 * LANE))]
    args += [mod_l, n1g, w_in_l, gains, gate_b, tabs[0], tabs[1], tabs[2], tabs[3], pair_mat]

    def o(width, dtype):
        return pl.BlockSpec((TILE, width), tile_map), jax.ShapeDtypeStruct((nt, width), dtype)

    outs = []
    if has_moe:
        outs.append(o(D_MODEL, F32))
    outs += [o(512, BF16), o(512, BF16), o(512, BF16), o(384, BF16), o(LANE, BF16), o(LANE, BF16),
             o(512, F32), o(256, BF16), o(256, F32), o(LANE, F32)]
    res = pl.pallas_call(
        functools.partial(_inproj_kernel, has_moe),
        grid=(n_tiles,),
        in_specs=in_specs,
        out_specs=[s for s, _ in outs],
        out_shape=[s for _, s in outs],
        compiler_params=_cparams(("arbitrary",)),
        name="inproj_moe" if has_moe else "inproj",
    )(*args)
    if has_moe:
        return res[0], res[1:]
    return x, res


def _lambda_value(lam_ref, lam_init):
    lam = lam_ref[...]
    s01 = jnp.sum(lam[0:1] * lam[1:2], axis=-1, keepdims=True)
    s23 = jnp.sum(lam[2:3] * lam[3:4], axis=-1, keepdims=True)
    return jnp.exp(s01) - jnp.exp(s23) + lam_init


LOG2E = 1.4426950408889634
SAFE_LOG2_RANGE = 60.0


def _exp_scores(s, stabilise):
    if stabilise:
        s = s - jnp.max(s, axis=-1, keepdims=True)
    return jnp.exp2(s).astype(BF16)


def _pv(q, kk, kmask, vv, stabilise):
    return _dot(_exp_scores(_dot_nt(q, kk * kmask), stabilise), vv)


def _score_bound(q_gain, k_gain, head_dim):
    return (1.02 * LOG2E * math.sqrt(head_dim)) * jnp.max(jnp.abs(q_gain), axis=-1) * jnp.max(jnp.abs(k_gain), axis=-1)


def _attn_branches(with_ctx, bound_ref, run, k_ref, v_ref):
    def on_keys(rows):
        small = bound_ref[0] <= SAFE_LOG2_RANGE

        @pl.when(small)
        def _():
            run(k_ref[0:rows, :], v_ref[0:rows, :], False)

        @pl.when(jnp.logical_not(small))
        def _():
            run(k_ref[0:rows, :], v_ref[0:rows, :], True)

    if not with_ctx:
        on_keys(SEG_ROWS)
        return
    t = pl.program_id(2)

    @pl.when(t == 0)
    def _():
        on_keys(TILE)

    @pl.when(t > 0)
    def _():
        on_keys(SEG_ROWS)


def _attn_maps(with_ctx):
    if with_ctx:
        m = lambda b, h, t: (b * SEG_TILES + t, h)
        return SEG_TILES, m, m
    nq = SEG_TILES - 1
    return nq, (lambda b, h, t: (b * SEG_TILES + 1 + t, h)), (lambda b, h, t: (b * nq + t, h))


def _lane_masks():
    lane = np.arange(LANE)
    even = (lane // 32) % 2 == 0
    rows = [even, ~even, lane < 64, lane >= 64, lane == 64, lane == 0, lane == DIFF_V_DIM, lane < 0]
    return jnp.asarray(np.stack(rows).astype(np.float32))


def _mask_row(lm_ref, r):
    return lm_ref[r:r + 1, :].astype(BF16)


def _diff_attn_kernel(lam_init, with_ctx, bound_ref, q_ref, k_ref, v_ref, lam_ref, sg_ref, lm_ref, o_ref):
    def run(kk, vv, stabilise):
        q = q_ref[...]
        v1 = vv + _mask_row(lm_ref, 6)
        o1 = _pv(q, kk, _mask_row(lm_ref, 0), v1, stabilise)
        o2 = _pv(q, kk, _mask_row(lm_ref, 1), v1, stabilise)
        lam = _lambda_value(lam_ref, lam_init)
        o = o1 * (1.0 / o1[:, DIFF_V_DIM:DIFF_V_DIM + 1]) - o2 * (lam / o2[:, DIFF_V_DIM:DIFF_V_DIM + 1])
        o = jnp.where(lax.broadcasted_iota(I32, o.shape, 1) < DIFF_V_DIM, o, 0.0)
        ms = jnp.sum(o * o, axis=-1, keepdims=True) * (1.0 / DIFF_V_DIM)
        o_ref[...] = (o * lax.rsqrt(ms + EPS) * sg_ref[...] * (1.0 - lam_init)).astype(BF16)

    _attn_branches(with_ctx, bound_ref, run, k_ref, v_ref)


def _diff_attention(bound, dq, dk, dv, lam_pad, subln, lam_init, nt, with_ctx):
    nb = nt // SEG_ROWS
    nq, q_map, o_map = _attn_maps(with_ctx)
    kv_map = lambda b, h, t: (b, h)
    const = lambda b, h, t: (0, 0)
    return pl.pallas_call(
        functools.partial(_diff_attn_kernel, lam_init, with_ctx),
        grid=(nb, DIFF_HEADS, nq),
        in_specs=[pl.BlockSpec(memory_space=pltpu.SMEM),
                  pl.BlockSpec((TILE, LANE), q_map), pl.BlockSpec((SEG_ROWS, LANE), kv_map),
                  pl.BlockSpec((SEG_ROWS, LANE), kv_map), pl.BlockSpec((8, LANE), const),
                  pl.BlockSpec((1, LANE), const), pl.BlockSpec((8, LANE), const)],
        out_specs=pl.BlockSpec((TILE, LANE), o_map),
        out_shape=jax.ShapeDtypeStruct((nb * nq * TILE, 512), BF16),
        compiler_params=_cparams(("arbitrary", "arbitrary", "arbitrary")),
        name="diff_attn",
    )(bound, dq, dk, dv, lam_pad, subln, _lane_masks())


def _gqa_attn_kernel(with_ctx, bound_ref, q_ref, k_ref, v_ref, lm_ref, o_ref):
    def run(kk, vv, stabilise):
        q = q_ref[...]
        va = vv * _mask_row(lm_ref, 2) + _mask_row(lm_ref, 4)
        vb = vv * _mask_row(lm_ref, 3) + _mask_row(lm_ref, 5)
        oa = _pv(q, kk, _mask_row(lm_ref, 0), va, stabilise)
        ob = _pv(q, kk, _mask_row(lm_ref, 1), vb, stabilise)
        lane = lax.broadcasted_iota(I32, oa.shape, 1)
        o_ref[...] = jnp.where(lane < GQA_HEAD_DIM, oa * (1.0 / oa[:, GQA_HEAD_DIM:GQA_HEAD_DIM + 1]),
                               ob * (1.0 / ob[:, 0:1])).astype(BF16)

    _attn_branches(with_ctx, bound_ref, run, k_ref, v_ref)


def _gqa_attention(bound, gq, gk, gv, nt, with_ctx):
    nb = nt // SEG_ROWS
    nq, q_map, o_map = _attn_maps(with_ctx)
    kv_map = lambda b, p, t: (b, 0)
    return pl.pallas_call(
        functools.partial(_gqa_attn_kernel, with_ctx),
        grid=(nb, 3, nq),
        in_specs=[pl.BlockSpec(memory_space=pltpu.SMEM),
                  pl.BlockSpec((TILE, LANE), q_map), pl.BlockSpec((SEG_ROWS, LANE), kv_map),
                  pl.BlockSpec((SEG_ROWS, LANE), kv_map), pl.BlockSpec((8, LANE), lambda b, p, t: (0, 0))],
        out_specs=pl.BlockSpec((TILE, LANE), o_map),
        out_shape=jax.ShapeDtypeStruct((nb * nq * TILE, 384), BF16),
        compiler_params=_cparams(("arbitrary", "arbitrary", "arbitrary")),
        name="gqa_attn",
    )(bound, gq, gk, gv, _lane_masks())


def _mlstm_kernel(nbk, mqk_ref, mv_ref, mg_ref, mgt_ref, cw_ref, cb_ref, h_ref, qk_s, bcf_s, bcb_s, brf_s, brb_s):
    L = MLSTM_CHUNK
    h_ref[...] = jnp.zeros_like(h_ref)
    w0, w1, w2, cb = cw_ref[0:1, :], cw_ref[1:2, :], cw_ref[2:3, :], cb_ref[...]
    rid = lax.broadcasted_iota(I32, (TILE, 512), 0)
    kscale = jnp.where(lax.broadcasted_iota(I32, (1, 512), 1) < 256, 1.0, MLSTM_HEAD_DIM ** -0.5)
    zrow = jnp.zeros((1, 512), F32)
    for bc in range(nbk * SEG_TILES):
        c = bc % SEG_TILES
        r0 = bc * TILE
        xc = mqk_ref[r0:r0 + TILE, :]
        prev = zrow if c in (0, 1) else mqk_ref[r0 - 1:r0, :]
        nxt = zrow if c in (0, SEG_TILES - 1) else mqk_ref[r0 + TILE:r0 + TILE + 1, :]
        up = jnp.where(rid == 0, prev, pltpu.roll(xc, 1, 0))
        dn = jnp.where(rid == TILE - 1, nxt, pltpu.roll(xc, TILE - 1, 0))
        y = w0 * up + w1 * xc + w2 * dn + cb
        qk_s[r0:r0 + TILE, :] = (_silu(y) * kscale).astype(BF16)

    ti = lax.broadcasted_iota(I32, (L, L), 0)
    si = lax.broadcasted_iota(I32, (L, L), 1)
    tri_le = (si <= ti)
    m_le = tri_le.astype(BF16)
    m_ge = (si >= ti).astype(BF16)
    row2 = lax.broadcasted_iota(I32, (2 * L, L), 0)
    trow = row2 % L
    scol = lax.broadcasted_iota(I32, (2 * L, L), 1)
    top2 = row2 < L
    mask_f = scol <= trow
    mask_b = scol >= trow
    lane_lo = lax.broadcasted_iota(I32, (L, LANE), 1) < L
    lane_lo256 = lax.broadcasted_iota(I32, (L, 2 * LANE), 1) % LANE < L
    rr = lax.broadcasted_iota(I32, (LANE, LANE), 0)
    cc = lax.broadcasted_iota(I32, (LANE, LANE), 1)
    blockdiag = (rr < L) == (cc < L)
    rows_lo = lax.broadcasted_iota(I32, (LANE, 1), 0) < L
    top_col = lax.broadcasted_iota(I32, (2 * L, 1), 0) < L

    def chain(bb, c, is_fwd, p, gcol, bcol_all, grow, brow_all, state):
        ct, nm, m0, m1 = state
        h0, h1 = 2 * p, 2 * p + 1
        gi, gf = (0, 4) if is_fwd else (8, 12)
        r0 = pl.multiple_of(bb * SEG_ROWS + c * L, L)
        qb = qk_s[pl.ds(r0, L), p * LANE:(p + 1) * LANE]
        kb = qk_s[pl.ds(r0, L), 256 + p * LANE:256 + (p + 1) * LANE]
        v128 = mv_ref[pl.ds(r0, L), p * LANE:(p + 1) * LANE]

        def stack_cols(arr, j0, j1):
            return jnp.concatenate([arr[:, j0:j0 + 1], arr[:, j1:j1 + 1]], axis=0)

        def stack_rows(arr, j0, j1):
            return jnp.where(top2, arr[j0:j0 + 1, :], arr[j1:j1 + 1, :])

        bcol = stack_cols(bcol_all, gf + h0, gf + h1)
        licol = stack_cols(gcol, gi + h0, gi + h1)
        crow = stack_rows(grow, gi + h0, gi + h1) - stack_rows(brow_all, gf + h0, gf + h1)
        cm = jnp.where(mask_f if is_fwd else mask_b, crow, -jnp.inf)
        mcol = jnp.where(top_col, m0, m1)
        u = jnp.maximum(mcol, jnp.max(cm, axis=-1, keepdims=True))
        zq = jnp.zeros_like(qb)
        qstack = jnp.concatenate([jnp.where(lane_lo, qb, zq), jnp.where(lane_lo, zq, qb)], axis=0)
        w = (jnp.exp(cm - u) * _dot_nt(qstack, kb)).astype(BF16)
        wv = _dot(w, jnp.concatenate([v128, jnp.ones_like(v128)], axis=1))
        wv = jnp.where(lane_lo256, wv[:L], wv[L:])
        qcn = _dot(qb, jnp.concatenate([ct, nm], axis=1).astype(BF16))
        a = jnp.exp(mcol - u)
        emt = jnp.exp(-(bcol + u))
        a128 = jnp.where(lane_lo, a[:L], a[L:])
        emt128 = jnp.where(lane_lo, emt[:L], emt[L:])
        num = wv[:, :LANE] + a128 * qcn[:, :LANE]
        den = wv[:, LANE:] + a128 * qcn[:, LANE:]
        h_ref[pl.ds(r0, L), p * LANE:(p + 1) * LANE] += num / jnp.maximum(jnp.abs(den), emt128)
        e0 = (L - 1) if is_fwd else 0
        bend0 = bcol[e0:e0 + 1, :]
        bend1 = bcol[L + e0:L + e0 + 1, :]
        bend = jnp.where(top_col, bend0, bend1)
        g = bend - bcol + licol
        m0n = jnp.maximum(bend0 + m0, jnp.max(g[:L], axis=0, keepdims=True))
        m1n = jnp.maximum(bend1 + m1, jnp.max(g[L:], axis=0, keepdims=True))
        ws = jnp.exp(g - jnp.where(top_col, m0n, m1n))
        ae0 = jnp.exp(bend0 + m0 - m0n)
        ae1 = jnp.exp(bend1 + m1 - m1n)
        ws128 = jnp.where(lane_lo, ws[:L], ws[L:])
        vw = jnp.concatenate([v128.astype(F32) * ws128, ws128], axis=1).astype(BF16)
        upd = _dot_tn(kb, vw)
        ae = jnp.where(rows_lo, ae0, ae1)
        ct_new = ae * ct + jnp.where(blockdiag, upd[:, :LANE], 0.0)
        nm_new = ae * nm + jnp.where(blockdiag, upd[:, LANE:], 0.0)
        return ct_new, nm_new, m0n, m1n

    def gate_sums(j, _):
        for bb in range(nbk):
            r0 = pl.multiple_of(bb * SEG_ROWS + j * L, L)
            lf_col = _log_sigmoid(mg_ref[pl.ds(r0, L), :])
            lf_row = _log_sigmoid(mgt_ref[bb, j])
            bcf_s[pl.ds(r0, L), :] = _exact_by_dot_f32(m_le, lf_col)
            bcb_s[pl.ds(r0, L), :] = _exact_by_dot_f32(m_ge, lf_col)
            brf_s[bb * N_CHUNKS + j] = _dot_f32_by_exact(lf_row, m_ge)
            brb_s[bb * N_CHUNKS + j] = _dot_f32_by_exact(lf_row, m_le)
        return 0

    lax.fori_loop(0, N_CHUNKS, gate_sums, 0, unroll=2)

    def body(i, carry):
        cf = i
        cbk = jnp.where(i < CTX_CHUNKS, CTX_CHUNKS - 1 - i, N_CHUNKS + CTX_CHUNKS - 1 - i)
        new = []
        for bb in range(nbk):
            for d, c in enumerate((cf, cbk)):
                is_fwd = d == 0
                r0 = pl.multiple_of(bb * SEG_ROWS + c * L, L)
                gcol = mg_ref[pl.ds(r0, L), :]
                grow = mgt_ref[bb, c]
                bcol_all = (bcf_s if is_fwd else bcb_s)[pl.ds(r0, L), :]
                brow_all = (brf_s if is_fwd else brb_s)[bb * N_CHUNKS + c]
                for p in range(2):
                    st = carry[(bb * 2 + d) * 2 + p]
                    new.append(chain(bb, c, is_fwd, p, gcol, bcol_all, grow, brow_all, st))
        return tuple(new)

    z = (jnp.zeros((LANE, LANE), F32), jnp.zeros((LANE, LANE), F32), jnp.zeros((1, 1), F32), jnp.zeros((1, 1), F32))
    lax.fori_loop(0, N_CHUNKS, body, (z,) * (4 * nbk), unroll=2)


def _mlstm(mqk, mv, mg, mgt, conv_w, conv_b, nt):
    nb = nt // SEG_ROWS
    nbk = 1
    blk = lambda w: pl.BlockSpec((nbk * SEG_ROWS, w), lambda g: (g, 0))
    return pl.pallas_call(
        functools.partial(_mlstm_kernel, nbk),
        grid=(nb // nbk,),
        in_specs=[blk(512), blk(256), blk(LANE),
                  pl.BlockSpec((nbk, N_CHUNKS, 16, MLSTM_CHUNK), lambda g: (g, 0, 0, 0)),
                  pl.BlockSpec((8, 512), lambda g: (0, 0)), pl.BlockSpec((1, 512), lambda g: (0, 0))],
        out_specs=blk(256),
        out_shape=jax.ShapeDtypeStruct((nt, 256), F32),
        scratch_shapes=[pltpu.VMEM((nbk * SEG_ROWS, 512), BF16), pltpu.VMEM((nbk * SEG_ROWS, LANE), F32),
                        pltpu.VMEM((nbk * SEG_ROWS, LANE), F32), pltpu.VMEM((nbk * N_CHUNKS, 16, MLSTM_CHUNK), F32),
                        pltpu.VMEM((nbk * N_CHUNKS, 16, MLSTM_CHUNK), F32)],
        compiler_params=_cparams(("arbitrary",)),
        name="mlstm",
    )(mqk, mv, mg, mgt, conv_w, conv_b)


def _outproj_kernel(do_ref, go_ref, hs_ref, mo_ref, x_ref, mod_ref, n2_ref, hn_ref, wo_ref, wr_hi_ref, wr_lo_ref,
                    rb_ref, hm_ref, xm_ref, f_ref, ri_ref, rg_ref, cnt_ref, carry):
    i = pl.program_id(0)

    @pl.when(i == 0)
    def _():
        carry[...] = jnp.zeros_like(carry)

    hsum = hs_ref[...]
    ssq = _dot_f32_by_exact(hsum * hsum, hm_ref[...])
    ml = hsum * lax.rsqrt(ssq * (1.0 / MLSTM_HEAD_DIM) + EPS) * hn_ref[...] * _sigmoid(mo_ref[...])
    acc = _dot(do_ref[...], wo_ref[0:512, :])
    acc += _dot(go_ref[...], wo_ref[512:896, :])
    acc += _dot(ml.astype(BF16), wo_ref[896:MIX_ROWS, :])
    g1 = mod_ref[0, :, 2 * D_MODEL:3 * D_MODEL]
    sh2 = mod_ref[0, :, 3 * D_MODEL:4 * D_MODEL]
    sc2 = mod_ref[0, :, 4 * D_MODEL:5 * D_MODEL]
    x = x_ref[...] + g1 * acc
    xm_ref[...] = x
    xn = x * lax.rsqrt(jnp.mean(x * x, axis=-1, keepdims=True) + EPS) * n2_ref[...]
    f = xn * (1.0 + sc2) + sh2
    _store_row_tiles(f_ref, (), TILE, f)
    f1, f2, _ = _split3(f)
    logits = _dot(f1, wr_hi_ref[...]) + _dot(f2, wr_hi_ref[...]) + _dot(f1, wr_lo_ref[...]) + rb_ref[...]
    lane = lax.broadcasted_iota(I32, logits.shape, 1)
    neg = jnp.float32(-jnp.inf)
    big = jnp.int32(1 << 20)
    is_g = lane < MOE_GROUPS
    lg = jnp.where(is_g, logits, neg)
    gmax = jnp.max(lg, axis=-1, keepdims=True)
    g_top = 1.0 / jnp.sum(jnp.exp(lg - gmax), axis=-1, keepdims=True)
    g_idx = jnp.min(jnp.where(lg == gmax, lane, big), axis=-1, keepdims=True)
    in_grp = (lane >= MOE_GROUPS) & (lane < MOE_GROUPS + MOE_EXPERTS) & ((lane - MOE_GROUPS) // MOE_EPG == g_idx)
    le = jnp.where(in_grp, logits, neg)
    v1 = jnp.max(le, axis=-1, keepdims=True)
    l1 = jnp.min(jnp.where(le == v1, lane, big), axis=-1, keepdims=True)
    le2 = jnp.where(lane == l1, neg, le)
    v2 = jnp.max(le2, axis=-1, keepdims=True)
    l2 = jnp.min(jnp.where(le2 == v2, lane, big), axis=-1, keepdims=True)
    ex = jnp.exp(v2 - v1)
    gate1 = g_top / (1.0 + ex)
    gate2 = gate1 * ex
    oh1 = (lane == l1)
    oh2 = (lane == l2)
    both = oh1.astype(BF16) + oh2.astype(BF16)
    ri_ = lax.broadcasted_iota(I32, (TILE, TILE), 0)
    ci_ = lax.broadcasted_iota(I32, (TILE, TILE), 1)
    before = _dot((ci_ < ri_).astype(BF16), both) + carry[0:1, :]
    rank1 = jnp.sum(jnp.where(oh1, before, 0.0), axis=-1, keepdims=True)
    rank2 = jnp.sum(jnp.where(oh2, before, 0.0), axis=-1, keepdims=True)
    carry[0:1, :] = carry[0:1, :] + jnp.sum(both.astype(F32), axis=0, keepdims=True)
    cnt_ref[...] = jnp.broadcast_to(carry[0:1, :], cnt_ref.shape)
    ri = jnp.where(lane == 0, l1 - MOE_GROUPS, jnp.where(lane == 1, l2 - MOE_GROUPS,
         jnp.where(lane == 2, rank1.astype(I32), jnp.where(lane == 3, rank2.astype(I32), 0))))
    ri_ref[...] = ri
    rg_ref[...] = jnp.where(lane == 0, gate1, jnp.where(lane == 1, gate2, 0.0))


def _outproj(do, go, hs, mo, x, mod_l, n2g, hn, w_out_l, wr_hi, wr_lo, rb, head_mat, nt, with_ctx):
    nb = nt // SEG_ROWS
    nq, off = (SEG_TILES, 0) if with_ctx else (SEG_TILES - 1, 1)
    n_steps = nb * nq
    n_out = n_steps * TILE
    in_map = lambda i: (_seg_tile(i, nq, off), 0)
    out_map = lambda i: (i, 0)

    def full(shape):
        return pl.BlockSpec(shape, lambda i: (0,) * len(shape))

    in_specs = [pl.BlockSpec((TILE, 512), out_map), pl.BlockSpec((TILE, 384), out_map), pl.BlockSpec((TILE, 256), in_map),
                pl.BlockSpec((TILE, 256), in_map), pl.BlockSpec((TILE, D_MODEL), in_map),
                pl.BlockSpec((1, 1, 6 * D_MODEL), lambda i: (_mod_row(i, nq, off), 0, 0)),
                full((1, D_MODEL)), full((1, 256)), full((MIX_ROWS, D_MODEL)), full((D_MODEL, LANE)), full((D_MODEL, LANE)),
                full((1, LANE)), full((256, 256))]
    out_specs = [pl.BlockSpec((TILE, D_MODEL), out_map), pl.BlockSpec((TILE * ROW_TILES, LANE), out_map),
                 pl.BlockSpec((TILE, LANE), out_map), pl.BlockSpec((TILE, LANE), out_map), pl.BlockSpec((8, LANE), lambda i: (0, 0))]
    out_shape = [jax.ShapeDtypeStruct((n_out, D_MODEL), F32), jax.ShapeDtypeStruct((n_out * ROW_TILES, LANE), F32),
                 jax.ShapeDtypeStruct((n_out, LANE), I32), jax.ShapeDtypeStruct((n_out, LANE), F32),
                 jax.ShapeDtypeStruct((8, LANE), F32)]
    return pl.pallas_call(
        _outproj_kernel,
        grid=(n_steps,),
        in_specs=in_specs,
        out_specs=out_specs,
        out_shape=out_shape,
        scratch_shapes=[pltpu.VMEM((8, LANE), F32)],
        compiler_params=_cparams(("arbitrary",)),
        name="outproj_route",
    )(do, go, hs, mo, x, mod_l, n2g, hn, w_out_l, wr_hi, wr_lo, rb, head_mat)


DMA_UNROLL = 8


def _for_rows(n, fn):
    groups = lax.shift_right_logical(n, int(math.log2(DMA_UNROLL)))

    def group(g, _):
        for u in range(DMA_UNROLL):
            fn(g * DMA_UNROLL + u, u % 2)
        return 0

    def single(r, _):
        fn(r, 0)
        return 0

    lax.fori_loop(0, groups, group, 0)
    lax.fori_loop(groups * DMA_UNROLL, n, single, 0)


def _expert_kernel(be_ref, nv_ref, src_ref, srcn_ref, dst_ref, f_hbm, w1_ref, w3_ref, w2_ref, y_hbm,
                   xbuf, ybuf, w1s, w3s, w2s, gsem, ssem):
    i = pl.program_id(0)
    nv = nv_ref[i]
    nv_next = nv_ref[i + 1]
    slot = i % 2

    def tile(r):
        return pl.ds(pl.multiple_of(r * ROW_TILES, ROW_TILES), ROW_TILES)

    def gather_copy(idx_ref, s, r):
        return pltpu.make_async_copy(f_hbm.at[tile(idx_ref[0, 0, r])], xbuf.at[s, tile(r)], gsem.at[s])

    def scatter_copy(r):
        return pltpu.make_async_copy(ybuf.at[tile(r)], y_hbm.at[tile(dst_ref[0, 0, r])], ssem)

    def rows(n):
        return pl.ds(0, pl.multiple_of(n * ROW_TILES, ROW_TILES))

    def gather_wait(n):
        pltpu.make_async_copy(f_hbm.at[rows(n)], xbuf.at[slot, rows(n)], gsem.at[slot]).wait()

    def scatter_wait(n):
        pltpu.make_async_copy(ybuf.at[rows(n)], y_hbm.at[rows(n)], ssem).wait()

    @pl.when(i == 0)
    def _():
        xbuf[...] = jnp.zeros_like(xbuf)
        _for_rows(nv, lambda r, pr: gather_copy(src_ref, 0, r).start(priority=pr))

    @pl.when(nv_next > 0)
    def _():
        _for_rows(nv_next, lambda r, pr: gather_copy(srcn_ref, 1 - slot, r).start(priority=pr))

    @pl.when((nv > 0) & ((i == 0) | (be_ref[i] != be_ref[jnp.maximum(i - 1, 0)])))
    def _():
        w1s[...] = w1_ref[0, 0].astype(BF16)
        w3s[...] = w3_ref[0, 0].astype(BF16)
        w2s[...] = w2_ref[0, 0].astype(BF16)

    @pl.when(nv > 0)
    def _():
        gather_wait(nv)
        xb = _load_row_tiles(xbuf, (slot,), MOE_TILE).astype(BF16)
        hh = _silu(_dot(xb, w1s[...])) * _dot(xb, w3s[...])
        y = _dot(hh.astype(BF16), w2s[...])

        @pl.when(i > 0)
        def _():
            scatter_wait(nv_ref[jnp.maximum(i - 1, 0)])

        _store_row_tiles(ybuf, (), MOE_TILE, y)
        _for_rows(nv, lambda r, pr: scatter_copy(r).start(priority=pr))

        @pl.when(nv_next == 0)
        def _():
            scatter_wait(nv)


def _experts(f, row_src, row_dst, blk_expert, blk_valid, w1, w3, w2, layer, n_tok):
    n_blk = row_src.shape[0]
    idx_spec = lambda fn: pl.BlockSpec((1, 1, MOE_TILE), fn, memory_space=pltpu.SMEM)
    w_spec = lambda shape: pl.BlockSpec((1, 1) + shape, lambda i, be, nv: (layer, be[i], 0, 0))
    grid_spec = pltpu.PrefetchScalarGridSpec(
        num_scalar_prefetch=2,
        grid=(n_blk,),
        in_specs=[idx_spec(lambda i, be, nv: (i, 0, 0)),
                  idx_spec(lambda i, be, nv: (jnp.minimum(i + 1, n_blk - 1), 0, 0)),
                  idx_spec(lambda i, be, nv: (i, 0, 0)),
                  pl.BlockSpec(memory_space=pl.ANY),
                  w_spec((D_MODEL, MOE_HIDDEN)), w_spec((D_MODEL, MOE_HIDDEN)), w_spec((MOE_HIDDEN, D_MODEL))],
        out_specs=pl.BlockSpec(memory_space=pl.ANY),
        scratch_shapes=[pltpu.VMEM((2, MOE_TILE * ROW_TILES, LANE), F32), pltpu.VMEM((MOE_TILE * ROW_TILES, LANE), F32),
                        pltpu.VMEM((D_MODEL, MOE_HIDDEN), BF16), pltpu.VMEM((D_MODEL, MOE_HIDDEN), BF16),
                        pltpu.VMEM((MOE_HIDDEN, D_MODEL), BF16),
                        pltpu.SemaphoreType.DMA((2,)), pltpu.SemaphoreType.DMA(())],
    )
    return pl.pallas_call(
        _expert_kernel,
        grid_spec=grid_spec,
        out_shape=jax.ShapeDtypeStruct((2 * n_tok * ROW_TILES, LANE), F32),
        compiler_params=_cparams(("arbitrary",)),
        name="experts",
    )(blk_expert, blk_valid, row_src, row_src, row_dst, f, w1, w3, w2).reshape(2, n_tok * ROW_TILES, LANE)


def _dest_kernel(ri_ref, ps_ref, o_ref):
    ri = ri_ref[...]
    lane = lax.broadcasted_iota(I32, ri.shape, 1)
    ps = ps_ref[...]

    def dest(k):
        start = jnp.sum(jnp.where(lane == ri[:, k:k + 1], ps, 0.0), axis=-1, keepdims=True)
        return start.astype(I32) + ri[:, 2 + k:3 + k]

    o_ref[...] = jnp.where(lane == 0, dest(0), jnp.where(lane == 1, dest(1), 0))


def _pair_dest(ri, pad_start, n_tok):
    ps = jnp.pad(pad_start.astype(F32), (0, LANE - MOE_EXPERTS)).reshape(1, LANE)
    rows = next(r for r in (2048, 1024, 512, TILE) if n_tok % r == 0)
    return pl.pallas_call(
        _dest_kernel,
        grid=(n_tok // rows,),
        in_specs=[pl.BlockSpec((rows, LANE), lambda i: (i, 0)), pl.BlockSpec((1, LANE), lambda i: (0, 0))],
        out_specs=pl.BlockSpec((rows, LANE), lambda i: (i, 0)),
        out_shape=jax.ShapeDtypeStruct((n_tok, LANE), I32),
        compiler_params=_cparams(("arbitrary",)),
        name="pair_dest",
    )(ri, ps)


INV_UNROLL = 8


def _inverse_kernel(n_pairs, dest_ref, lo_ref, hi_ref, out_ref):
    def body(g, _):
        for u in range(INV_UNROLL):
            p = g * INV_UNROLL + u
            out_ref[dest_ref[p]] = p
        return 0

    lax.fori_loop(0, n_pairs // INV_UNROLL, body, 0)

    def fill_segment(e, _):
        def fill(r, _):
            out_ref[r] = 0
            return 0
        return lax.fori_loop(lo_ref[e], hi_ref[e], fill, 0)

    lax.fori_loop(0, MOE_EXPERTS + 1, fill_segment, 0)


def _inverse_rows(dest_flat, fill_lo, fill_hi, p_rows):
    n_pairs = dest_flat.shape[0]
    smem = pl.BlockSpec(memory_space=pltpu.SMEM)
    return pl.pallas_call(
        functools.partial(_inverse_kernel, n_pairs),
        in_specs=[smem, smem, smem],
        out_specs=smem,
        out_shape=jax.ShapeDtypeStruct((p_rows,), I32),
        name="inverse_rows",
    )(dest_flat, fill_lo, fill_hi)


def _route_plan(ri, counts, n_tok):
    n_blk = (2 * n_tok) // MOE_TILE + MOE_EXPERTS
    p_rows = n_blk * MOE_TILE
    cnt = counts[0, MOE_GROUPS:MOE_GROUPS + MOE_EXPERTS].astype(I32)
    padded = (cnt + MOE_TILE - 1) // MOE_TILE * MOE_TILE
    pad_end = jnp.cumsum(padded)
    pad_start = pad_end - padded
    dest = _pair_dest(ri, pad_start, n_tok)[:, 0:2]
    fill_lo = jnp.concatenate([pad_start + cnt, pad_end[-1:]]).astype(I32)
    fill_hi = jnp.concatenate([pad_end, jnp.full((1,), p_rows, I32)]).astype(I32)
    pair = _inverse_rows(dest.reshape(-1), fill_lo, fill_hi, p_rows)
    row_src = pair // 2
    row_dst = (pair % 2) * n_tok + row_src
    blk_start = jnp.arange(n_blk + 1, dtype=I32) * MOE_TILE
    blk_expert = jnp.minimum(jnp.sum(pad_end[None, :] <= blk_start[:, None], axis=-1), MOE_EXPERTS - 1).astype(I32)
    in_expert = blk_start - pad_start[blk_expert]
    blk_valid = jnp.where(blk_start < pad_end[-1], jnp.clip(cnt[blk_expert] - in_expert, 0, MOE_TILE), 0).astype(I32)
    return (row_src.reshape(n_blk, 1, MOE_TILE), row_dst.reshape(n_blk, 1, MOE_TILE), blk_expert[:n_blk], blk_valid)


def _final_kernel(x_ref, y_ref, rg_ref, mod_ref, o_ref):
    g2 = mod_ref[0, :, 5 * D_MODEL:6 * D_MODEL]
    rg = rg_ref[...]
    o_ref[...] = x_ref[...] + g2 * (rg[:, 0:1] * _load_row_tiles(y_ref, (0,), TILE) + rg[:, 1:2] * _load_row_tiles(y_ref, (1,), TILE))


def _final(x, y_flat, rg, mod_l, n_tok):
    nq = SEQ // TILE
    tile_map = lambda i: (i, 0)
    return pl.pallas_call(
        _final_kernel,
        grid=(n_tok // TILE,),
        in_specs=[pl.BlockSpec((TILE, D_MODEL), tile_map), pl.BlockSpec((2, TILE * ROW_TILES, LANE), lambda i: (0, i, 0)),
                  pl.BlockSpec((TILE, LANE), tile_map), pl.BlockSpec((1, 1, 6 * D_MODEL), lambda i: (i // nq, 0, 0))],
        out_specs=pl.BlockSpec((TILE, D_MODEL), tile_map),
        out_shape=jax.ShapeDtypeStruct((n_tok, D_MODEL), F32),
        compiler_params=_cparams(("arbitrary",)),
        name="final_residual",
    )(x, y_flat, rg, mod_l)


def _diff_rows(w, heads):
    nl, _, k = w.shape
    w = w.reshape(nl, heads, 2, 2, 2, 12, k)
    w = w.transpose(0, 1, 4, 2, 3, 5, 6)
    w = jnp.pad(w.reshape(nl, heads, 4, 24, k), ((0, 0), (0, 0), (0, 0), (0, 8), (0, 0)))
    return w.reshape(nl, heads * LANE, k)


def _gqa_rows(w, groups):
    nl, _, k = w.shape
    w = w.reshape(nl, 2, groups, 2, 2, 16, k)
    w = w.transpose(0, 2, 4, 1, 3, 5, 6)
    return w.reshape(nl, groups * LANE, k)


def _in_weight(w_in):
    wt = jnp.swapaxes(w_in, 1, 2)
    nl = wt.shape[0]
    dv = jnp.pad(wt[:, 768:1152].reshape(nl, DIFF_HEADS, DIFF_V_DIM, D_MODEL), ((0, 0), (0, 0), (0, LANE - DIFF_V_DIM), (0, 0)))
    parts = [_diff_rows(wt[:, 0:384], DIFF_HEADS), _diff_rows(wt[:, 384:768], DIFF_HEADS),
             dv.reshape(nl, DIFF_HEADS * LANE, D_MODEL), _gqa_rows(wt[:, 1152:1536], 3), _gqa_rows(wt[:, 1536:1664], 1),
             wt[:, 1664:2816], jnp.pad(wt[:, 2816:2832], ((0, 0), (0, LANE - 16), (0, 0)))]
    return jnp.concatenate(parts, axis=1).astype(BF16)


def _lane_gain(g, rows_fn):
    return rows_fn(jnp.concatenate([g, g], axis=-1)[:, :, None], 1)[:, :, 0]


def _out_weight(w_out):
    nl = w_out.shape[0]
    diff = jnp.pad(w_out[:, 0:384].reshape(nl, DIFF_HEADS, DIFF_V_DIM, D_MODEL), ((0, 0), (0, 0), (0, LANE - DIFF_V_DIM), (0, 0)))
    gqa = w_out[:, 384:768].reshape(nl, 2, 3, GQA_HEAD_DIM, D_MODEL).transpose(0, 2, 1, 3, 4)
    return jnp.concatenate([diff.reshape(nl, 512, D_MODEL), gqa.reshape(nl, 384, D_MODEL), w_out[:, 768:]], axis=1).astype(BF16)


def kernel(x, c, ctx, c_ctx, norm1_g, norm2_g, w_mod, b_mod, w_in, w_out, diff_q_norm, diff_k_norm, diff_lambda, diff_subln, gqa_q_norm, gqa_k_norm, mlstm_conv_w, mlstm_conv_b, mlstm_gate_b, mlstm_head_norm, moe_wg, moe_bg, moe_we, moe_be, moe_w1, moe_w3, moe_w2):
    B = x.shape[0]
    nt = B * SEG_ROWS
    n_lat = B * SEQ

    w_in_r = _in_weight(w_in)
    w_out_r = _out_weight(w_out)
    gq_d = _lane_gain(diff_q_norm, _diff_rows) * (LOG2E * DIFF_HEAD_DIM ** -0.5)
    gk_d = _lane_gain(diff_k_norm, _diff_rows)
    gq_g = _lane_gain(gqa_q_norm, _gqa_rows) * (LOG2E * GQA_HEAD_DIM ** -0.5)
    gk_g = _lane_gain(gqa_k_norm, _gqa_rows)
    bound_d = _score_bound(diff_q_norm, diff_k_norm, DIFF_HEAD_DIM).reshape(DEPTH, 1)
    bound_g = _score_bound(gqa_q_norm, gqa_k_norm, GQA_HEAD_DIM).reshape(DEPTH, 1)
    gains = jnp.concatenate([jnp.tile(gq_d, (1, 4)), jnp.tile(gk_d, (1, 4)), jnp.tile(gq_g, (1, 3)), gk_g], axis=1)
    gains = gains.reshape(DEPTH, 1, 1536)
    gate_b = jnp.pad(mlstm_gate_b, ((0, 0), (0, LANE - 16))).reshape(DEPTH, 1, LANE)
    subln = jnp.pad(diff_subln, ((0, 0), (0, LANE - DIFF_V_DIM))).reshape(DEPTH, 1, LANE)
    lam_pad = jnp.pad(diff_lambda, ((0, 0), (0, 4), (0, LANE - DIFF_HEAD_DIM)))
    conv_w = jnp.pad(mlstm_conv_w, ((0, 0), (0, 5), (0, 0)))
    conv_b = mlstm_conv_b.reshape(DEPTH, 1, 512)
    w_r = jnp.pad(jnp.concatenate([moe_wg, moe_we], axis=2), ((0, 0), (0, 0), (0, LANE - 36)))
    wr_hi = w_r.astype(BF16)
    wr_lo = (w_r - wr_hi.astype(F32)).astype(BF16)
    rb = jnp.pad(jnp.concatenate([moe_bg, moe_be], axis=1), ((0, 0), (0, LANE - 36))).reshape(DEPTH, 1, LANE)
    pair_mat = jnp.asarray(_PAIR_MAT, BF16)
    head_mat = jnp.asarray((np.arange(256)[:, None] // 64 == np.arange(256)[None, :] // 64).astype(np.float32), BF16)
    tabs = _rope_tables(12, 24) + _rope_tables(16, 32)

    xa = jnp.concatenate([ctx, x], axis=1).reshape(nt, D_MODEL)
    assert B <= 8, "row 8 of the modulation table is reserved for the context conditioning"
    cvec = jnp.concatenate([c, jnp.zeros((8 - B, D_MODEL), F32), c_ctx[None, :], jnp.zeros((7, D_MODEL), F32)], axis=0)
    mod = _modulation(cvec, w_mod, b_mod).reshape(DEPTH, 16, 1, 6 * D_MODEL)

    moe_in = None
    for l in range(DEPTH):
        last = l == DEPTH - 1
        lam_init = 0.8 - 0.6 * math.exp(-0.3 * l)
        xa, (dq, dk, dv, gq, gk, gv, mqk, mv, mo, mg) = _inproj(
            xa, moe_in, mod[l], norm1_g[l].reshape(1, D_MODEL), w_in_r[l], gains[l], gate_b[l], tabs, pair_mat, nt)
        do = _diff_attention(bound_d[l], dq, dk, dv, lam_pad[l], subln[l], lam_init, nt, with_ctx=not last)
        go = _gqa_attention(bound_g[l], gq, gk, gv, nt, with_ctx=not last)
        mgt = mg[:, :16].reshape(B, N_CHUNKS, MLSTM_CHUNK, 16).transpose(0, 1, 3, 2)
        hs = _mlstm(mqk, mv, mg, mgt, conv_w[l], conv_b[l], nt)
        xm, f, ri, rg, counts = _outproj(do, go, hs, mo, xa, mod[l], norm2_g[l].reshape(1, D_MODEL),
                                         mlstm_head_norm[l].reshape(1, 256), w_out_r[l], wr_hi[l], wr_lo[l], rb[l],
                                         head_mat, nt, with_ctx=not last)
        n_tok = n_lat if last else nt
        row_src, row_dst, blk_expert, blk_valid = _route_plan(ri, counts, n_tok)
        y = _experts(f, row_src, row_dst, blk_expert, blk_valid, moe_w1, moe_w3, moe_w2, l, n_tok)
        if last:
            out = _final(xm, y, rg, mod[l][:B], n_tok)
            return out.reshape(B, SEQ, D_MODEL)
        xa = xm
        moe_in = (y, rg, mod[l])
```

```python
import functools
import math

import numpy as np
import jax
import jax.numpy as jnp
from jax import lax
from jax.experimental import pallas as pl
from jax.experimental.pallas import tpu as pltpu

F32 = jnp.float32
BF16 = jnp.bfloat16
I32 = jnp.int32

D_MODEL = 1024
DEPTH = 4
GRID_W = 64
CTX_LEN = 256
SEQ = 2048
ROPE_THETA = 10000.0
EPS = 1e-6

DIFF_HEADS = 4
DIFF_HEAD_DIM = 48
DIFF_V_DIM = 96
GQA_Q_HEADS = 6
GQA_KV_HEADS = 2
GQA_HEAD_DIM = 64
MLSTM_HEADS = 4
MLSTM_HEAD_DIM = 64
MLSTM_CHUNK = 64
IN_WIDTH = 2832
MOE_GROUPS = 4
MOE_EPG = 8
MOE_EXPERTS = 32
MOE_HIDDEN = 512

LANE = 128
TILE = 256
SEG_TILES = (CTX_LEN + SEQ) // TILE
SEG_ROWS = CTX_LEN + SEQ
N_CHUNKS = SEG_ROWS // MLSTM_CHUNK
CTX_CHUNKS = CTX_LEN // MLSTM_CHUNK
MOE_TILE = 256
VMEM_LIMIT = 56 * 1024 * 1024

C_DQ, C_DK, C_DV, C_GQ, C_GK, C_GV, C_MQK, C_MV, C_MO, C_MG, C_END = (
    0, 512, 1024, 1536, 1920, 2048, 2176, 2688, 2944, 3200, 3328)
MIX_ROWS = 512 + 384 + 256


def _cparams(sem):
    return pltpu.CompilerParams(dimension_semantics=sem, vmem_limit_bytes=VMEM_LIMIT)


_LANE2 = np.arange(2 * LANE)
_PAIR_MAT = ((_LANE2[:, None] // LANE == _LANE2[None, :] // LANE)
             & ((_LANE2[:, None] // 32) % 2 == (_LANE2[None, :] // 32) % 2)).astype(np.float32)


def _rope_tables(nf, pad_from):
    t = jnp.arange(SEQ, dtype=I32)
    rows = (t // GRID_W).astype(F32)
    cols = (t % GRID_W).astype(F32)
    freqs = ROPE_THETA ** (-jnp.arange(nf, dtype=F32) / nf)
    lane = np.arange(LANE)
    i = lane % 32
    typ = (lane // 32) // 2
    use_rows = i < nf
    fidx = np.where(use_rows, i, i - nf)
    valid = i < pad_from
    fidx = np.where(valid, fidx, 0)
    ang = jnp.where(jnp.asarray(use_rows)[None, :], rows[:, None], cols[:, None]) * freqs[jnp.asarray(fidx)][None, :]
    cos = jnp.where(jnp.asarray(valid)[None, :], jnp.cos(ang), 1.0)
    sin = jnp.where(jnp.asarray(valid)[None, :], jnp.sin(ang), 0.0)
    sin = sin * jnp.asarray(np.where(typ == 0, -1.0, 1.0), F32)[None, :]
    cos = jnp.concatenate([jnp.ones((CTX_LEN, LANE), F32), cos], axis=0)
    sin = jnp.concatenate([jnp.zeros((CTX_LEN, LANE), F32), sin], axis=0)
    return cos, sin


def _dot(a, b):
    return jnp.dot(a, b, preferred_element_type=F32)


def _dot_nt(a, b):
    return lax.dot_general(a, b, (((1,), (1,)), ((), ())), preferred_element_type=F32)


def _dot_tn(a, b):
    return lax.dot_general(a, b, (((0,), (0,)), ((), ())), preferred_element_type=F32)


def _split3(x):
    x1 = x.astype(BF16)
    r1 = x - x1.astype(F32)
    x2 = r1.astype(BF16)
    x3 = (r1 - x2.astype(F32)).astype(BF16)
    return x1, x2, x3


def _dot_f32_by_exact(x, m):
    x1, x2, x3 = _split3(x)
    return _dot(x1, m) + _dot(x2, m) + _dot(x3, m)


def _dot_f32_by_exact2(x, m):
    x1 = x.astype(BF16)
    x2 = (x - x1.astype(F32)).astype(BF16)
    return _dot(x1, m) + _dot(x2, m)


def _exact_by_dot_f32(m, x):
    x1, x2, x3 = _split3(x)
    return _dot(m, x1) + _dot(m, x2) + _dot(m, x3)


def _sigmoid(x):
    return 1.0 / (1.0 + jnp.exp(-x))


def _silu(x):
    return x * _sigmoid(x)


def _log_sigmoid(x):
    return jnp.minimum(x, 0.0) - jnp.log1p(jnp.exp(-jnp.abs(x)))


ROW_TILES = D_MODEL // LANE


def _store_row_tiles(ref, lead, rows, x):
    for c in range(ROW_TILES):
        ref[lead + (pl.ds(c, rows, stride=ROW_TILES), slice(None))] = x[:, c * LANE:(c + 1) * LANE]


def _load_row_tiles(ref, lead, rows):
    return jnp.concatenate([ref[lead + (pl.ds(c, rows, stride=ROW_TILES), slice(None))] for c in range(ROW_TILES)], axis=1)


def _seg_tile(i, nq, off):
    return (i // nq) * SEG_TILES + off + i % nq


def _mod_row(i, nq, off):
    return jnp.where((off + i % nq) == 0, 8, i // nq)


MOD_BN = 1536


def _mod_kernel(c_ref, w_ref, b_ref, o_ref):
    a = _silu(c_ref[...]).astype(BF16)
    o_ref[0] = _dot(a, w_ref[0].astype(BF16)) + b_ref[0]


def _modulation(cvec, w_mod, b_mod):
    nb = 6 * D_MODEL // MOD_BN
    return pl.pallas_call(
        _mod_kernel,
        grid=(DEPTH, nb),
        in_specs=[pl.BlockSpec((16, D_MODEL), lambda l, n: (0, 0)),
                  pl.BlockSpec((1, D_MODEL, MOD_BN), lambda l, n: (l, 0, n)),
                  pl.BlockSpec((1, 1, MOD_BN), lambda l, n: (l, 0, n))],
        out_specs=pl.BlockSpec((1, 16, MOD_BN), lambda l, n: (l, 0, n)),
        out_shape=jax.ShapeDtypeStruct((DEPTH, 16, 6 * D_MODEL), F32),
        compiler_params=_cparams(("arbitrary", "arbitrary")),
        name="modulation",
    )(cvec, w_mod, b_mod.reshape(DEPTH, 1, 6 * D_MODEL))


def _inproj_kernel(has_moe, *refs):
    if has_moe:
        (x_ref, y_ref, rg_ref, modp_ref, mod_ref, n1_ref, w_ref, gains_ref, gb_ref, cd_ref, sd_ref, cg_ref, sg_ref,
         pm_ref, xo_ref, dq_ref, dk_ref, dv_ref, gq_ref, gk_ref, gv_ref, mqk_ref, mv_ref, mo_ref, mg_ref) = refs
    else:
        (x_ref, mod_ref, n1_ref, w_ref, gains_ref, gb_ref, cd_ref, sd_ref, cg_ref, sg_ref,
         pm_ref, dq_ref, dk_ref, dv_ref, gq_ref, gk_ref, gv_ref, mqk_ref, mv_ref, mo_ref, mg_ref) = refs
    x = x_ref[...]
    if has_moe:
        g2 = modp_ref[0, :, 5 * D_MODEL:6 * D_MODEL]
        rg = rg_ref[...]
        x = x + g2 * (rg[:, 4:5] * _load_row_tiles(y_ref, (0,), TILE) + rg[:, 5:6] * _load_row_tiles(y_ref, (1,), TILE))
        xo_ref[...] = x
    sh = mod_ref[0, :, 0:D_MODEL]
    sc = mod_ref[0, :, D_MODEL:2 * D_MODEL]
    xn = x * lax.rsqrt(jnp.mean(x * x, axis=-1, keepdims=True) + EPS) * n1_ref[...]
    h = (xn * (1.0 + sc) + sh).astype(BF16)
    pm = pm_ref[...]

    def proj(a, b):
        return _dot_nt(h, w_ref[a:b, :])

    def qk_group(col, gain_col, inv_dim, cos, sin, outs):
        y = proj(col, col + 512)
        for half in range(2):
            yh = y[:, half * 256:(half + 1) * 256]
            yn = yh * lax.rsqrt(_dot_f32_by_exact2(yh * yh, pm) * inv_dim + EPS)
            yn = yn * gains_ref[:, gain_col + half * 256:gain_col + (half + 1) * 256]
            for j in range(2):
                yb = yn[:, j * LANE:(j + 1) * LANE]
                ref, off = outs[half * 2 + j]
                ref[:, off:off + LANE] = (yb * cos + pltpu.roll(yb, 64, 1) * sin).astype(BF16)

    cd, sd, cg, sg = cd_ref[...], sd_ref[...], cg_ref[...], sg_ref[...]
    qk_group(C_DQ, 0, 1.0 / DIFF_HEAD_DIM, cd, sd, [(dq_ref, b * LANE) for b in range(4)])
    qk_group(C_DK, 512, 1.0 / DIFF_HEAD_DIM, cd, sd, [(dk_ref, b * LANE) for b in range(4)])
    qk_group(C_GQ, 1024, 1.0 / GQA_HEAD_DIM, cg, sg, [(gq_ref, 0), (gq_ref, LANE), (gq_ref, 2 * LANE), (gk_ref, 0)])
    dv_ref[...] = proj(C_DV, C_GQ).astype(BF16)
    rest = proj(C_GV, C_END)
    gv_ref[...] = rest[:, 0:C_MQK - C_GV].astype(BF16)
    mqk_ref[...] = rest[:, C_MQK - C_GV:C_MV - C_GV]
    mv_ref[...] = rest[:, C_MV - C_GV:C_MO - C_GV].astype(BF16)
    mo_ref[...] = rest[:, C_MO - C_GV:C_MG - C_GV]
    mg_ref[...] = rest[:, C_MG - C_GV:C_END - C_GV] + gb_ref[...]


def _inproj(x, moe_in, mod_l, n1g, w_in_l, gains, gate_b, tabs, pair_mat, nt):
    has_moe = moe_in is not None
    n_tiles = nt // TILE
    tile_map = lambda i: (i, 0)
    mod_spec = pl.BlockSpec((1, 1, 6 * D_MODEL), lambda i: (_mod_row(i, SEG_TILES, 0), 0, 0))
    tab_spec = pl.BlockSpec((TILE, LANE), lambda i: (i % SEG_TILES, 0))

    def full(shape):
        return pl.BlockSpec(shape, lambda i: (0,) * len(shape))

    in_specs = [pl.BlockSpec((TILE, D_MODEL), tile_map)]
    args = [x]
    if has_moe:
        y_flat, rg, mod_prev = moe_in
        in_specs += [pl.BlockSpec((2, TILE * ROW_TILES, LANE), lambda i: (0, i, 0)), pl.BlockSpec((TILE, LANE), tile_map), mod_spec]
        args += [y_flat, rg, mod_prev]
    in_specs += [mod_spec, full((1, D_MODEL)), full((C_END, D_MODEL)), full((1, 1536)), full((1, LANE)),
                 tab_spec, tab_spec, tab_spec, tab_spec, full((2 * LANE, 2 * LANE))]
    args += [mod_l, n1g, w_in_l, gains, gate_b, tabs[0], tabs[1], tabs[2], tabs[3], pair_mat]

    def o(width, dtype):
        return pl.BlockSpec((TILE, width), tile_map), jax.ShapeDtypeStruct((nt, width), dtype)

    outs = []
    if has_moe:
        outs.append(o(D_MODEL, F32))
    outs += [o(512, BF16), o(512, BF16), o(512, BF16), o(384, BF16), o(LANE, BF16), o(LANE, BF16),
             o(512, F32), o(256, BF16), o(256, F32), o(LANE, F32)]
    res = pl.pallas_call(
        functools.partial(_inproj_kernel, has_moe),
        grid=(n_tiles,),
        in_specs=in_specs,
        out_specs=[s for s, _ in outs],
        out_shape=[s for _, s in outs],
        compiler_params=_cparams(("arbitrary",)),
        name="inproj_moe" if has_moe else "inproj",
    )(*args)
    if has_moe:
        return res[0], res[1:]
    return x, res


def _lambda_value(lam_ref, lam_init):
    lam = lam_ref[...]
    s01 = jnp.sum(lam[0:1] * lam[1:2], axis=-1, keepdims=True)
    s23 = jnp.sum(lam[2:3] * lam[3:4], axis=-1, keepdims=True)
    return jnp.exp(s01) - jnp.exp(s23) + lam_init


LOG2E = 1.4426950408889634
SAFE_LOG2_RANGE = 60.0


def _exp_scores(s, stabilise):
    if stabilise:
        s = s - jnp.max(s, axis=-1, keepdims=True)
    return jnp.exp2(s).astype(BF16)


def _pv(q, kk, kmask, vv, stabilise):
    return _dot(_exp_scores(_dot_nt(q, kk * kmask), stabilise), vv)


def _score_bound(q_gain, k_gain, head_dim):
    return (1.02 * LOG2E * math.sqrt(head_dim)) * jnp.max(jnp.abs(q_gain), axis=-1) * jnp.max(jnp.abs(k_gain), axis=-1)


def _attn_branches(with_ctx, bound_ref, run, k_ref, v_ref):
    def on_keys(rows):
        small = bound_ref[0] <= SAFE_LOG2_RANGE

        @pl.when(small)
        def _():
            run(k_ref[0:rows, :], v_ref[0:rows, :], False)

        @pl.when(jnp.logical_not(small))
        def _():
            run(k_ref[0:rows, :], v_ref[0:rows, :], True)

    if not with_ctx:
        on_keys(SEG_ROWS)
        return
    t = pl.program_id(2)

    @pl.when(t == 0)
    def _():
        on_keys(TILE)

    @pl.when(t > 0)
    def _():
        on_keys(SEG_ROWS)


def _attn_maps(with_ctx):
    if with_ctx:
        m = lambda b, h, t: (b * SEG_TILES + t, h)
        return SEG_TILES, m, m
    nq = SEG_TILES - 1
    return nq, (lambda b, h, t: (b * SEG_TILES + 1 + t, h)), (lambda b, h, t: (b * nq + t, h))


def _lane_masks():
    lane = np.arange(LANE)
    even = (lane // 32) % 2 == 0
    rows = [even, ~even, lane < 64, lane >= 64, lane == 64, lane == 0, lane == DIFF_V_DIM, lane < 0]
    return jnp.asarray(np.stack(rows).astype(np.float32))


def _mask_row(lm_ref, r):
    return lm_ref[r:r + 1, :].astype(BF16)


def _diff_attn_kernel(lam_init, with_ctx, bound_ref, q_ref, k_ref, v_ref, lam_ref, sg_ref, lm_ref, o_ref):
    def run(kk, vv, stabilise):
        q = q_ref[...]
        v1 = vv + _mask_row(lm_ref, 6)
        o1 = _pv(q, kk, _mask_row(lm_ref, 0), v1, stabilise)
        o2 = _pv(q, kk, _mask_row(lm_ref, 1), v1, stabilise)
        lam = _lambda_value(lam_ref, lam_init)
        o = o1 * (1.0 / o1[:, DIFF_V_DIM:DIFF_V_DIM + 1]) - o2 * (lam / o2[:, DIFF_V_DIM:DIFF_V_DIM + 1])
        o = jnp.where(lax.broadcasted_iota(I32, o.shape, 1) < DIFF_V_DIM, o, 0.0)
        ms = jnp.sum(o * o, axis=-1, keepdims=True) * (1.0 / DIFF_V_DIM)
        o_ref[...] = (o * lax.rsqrt(ms + EPS) * sg_ref[...] * (1.0 - lam_init)).astype(BF16)

    _attn_branches(with_ctx, bound_ref, run, k_ref, v_ref)


def _diff_attention(bound, dq, dk, dv, lam_pad, subln, lam_init, nt, with_ctx):
    nb = nt // SEG_ROWS
    nq, q_map, o_map = _attn_maps(with_ctx)
    kv_map = lambda b, h, t: (b, h)
    const = lambda b, h, t: (0, 0)
    return pl.pallas_call(
        functools.partial(_diff_attn_kernel, lam_init, with_ctx),
        grid=(nb, DIFF_HEADS, nq),
        in_specs=[pl.BlockSpec(memory_space=pltpu.SMEM),
                  pl.BlockSpec((TILE, LANE), q_map), pl.BlockSpec((SEG_ROWS, LANE), kv_map),
                  pl.BlockSpec((SEG_ROWS, LANE), kv_map), pl.BlockSpec((8, LANE), const),
                  pl.BlockSpec((1, LANE), const), pl.BlockSpec((8, LANE), const)],
        out_specs=pl.BlockSpec((TILE, LANE), o_map),
        out_shape=jax.ShapeDtypeStruct((nb * nq * TILE, 512), BF16),
        compiler_params=_cparams(("arbitrary", "arbitrary", "arbitrary")),
        name="diff_attn",
    )(bound, dq, dk, dv, lam_pad, subln, _lane_masks())


def _gqa_attn_kernel(with_ctx, bound_ref, q_ref, k_ref, v_ref, lm_ref, o_ref):
    def run(kk, vv, stabilise):
        q = q_ref[...]
        va = vv * _mask_row(lm_ref, 2) + _mask_row(lm_ref, 4)
        vb = vv * _mask_row(lm_ref, 3) + _mask_row(lm_ref, 5)
        oa = _pv(q, kk, _mask_row(lm_ref, 0), va, stabilise)
        ob = _pv(q, kk, _mask_row(lm_ref, 1), vb, stabilise)
        lane = lax.broadcasted_iota(I32, oa.shape, 1)
        o_ref[...] = jnp.where(lane < GQA_HEAD_DIM, oa * (1.0 / oa[:, GQA_HEAD_DIM:GQA_HEAD_DIM + 1]),
                               ob * (1.0 / ob[:, 0:1])).astype(BF16)

    _attn_branches(with_ctx, bound_ref, run, k_ref, v_ref)


def _gqa_attention(bound, gq, gk, gv, nt, with_ctx):
    nb = nt // SEG_ROWS
    nq, q_map, o_map = _attn_maps(with_ctx)
    kv_map = lambda b, p, t: (b, 0)
    return pl.pallas_call(
        functools.partial(_gqa_attn_kernel, with_ctx),
        grid=(nb, 3, nq),
        in_specs=[pl.BlockSpec(memory_space=pltpu.SMEM),
                  pl.BlockSpec((TILE, LANE), q_map), pl.BlockSpec((SEG_ROWS, LANE), kv_map),
                  pl.BlockSpec((SEG_ROWS, LANE), kv_map), pl.BlockSpec((8, LANE), lambda b, p, t: (0, 0))],
        out_specs=pl.BlockSpec((TILE, LANE), o_map),
        out_shape=jax.ShapeDtypeStruct((nb * nq * TILE, 384), BF16),
        compiler_params=_cparams(("arbitrary", "arbitrary", "arbitrary")),
        name="gqa_attn",
    )(bound, gq, gk, gv, _lane_masks())


def _mlstm_kernel(nbk, mqk_ref, mv_ref, mg_ref, mgt_ref, cw_ref, cb_ref, h_ref, qk_s, bcf_s, bcb_s, brf_s, brb_s):
    L = MLSTM_CHUNK
    h_ref[...] = jnp.zeros_like(h_ref)
    w0, w1, w2, cb = cw_ref[0:1, :], cw_ref[1:2, :], cw_ref[2:3, :], cb_ref[...]
    rid = lax.broadcasted_iota(I32, (TILE, 512), 0)
    kscale = jnp.where(lax.broadcasted_iota(I32, (1, 512), 1) < 256, 1.0, MLSTM_HEAD_DIM ** -0.5)
    zrow = jnp.zeros((1, 512), F32)
    for bc in range(nbk * SEG_TILES):
        c = bc % SEG_TILES
        r0 = bc * TILE
        xc = mqk_ref[r0:r0 + TILE, :]
        prev = zrow if c in (0, 1) else mqk_ref[r0 - 1:r0, :]
        nxt = zrow if c in (0, SEG_TILES - 1) else mqk_ref[r0 + TILE:r0 + TILE + 1, :]
        up = jnp.where(rid == 0, prev, pltpu.roll(xc, 1, 0))
        dn = jnp.where(rid == TILE - 1, nxt, pltpu.roll(xc, TILE - 1, 0))
        y = w0 * up + w1 * xc + w2 * dn + cb
        qk_s[r0:r0 + TILE, :] = (_silu(y) * kscale).astype(BF16)

    ti = lax.broadcasted_iota(I32, (L, L), 0)
    si = lax.broadcasted_iota(I32, (L, L), 1)
    tri_le = (si <= ti)
    m_le = tri_le.astype(BF16)
    m_ge = (si >= ti).astype(BF16)
    row2 = lax.broadcasted_iota(I32, (2 * L, L), 0)
    trow = row2 % L
    scol = lax.broadcasted_iota(I32, (2 * L, L), 1)
    top2 = row2 < L
    mask_f = scol <= trow
    mask_b = scol >= trow
    lane_lo = lax.broadcasted_iota(I32, (L, LANE), 1) < L
    lane_lo256 = lax.broadcasted_iota(I32, (L, 2 * LANE), 1) % LANE < L
    rr = lax.broadcasted_iota(I32, (LANE, LANE), 0)
    cc = lax.broadcasted_iota(I32, (LANE, LANE), 1)
    blockdiag = (rr < L) == (cc < L)
    rows_lo = lax.broadcasted_iota(I32, (LANE, 1), 0) < L
    top_col = lax.broadcasted_iota(I32, (2 * L, 1), 0) < L

    def chain(bb, c, is_fwd, p, gcol, bcol_all, grow, brow_all, state):
        ct, nm, m0, m1 = state
        h0, h1 = 2 * p, 2 * p + 1
        gi, gf = (0, 4) if is_fwd else (8, 12)
        r0 = pl.multiple_of(bb * SEG_ROWS + c * L, L)
        qb = qk_s[pl.ds(r0, L), p * LANE:(p + 1) * LANE]
        kb = qk_s[pl.ds(r0, L), 256 + p * LANE:256 + (p + 1) * LANE]
        v128 = mv_ref[pl.ds(r0, L), p * LANE:(p + 1) * LANE]

        def stack_cols(arr, j0, j1):
            return jnp.concatenate([arr[:, j0:j0 + 1], arr[:, j1:j1 + 1]], axis=0)

        def stack_rows(arr, j0, j1):
            return jnp.where(top2, arr[j0:j0 + 1, :], arr[j1:j1 + 1, :])

        bcol = stack_cols(bcol_all, gf + h0, gf + h1)
        licol = stack_cols(gcol, gi + h0, gi + h1)
        crow = stack_rows(grow, gi + h0, gi + h1) - stack_rows(brow_all, gf + h0, gf + h1)
        cm = jnp.where(mask_f if is_fwd else mask_b, crow, -jnp.inf)
        mcol = jnp.where(top_col, m0, m1)
        u = jnp.maximum(mcol, jnp.max(cm, axis=-1, keepdims=True))
        zq = jnp.zeros_like(qb)
        qstack = jnp.concatenate([jnp.where(lane_lo, qb, zq), jnp.where(lane_lo, zq, qb)], axis=0)
        w = (jnp.exp(cm - u) * _dot_nt(qstack, kb)).astype(BF16)
        wv = _dot(w, jnp.concatenate([v128, jnp.ones_like(v128)], axis=1))
        wv = jnp.where(lane_lo256, wv[:L], wv[L:])
        qcn = _dot(qb, jnp.concatenate([ct, nm], axis=1).astype(BF16))
        a = jnp.exp(mcol - u)
        emt = jnp.exp(-(bcol + u))
        a128 = jnp.where(lane_lo, a[:L], a[L:])
        emt128 = jnp.where(lane_lo, emt[:L], emt[L:])
        num = wv[:, :LANE] + a128 * qcn[:, :LANE]
        den = wv[:, LANE:] + a128 * qcn[:, LANE:]
        h_ref[pl.ds(r0, L), p * LANE:(p + 1) * LANE] += num / jnp.maximum(jnp.abs(den), emt128)
        e0 = (L - 1) if is_fwd else 0
        bend0 = bcol[e0:e0 + 1, :]
        bend1 = bcol[L + e0:L + e0 + 1, :]
        bend = jnp.where(top_col, bend0, bend1)
        g = bend - bcol + licol
        m0n = jnp.maximum(bend0 + m0, jnp.max(g[:L], axis=0, keepdims=True))
        m1n = jnp.maximum(bend1 + m1, jnp.max(g[L:], axis=0, keepdims=True))
        ws = jnp.exp(g - jnp.where(top_col, m0n, m1n))
        ae0 = jnp.exp(bend0 + m0 - m0n)
        ae1 = jnp.exp(bend1 + m1 - m1n)
        ws128 = jnp.where(lane_lo, ws[:L], ws[L:])
        vw = jnp.concatenate([v128.astype(F32) * ws128, ws128], axis=1).astype(BF16)
        upd = _dot_tn(kb, vw)
        ae = jnp.where(rows_lo, ae0, ae1)
        ct_new = ae * ct + jnp.where(blockdiag, upd[:, :LANE], 0.0)
        nm_new = ae * nm + jnp.where(blockdiag, upd[:, LANE:], 0.0)
        return ct_new, nm_new, m0n, m1n

    def gate_sums(j, _):
        for bb in range(nbk):
            r0 = pl.multiple_of(bb * SEG_ROWS + j * L, L)
            lf_col = _log_sigmoid(mg_ref[pl.ds(r0, L), :])
            lf_row = _log_sigmoid(mgt_ref[bb, j])
            bcf_s[pl.ds(r0, L), :] = _exact_by_dot_f32(m_le, lf_col)
            bcb_s[pl.ds(r0, L), :] = _exact_by_dot_f32(m_ge, lf_col)
            brf_s[bb * N_CHUNKS + j] = _dot_f32_by_exact(lf_row, m_ge)
            brb_s[bb * N_CHUNKS + j] = _dot_f32_by_exact(lf_row, m_le)
        return 0

    lax.fori_loop(0, N_CHUNKS, gate_sums, 0, unroll=2)

    def body(i, carry):
        cf = i
        cbk = jnp.where(i < CTX_CHUNKS, CTX_CHUNKS - 1 - i, N_CHUNKS + CTX_CHUNKS - 1 - i)
        new = []
        for bb in range(nbk):
            for d, c in enumerate((cf, cbk)):
                is_fwd = d == 0
                r0 = pl.multiple_of(bb * SEG_ROWS + c * L, L)
                gcol = mg_ref[pl.ds(r0, L), :]
                grow = mgt_ref[bb, c]
                bcol_all = (bcf_s if is_fwd else bcb_s)[pl.ds(r0, L), :]
                brow_all = (brf_s if is_fwd else brb_s)[bb * N_CHUNKS + c]
                for p in range(2):
                    st = carry[(bb * 2 + d) * 2 + p]
                    new.append(chain(bb, c, is_fwd, p, gcol, bcol_all, grow, brow_all, st))
        return tuple(new)

    z = (jnp.zeros((LANE, LANE), F32), jnp.zeros((LANE, LANE), F32), jnp.zeros((1, 1), F32), jnp.zeros((1, 1), F32))
    lax.fori_loop(0, N_CHUNKS, body, (z,) * (4 * nbk), unroll=9)


def _mlstm(mqk, mv, mg, mgt, conv_w, conv_b, nt):
    nb = nt // SEG_ROWS
    nbk = 1
    blk = lambda w: pl.BlockSpec((nbk * SEG_ROWS, w), lambda g: (g, 0))
    return pl.pallas_call(
        functools.partial(_mlstm_kernel, nbk),
        grid=(nb // nbk,),
        in_specs=[blk(512), blk(256), blk(LANE),
                  pl.BlockSpec((nbk, N_CHUNKS, 16, MLSTM_CHUNK), lambda g: (g, 0, 0, 0)),
                  pl.BlockSpec((8, 512), lambda g: (0, 0)), pl.BlockSpec((1, 512), lambda g: (0, 0))],
        out_specs=blk(256),
        out_shape=jax.ShapeDtypeStruct((nt, 256), F32),
        scratch_shapes=[pltpu.VMEM((nbk * SEG_ROWS, 512), BF16), pltpu.VMEM((nbk * SEG_ROWS, LANE), F32),
                        pltpu.VMEM((nbk * SEG_ROWS, LANE), F32), pltpu.VMEM((nbk * N_CHUNKS, 16, MLSTM_CHUNK), F32),
                        pltpu.VMEM((nbk * N_CHUNKS, 16, MLSTM_CHUNK), F32)],
        compiler_params=_cparams(("arbitrary",)),
        name="mlstm",
    )(mqk, mv, mg, mgt, conv_w, conv_b)


def _outproj_kernel(do_ref, go_ref, hs_ref, mo_ref, x_ref, mod_ref, n2_ref, hn_ref, wo_ref, wr_hi_ref, wr_lo_ref,
                    rb_ref, hm_ref, xm_ref, f_ref, rt_ref, rtt_ref, cnt_ref, carry):
    i = pl.program_id(0)

    @pl.when(i == 0)
    def _():
        carry[...] = jnp.zeros_like(carry)

    hsum = hs_ref[...]
    ssq = _dot_f32_by_exact(hsum * hsum, hm_ref[...])
    ml = hsum * lax.rsqrt(ssq * (1.0 / MLSTM_HEAD_DIM) + EPS) * hn_ref[...] * _sigmoid(mo_ref[...])
    acc = _dot(do_ref[...], wo_ref[0:512, :])
    acc += _dot(go_ref[...], wo_ref[512:896, :])
    acc += _dot(ml.astype(BF16), wo_ref[896:MIX_ROWS, :])
    g1 = mod_ref[0, :, 2 * D_MODEL:3 * D_MODEL]
    sh2 = mod_ref[0, :, 3 * D_MODEL:4 * D_MODEL]
    sc2 = mod_ref[0, :, 4 * D_MODEL:5 * D_MODEL]
    x = x_ref[...] + g1 * acc
    xm_ref[...] = x
    xn = x * lax.rsqrt(jnp.mean(x * x, axis=-1, keepdims=True) + EPS) * n2_ref[...]
    f = xn * (1.0 + sc2) + sh2
    _store_row_tiles(f_ref, (), TILE, f)
    f1, f2, _ = _split3(f)
    lt = _dot_nt(wr_hi_ref[...], f1) + _dot_nt(wr_hi_ref[...], f2) + _dot_nt(wr_lo_ref[...], f1) + rb_ref[...]
    row = lax.broadcasted_iota(I32, lt.shape, 0)
    neg = jnp.float32(-jnp.inf)
    big = jnp.int32(1 << 20)
    lg = jnp.where(row < MOE_GROUPS, lt, neg)
    gmax = jnp.max(lg, axis=0, keepdims=True)
    g_top = 1.0 / jnp.sum(jnp.exp(lg - gmax), axis=0, keepdims=True)
    g_idx = jnp.min(jnp.where(lg == gmax, row, big), axis=0, keepdims=True)
    in_grp = (row >= MOE_GROUPS) & (row < MOE_GROUPS + MOE_EXPERTS) & ((row - MOE_GROUPS) // MOE_EPG == g_idx)
    le = jnp.where(in_grp, lt, neg)
    v1 = jnp.max(le, axis=0, keepdims=True)
    l1 = jnp.min(jnp.where(le == v1, row, big), axis=0, keepdims=True)
    le2 = jnp.where(row == l1, neg, le)
    v2 = jnp.max(le2, axis=0, keepdims=True)
    l2 = jnp.min(jnp.where(le2 == v2, row, big), axis=0, keepdims=True)
    ex = jnp.exp(v2 - v1)
    gate1 = g_top / (1.0 + ex)
    gate2 = gate1 * ex
    oh1 = (row == l1)
    oh2 = (row == l2)
    both = oh1.astype(BF16) + oh2.astype(BF16)
    ri_ = lax.broadcasted_iota(I32, (TILE, TILE), 0)
    ci_ = lax.broadcasted_iota(I32, (TILE, TILE), 1)
    before = _dot(both, (ri_ < ci_).astype(BF16)) + carry[:, 0:1]
    rank1 = jnp.sum(jnp.where(oh1, before, 0.0), axis=0, keepdims=True)
    rank2 = jnp.sum(jnp.where(oh2, before, 0.0), axis=0, keepdims=True)
    carry[...] = carry[...] + jnp.sum(both.astype(F32), axis=1, keepdims=True)
    cnt_ref[...] = carry[...]
    packed = jnp.where(row == 0, (l1 - MOE_GROUPS).astype(F32), jnp.where(row == 1, (l2 - MOE_GROUPS).astype(F32),
             jnp.where(row == 2, rank1, jnp.where(row == 3, rank2, jnp.where(row == 4, gate1, jnp.where(row == 5, gate2, 0.0))))))
    rt_ref[...] = packed.T
    rtt_ref[...] = packed[0:8, :]


def _outproj(do, go, hs, mo, x, mod_l, n2g, hn, w_out_l, wr_hi, wr_lo, rb, head_mat, nt, with_ctx):
    nb = nt // SEG_ROWS
    nq, off = (SEG_TILES, 0) if with_ctx else (SEG_TILES - 1, 1)
    n_steps = nb * nq
    n_out = n_steps * TILE
    in_map = lambda i: (_seg_tile(i, nq, off), 0)
    out_map = lambda i: (i, 0)

    def full(shape):
        return pl.BlockSpec(shape, lambda i: (0,) * len(shape))

    in_specs = [pl.BlockSpec((TILE, 512), out_map), pl.BlockSpec((TILE, 384), out_map), pl.BlockSpec((TILE, 256), in_map),
                pl.BlockSpec((TILE, 256), in_map), pl.BlockSpec((TILE, D_MODEL), in_map),
                pl.BlockSpec((1, 1, 6 * D_MODEL), lambda i: (_mod_row(i, nq, off), 0, 0)),
                full((1, D_MODEL)), full((1, 256)), full((MIX_ROWS, D_MODEL)), full((LANE, D_MODEL)), full((LANE, D_MODEL)),
                full((LANE, 1)), full((256, 256))]
    out_specs = [pl.BlockSpec((TILE, D_MODEL), out_map), pl.BlockSpec((TILE * ROW_TILES, LANE), out_map),
                 pl.BlockSpec((TILE, LANE), out_map), pl.BlockSpec((8, TILE), lambda i: (0, i)),
                 pl.BlockSpec((LANE, LANE), lambda i: (0, 0))]
    out_shape = [jax.ShapeDtypeStruct((n_out, D_MODEL), F32), jax.ShapeDtypeStruct((n_out * ROW_TILES, LANE), F32),
                 jax.ShapeDtypeStruct((n_out, LANE), F32), jax.ShapeDtypeStruct((8, n_out), F32),
                 jax.ShapeDtypeStruct((LANE, LANE), F32)]
    return pl.pallas_call(
        _outproj_kernel,
        grid=(n_steps,),
        in_specs=in_specs,
        out_specs=out_specs,
        out_shape=out_shape,
        scratch_shapes=[pltpu.VMEM((LANE, LANE), F32)],
        compiler_params=_cparams(("arbitrary",)),
        name="outproj_route",
    )(do, go, hs, mo, x, mod_l, n2g, hn, w_out_l, wr_hi, wr_lo, rb, head_mat)


DMA_UNROLL = 8


def _for_rows(n, fn):
    groups = lax.shift_right_logical(n, int(math.log2(DMA_UNROLL)))

    def group(g, _):
        for u in range(DMA_UNROLL):
            fn(g * DMA_UNROLL + u, u % 2)
        return 0

    def single(r, _):
        fn(r, 0)
        return 0

    lax.fori_loop(0, groups, group, 0)
    lax.fori_loop(groups * DMA_UNROLL, n, single, 0)


def _expert_kernel(be_ref, nv_ref, src_ref, srcn_ref, dst_ref, f_hbm, w1_ref, w3_ref, w2_ref, y_hbm,
                   xbuf, ybuf, w1s, w3s, w2s, gsem, ssem):
    i = pl.program_id(0)
    nv = nv_ref[i]
    nv_next = nv_ref[i + 1]
    slot = i % 2

    def tile(r):
        return pl.ds(pl.multiple_of(r * ROW_TILES, ROW_TILES), ROW_TILES)

    def gather_copy(idx_ref, s, r):
        return pltpu.make_async_copy(f_hbm.at[tile(idx_ref[0, 0, r])], xbuf.at[s, tile(r)], gsem.at[s])

    def scatter_copy(r):
        return pltpu.make_async_copy(ybuf.at[tile(r)], y_hbm.at[tile(dst_ref[0, 0, r])], ssem)

    def rows(n):
        return pl.ds(0, pl.multiple_of(n * ROW_TILES, ROW_TILES))

    def gather_wait(n):
        pltpu.make_async_copy(f_hbm.at[rows(n)], xbuf.at[slot, rows(n)], gsem.at[slot]).wait()

    def scatter_wait(n):
        pltpu.make_async_copy(ybuf.at[rows(n)], y_hbm.at[rows(n)], ssem).wait()

    @pl.when(i == 0)
    def _():
        xbuf[...] = jnp.zeros_like(xbuf)
        _for_rows(nv, lambda r, pr: gather_copy(src_ref, 0, r).start(priority=pr))

    @pl.when(nv_next > 0)
    def _():
        _for_rows(nv_next, lambda r, pr: gather_copy(srcn_ref, 1 - slot, r).start(priority=pr))

    @pl.when((nv > 0) & ((i == 0) | (be_ref[i] != be_ref[jnp.maximum(i - 1, 0)])))
    def _():
        w1s[...] = w1_ref[0, 0].astype(BF16)
        w3s[...] = w3_ref[0, 0].astype(BF16)
        w2s[...] = w2_ref[0, 0].astype(BF16)

    @pl.when(nv > 0)
    def _():
        gather_wait(nv)
        xb = _load_row_tiles(xbuf, (slot,), MOE_TILE).astype(BF16)
        hh = _silu(_dot(xb, w1s[...])) * _dot(xb, w3s[...])
        y = _dot(hh.astype(BF16), w2s[...])

        @pl.when(i > 0)
        def _():
            scatter_wait(nv_ref[jnp.maximum(i - 1, 0)])

        _store_row_tiles(ybuf, (), MOE_TILE, y)
        _for_rows(nv, lambda r, pr: scatter_copy(r).start(priority=pr))

        @pl.when(nv_next == 0)
        def _():
            scatter_wait(nv)


def _experts(f, row_src, row_dst, blk_expert, blk_valid, w1, w3, w2, layer, n_tok):
    n_blk = row_src.shape[0]
    idx_spec = lambda fn: pl.BlockSpec((1, 1, MOE_TILE), fn, memory_space=pltpu.SMEM)
    w_spec = lambda shape: pl.BlockSpec((1, 1) + shape, lambda i, be, nv: (layer, be[i], 0, 0))
    grid_spec = pltpu.PrefetchScalarGridSpec(
        num_scalar_prefetch=2,
        grid=(n_blk,),
        in_specs=[idx_spec(lambda i, be, nv: (i, 0, 0)),
                  idx_spec(lambda i, be, nv: (jnp.minimum(i + 1, n_blk - 1), 0, 0)),
                  idx_spec(lambda i, be, nv: (i, 0, 0)),
                  pl.BlockSpec(memory_space=pl.ANY),
                  w_spec((D_MODEL, MOE_HIDDEN)), w_spec((D_MODEL, MOE_HIDDEN)), w_spec((MOE_HIDDEN, D_MODEL))],
        out_specs=pl.BlockSpec(memory_space=pl.ANY),
        scratch_shapes=[pltpu.VMEM((2, MOE_TILE * ROW_TILES, LANE), F32), pltpu.VMEM((MOE_TILE * ROW_TILES, LANE), F32),
                        pltpu.VMEM((D_MODEL, MOE_HIDDEN), BF16), pltpu.VMEM((D_MODEL, MOE_HIDDEN), BF16),
                        pltpu.VMEM((MOE_HIDDEN, D_MODEL), BF16),
                        pltpu.SemaphoreType.DMA((2,)), pltpu.SemaphoreType.DMA(())],
    )
    return pl.pallas_call(
        _expert_kernel,
        grid_spec=grid_spec,
        out_shape=jax.ShapeDtypeStruct((2 * n_tok * ROW_TILES, LANE), F32),
        compiler_params=_cparams(("arbitrary",)),
        name="experts",
    )(blk_expert, blk_valid, row_src, row_src, row_dst, f, w1, w3, w2).reshape(2, n_tok * ROW_TILES, LANE)


def _dest_kernel(rt_ref, ps_ref, o_ref):
    rt = rt_ref[...].astype(I32)
    ps = ps_ref[...]
    row = lax.broadcasted_iota(I32, (LANE, rt.shape[1]), 0)

    def dest(k):
        start = jnp.sum(jnp.where(row == rt[k:k + 1, :], ps, 0.0), axis=0, keepdims=True)
        return start.astype(I32) + rt[2 + k:3 + k, :]

    orow = lax.broadcasted_iota(I32, rt.shape, 0)
    o_ref[...] = jnp.where(orow == 0, dest(0), jnp.where(orow == 1, dest(1), 0))


def _pair_dest(route_t, pad_start, n_tok):
    ps = jnp.pad(pad_start.astype(F32), (0, LANE - MOE_EXPERTS)).reshape(LANE, 1)
    cols = next(r for r in (2048, 1024, 512, TILE) if n_tok % r == 0)
    return pl.pallas_call(
        _dest_kernel,
        grid=(n_tok // cols,),
        in_specs=[pl.BlockSpec((8, cols), lambda i: (0, i)), pl.BlockSpec((LANE, 1), lambda i: (0, 0))],
        out_specs=pl.BlockSpec((8, cols), lambda i: (0, i)),
        out_shape=jax.ShapeDtypeStruct((8, n_tok), I32),
        compiler_params=_cparams(("arbitrary",)),
        name="pair_dest",
    )(route_t, ps)


INV_UNROLL = 8


def _inverse_kernel(n_pairs, dest_ref, lo_ref, hi_ref, out_ref):
    def body(g, _):
        for u in range(INV_UNROLL):
            p = g * INV_UNROLL + u
            out_ref[dest_ref[p]] = p
        return 0

    lax.fori_loop(0, n_pairs // INV_UNROLL, body, 0)

    def fill_segment(e, _):
        def fill(r, _):
            out_ref[r] = 0
            return 0
        return lax.fori_loop(lo_ref[e], hi_ref[e], fill, 0)

    lax.fori_loop(0, MOE_EXPERTS + 1, fill_segment, 0)


def _inverse_rows(dest_flat, fill_lo, fill_hi, p_rows):
    n_pairs = dest_flat.shape[0]
    smem = pl.BlockSpec(memory_space=pltpu.SMEM)
    return pl.pallas_call(
        functools.partial(_inverse_kernel, n_pairs),
        in_specs=[smem, smem, smem],
        out_specs=smem,
        out_shape=jax.ShapeDtypeStruct((p_rows,), I32),
        name="inverse_rows",
    )(dest_flat, fill_lo, fill_hi)


def _route_plan(route_t, counts, n_tok):
    n_blk = (2 * n_tok) // MOE_TILE + MOE_EXPERTS
    p_rows = n_blk * MOE_TILE
    cnt = counts[MOE_GROUPS:MOE_GROUPS + MOE_EXPERTS, 0].astype(I32)
    padded = (cnt + MOE_TILE - 1) // MOE_TILE * MOE_TILE
    pad_end = jnp.cumsum(padded)
    pad_start = pad_end - padded
    dest = _pair_dest(route_t, pad_start, n_tok)[0:2].reshape(-1)
    fill_lo = jnp.concatenate([pad_start + cnt, pad_end[-1:]]).astype(I32)
    fill_hi = jnp.concatenate([pad_end, jnp.full((1,), p_rows, I32)]).astype(I32)
    row_dst = _inverse_rows(dest, fill_lo, fill_hi, p_rows)
    row_src = jnp.where(row_dst >= n_tok, row_dst - n_tok, row_dst)
    blk_start = jnp.arange(n_blk + 1, dtype=I32) * MOE_TILE
    blk_expert = jnp.minimum(jnp.sum(pad_end[None, :] <= blk_start[:, None], axis=-1), MOE_EXPERTS - 1).astype(I32)
    in_expert = blk_start - pad_start[blk_expert]
    blk_valid = jnp.where(blk_start < pad_end[-1], jnp.clip(cnt[blk_expert] - in_expert, 0, MOE_TILE), 0).astype(I32)
    return (row_src.reshape(n_blk, 1, MOE_TILE), row_dst.reshape(n_blk, 1, MOE_TILE), blk_expert[:n_blk], blk_valid)


def _final_kernel(x_ref, y_ref, rg_ref, mod_ref, o_ref):
    g2 = mod_ref[0, :, 5 * D_MODEL:6 * D_MODEL]
    rg = rg_ref[...]
    o_ref[...] = x_ref[...] + g2 * (rg[:, 4:5] * _load_row_tiles(y_ref, (0,), TILE) + rg[:, 5:6] * _load_row_tiles(y_ref, (1,), TILE))


def _final(x, y_flat, rg, mod_l, n_tok):
    nq = SEQ // TILE
    tile_map = lambda i: (i, 0)
    return pl.pallas_call(
        _final_kernel,
        grid=(n_tok // TILE,),
        in_specs=[pl.BlockSpec((TILE, D_MODEL), tile_map), pl.BlockSpec((2, TILE * ROW_TILES, LANE), lambda i: (0, i, 0)),
                  pl.BlockSpec((TILE, LANE), tile_map), pl.BlockSpec((1, 1, 6 * D_MODEL), lambda i: (i // nq, 0, 0))],
        out_specs=pl.BlockSpec((TILE, D_MODEL), tile_map),
        out_shape=jax.ShapeDtypeStruct((n_tok, D_MODEL), F32),
        compiler_params=_cparams(("arbitrary",)),
        name="final_residual",
    )(x, y_flat, rg, mod_l)


def _diff_rows(w, heads):
    nl, _, k = w.shape
    w = w.reshape(nl, heads, 2, 2, 2, 12, k)
    w = w.transpose(0, 1, 4, 2, 3, 5, 6)
    w = jnp.pad(w.reshape(nl, heads, 4, 24, k), ((0, 0), (0, 0), (0, 0), (0, 8), (0, 0)))
    return w.reshape(nl, heads * LANE, k)


def _gqa_rows(w, groups):
    nl, _, k = w.shape
    w = w.reshape(nl, 2, groups, 2, 2, 16, k)
    w = w.transpose(0, 2, 4, 1, 3, 5, 6)
    return w.reshape(nl, groups * LANE, k)


def _in_weight(w_in):
    wt = jnp.swapaxes(w_in, 1, 2)
    nl = wt.shape[0]
    dv = jnp.pad(wt[:, 768:1152].reshape(nl, DIFF_HEADS, DIFF_V_DIM, D_MODEL), ((0, 0), (0, 0), (0, LANE - DIFF_V_DIM), (0, 0)))
    parts = [_diff_rows(wt[:, 0:384], DIFF_HEADS), _diff_rows(wt[:, 384:768], DIFF_HEADS),
             dv.reshape(nl, DIFF_HEADS * LANE, D_MODEL), _gqa_rows(wt[:, 1152:1536], 3), _gqa_rows(wt[:, 1536:1664], 1),
             wt[:, 1664:2816], jnp.pad(wt[:, 2816:2832], ((0, 0), (0, LANE - 16), (0, 0)))]
    return jnp.concatenate(parts, axis=1).astype(BF16)


def _lane_gain(g, rows_fn):
    return rows_fn(jnp.concatenate([g, g], axis=-1)[:, :, None], 1)[:, :, 0]


def _out_weight(w_out):
    nl = w_out.shape[0]
    diff = jnp.pad(w_out[:, 0:384].reshape(nl, DIFF_HEADS, DIFF_V_DIM, D_MODEL), ((0, 0), (0, 0), (0, LANE - DIFF_V_DIM), (0, 0)))
    gqa = w_out[:, 384:768].reshape(nl, 2, 3, GQA_HEAD_DIM, D_MODEL).transpose(0, 2, 1, 3, 4)
    return jnp.concatenate([diff.reshape(nl, 512, D_MODEL), gqa.reshape(nl, 384, D_MODEL), w_out[:, 768:]], axis=1).astype(BF16)


def kernel(x, c, ctx, c_ctx, norm1_g, norm2_g, w_mod, b_mod, w_in, w_out, diff_q_norm, diff_k_norm, diff_lambda, diff_subln, gqa_q_norm, gqa_k_norm, mlstm_conv_w, mlstm_conv_b, mlstm_gate_b, mlstm_head_norm, moe_wg, moe_bg, moe_we, moe_be, moe_w1, moe_w3, moe_w2):
    B = x.shape[0]
    nt = B * SEG_ROWS
    n_lat = B * SEQ

    w_in_r = _in_weight(w_in)
    w_out_r = _out_weight(w_out)
    gq_d = _lane_gain(diff_q_norm, _diff_rows) * (LOG2E * DIFF_HEAD_DIM ** -0.5)
    gk_d = _lane_gain(diff_k_norm, _diff_rows)
    gq_g = _lane_gain(gqa_q_norm, _gqa_rows) * (LOG2E * GQA_HEAD_DIM ** -0.5)
    gk_g = _lane_gain(gqa_k_norm, _gqa_rows)
    bound_d = _score_bound(diff_q_norm, diff_k_norm, DIFF_HEAD_DIM).reshape(DEPTH, 1)
    bound_g = _score_bound(gqa_q_norm, gqa_k_norm, GQA_HEAD_DIM).reshape(DEPTH, 1)
    gains = jnp.concatenate([jnp.tile(gq_d, (1, 4)), jnp.tile(gk_d, (1, 4)), jnp.tile(gq_g, (1, 3)), gk_g], axis=1)
    gains = gains.reshape(DEPTH, 1, 1536)
    gate_b = jnp.pad(mlstm_gate_b, ((0, 0), (0, LANE - 16))).reshape(DEPTH, 1, LANE)
    subln = jnp.pad(diff_subln, ((0, 0), (0, LANE - DIFF_V_DIM))).reshape(DEPTH, 1, LANE)
    lam_pad = jnp.pad(diff_lambda, ((0, 0), (0, 4), (0, LANE - DIFF_HEAD_DIM)))
    conv_w = jnp.pad(mlstm_conv_w, ((0, 0), (0, 5), (0, 0)))
    conv_b = mlstm_conv_b.reshape(DEPTH, 1, 512)
    w_r = jnp.swapaxes(jnp.pad(jnp.concatenate([moe_wg, moe_we], axis=2), ((0, 0), (0, 0), (0, LANE - 36))), 1, 2)
    wr_hi = w_r.astype(BF16)
    wr_lo = (w_r - wr_hi.astype(F32)).astype(BF16)
    rb = jnp.pad(jnp.concatenate([moe_bg, moe_be], axis=1), ((0, 0), (0, LANE - 36))).reshape(DEPTH, LANE, 1)
    pair_mat = jnp.asarray(_PAIR_MAT, BF16)
    head_mat = jnp.asarray((np.arange(256)[:, None] // 64 == np.arange(256)[None, :] // 64).astype(np.float32), BF16)
    tabs = _rope_tables(12, 24) + _rope_tables(16, 32)

    xa = jnp.concatenate([ctx, x], axis=1).reshape(nt, D_MODEL)
    assert B <= 8, "row 8 of the modulation table is reserved for the context conditioning"
    cvec = jnp.concatenate([c, jnp.zeros((8 - B, D_MODEL), F32), c_ctx[None, :], jnp.zeros((7, D_MODEL), F32)], axis=0)
    mod = _modulation(cvec, w_mod, b_mod).reshape(DEPTH, 16, 1, 6 * D_MODEL)

    moe_in = None
    for l in range(DEPTH):
        last = l == DEPTH - 1
        lam_init = 0.8 - 0.6 * math.exp(-0.3 * l)
        xa, (dq, dk, dv, gq, gk, gv, mqk, mv, mo, mg) = _inproj(
            xa, moe_in, mod[l], norm1_g[l].reshape(1, D_MODEL), w_in_r[l], gains[l], gate_b[l], tabs, pair_mat, nt)
        do = _diff_attention(bound_d[l], dq, dk, dv, lam_pad[l], subln[l], lam_init, nt, with_ctx=not last)
        go = _gqa_attention(bound_g[l], gq, gk, gv, nt, with_ctx=not last)
        mgt = mg[:, :16].reshape(B, N_CHUNKS, MLSTM_CHUNK, 16).transpose(0, 1, 3, 2)
        hs = _mlstm(mqk, mv, mg, mgt, conv_w[l], conv_b[l], nt)
        xm, f, route, route_t, counts = _outproj(do, go, hs, mo, xa, mod[l], norm2_g[l].reshape(1, D_MODEL),
                                         mlstm_head_norm[l].reshape(1, 256), w_out_r[l], wr_hi[l], wr_lo[l], rb[l],
                                         head_mat, nt, with_ctx=not last)
        n_tok = n_lat if last else nt
        row_src, row_dst, blk_expert, blk_valid = _route_plan(route_t, counts, n_tok)
        y = _experts(f, row_src, row_dst, blk_expert, blk_valid, moe_w1, moe_w3, moe_w2, l, n_tok)
        if last:
            out = _final(xm, y, route, mod[l][:B], n_tok)
            return out.reshape(B, SEQ, D_MODEL)
        xa = xm
        moe_in = (y, route, mod[l])
```

```python
import functools
import math

import numpy as np
import jax
import jax.numpy as jnp
from jax import lax
from jax.experimental import pallas as pl
from jax.experimental.pallas import tpu as pltpu

F32 = jnp.float32
BF16 = jnp.bfloat16
I32 = jnp.int32

D_MODEL = 1024
DEPTH = 4
GRID_W = 64
CTX_LEN = 256
SEQ = 2048
ROPE_THETA = 10000.0
EPS = 1e-6

DIFF_HEADS = 4
DIFF_HEAD_DIM = 48
DIFF_V_DIM = 96
GQA_Q_HEADS = 6
GQA_KV_HEADS = 2
GQA_HEAD_DIM = 64
MLSTM_HEADS = 4
MLSTM_HEAD_DIM = 64
MLSTM_CHUNK = 64
IN_WIDTH = 2832
MOE_GROUPS = 4
MOE_EPG = 8
MOE_EXPERTS = 32
MOE_HIDDEN = 512

LANE = 128
TILE = 256
SEG_TILES = (CTX_LEN + SEQ) // TILE
SEG_ROWS = CTX_LEN + SEQ
N_CHUNKS = SEG_ROWS // MLSTM_CHUNK
CTX_CHUNKS = CTX_LEN // MLSTM_CHUNK
MOE_TILE = 256
VMEM_LIMIT = 56 * 1024 * 1024

C_DQ, C_DK, C_DV, C_GQ, C_GK, C_GV, C_MQK, C_MV, C_MO, C_MG, C_END = (
    0, 512, 1024, 1536, 1920, 2048, 2176, 2688, 2944, 3200, 3328)
MIX_ROWS = 512 + 384 + 256


def _cparams(sem):
    return pltpu.CompilerParams(dimension_semantics=sem, vmem_limit_bytes=VMEM_LIMIT)


_LANE2 = np.arange(2 * LANE)
_PAIR_MAT = ((_LANE2[:, None] // LANE == _LANE2[None, :] // LANE)
             & ((_LANE2[:, None] // 32) % 2 == (_LANE2[None, :] // 32) % 2)).astype(np.float32)


def _rope_tables(nf, pad_from):
    t = jnp.arange(SEQ, dtype=I32)
    rows = (t // GRID_W).astype(F32)
    cols = (t % GRID_W).astype(F32)
    freqs = ROPE_THETA ** (-jnp.arange(nf, dtype=F32) / nf)
    lane = np.arange(LANE)
    i = lane % 32
    typ = (lane // 32) // 2
    use_rows = i < nf
    fidx = np.where(use_rows, i, i - nf)
    valid = i < pad_from
    fidx = np.where(valid, fidx, 0)
    ang = jnp.where(jnp.asarray(use_rows)[None, :], rows[:, None], cols[:, None]) * freqs[jnp.asarray(fidx)][None, :]
    cos = jnp.where(jnp.asarray(valid)[None, :], jnp.cos(ang), 1.0)
    sin = jnp.where(jnp.asarray(valid)[None, :], jnp.sin(ang), 0.0)
    sin = sin * jnp.asarray(np.where(typ == 0, -1.0, 1.0), F32)[None, :]
    cos = jnp.concatenate([jnp.ones((CTX_LEN, LANE), F32), cos], axis=0)
    sin = jnp.concatenate([jnp.zeros((CTX_LEN, LANE), F32), sin], axis=0)
    return cos, sin


def _dot(a, b):
    return jnp.dot(a, b, preferred_element_type=F32)


def _dot_nt(a, b):
    return lax.dot_general(a, b, (((1,), (1,)), ((), ())), preferred_element_type=F32)


def _dot_tn(a, b):
    return lax.dot_general(a, b, (((0,), (0,)), ((), ())), preferred_element_type=F32)


def _split3(x):
    x1 = x.astype(BF16)
    r1 = x - x1.astype(F32)
    x2 = r1.astype(BF16)
    x3 = (r1 - x2.astype(F32)).astype(BF16)
    return x1, x2, x3


def _dot_f32_by_exact(x, m):
    x1, x2, x3 = _split3(x)
    return _dot(x1, m) + _dot(x2, m) + _dot(x3, m)


def _dot_f32_by_exact2(x, m):
    x1 = x.astype(BF16)
    x2 = (x - x1.astype(F32)).astype(BF16)
    return _dot(x1, m) + _dot(x2, m)


def _exact_by_dot_f32(m, x):
    x1, x2, x3 = _split3(x)
    return _dot(m, x1) + _dot(m, x2) + _dot(m, x3)


def _sigmoid(x):
    return 1.0 / (1.0 + jnp.exp(-x))


def _silu(x):
    return x * _sigmoid(x)


def _log_sigmoid(x):
    return jnp.minimum(x, 0.0) - jnp.log1p(jnp.exp(-jnp.abs(x)))


ROW_TILES = D_MODEL // LANE


def _store_row_tiles(ref, lead, rows, x):
    for c in range(ROW_TILES):
        ref[lead + (pl.ds(c, rows, stride=ROW_TILES), slice(None))] = x[:, c * LANE:(c + 1) * LANE]


def _load_row_tiles(ref, lead, rows):
    return jnp.concatenate([ref[lead + (pl.ds(c, rows, stride=ROW_TILES), slice(None))] for c in range(ROW_TILES)], axis=1)


def _seg_tile(i, nq, off):
    return (i // nq) * SEG_TILES + off + i % nq


def _mod_row(i, nq, off):
    return jnp.where((off + i % nq) == 0, 8, i // nq)


MOD_BN = 1536


def _mod_kernel(c_ref, w_ref, b_ref, o_ref):
    a = _silu(c_ref[...]).astype(BF16)
    o_ref[0] = _dot(a, w_ref[0].astype(BF16)) + b_ref[0]


def _modulation(cvec, w_mod, b_mod):
    nb = 6 * D_MODEL // MOD_BN
    return pl.pallas_call(
        _mod_kernel,
        grid=(DEPTH, nb),
        in_specs=[pl.BlockSpec((16, D_MODEL), lambda l, n: (0, 0)),
                  pl.BlockSpec((1, D_MODEL, MOD_BN), lambda l, n: (l, 0, n)),
                  pl.BlockSpec((1, 1, MOD_BN), lambda l, n: (l, 0, n))],
        out_specs=pl.BlockSpec((1, 16, MOD_BN), lambda l, n: (l, 0, n)),
        out_shape=jax.ShapeDtypeStruct((DEPTH, 16, 6 * D_MODEL), F32),
        compiler_params=_cparams(("arbitrary", "arbitrary")),
        name="modulation",
    )(cvec, w_mod, b_mod.reshape(DEPTH, 1, 6 * D_MODEL))


def _inproj_kernel(has_moe, *refs):
    if has_moe:
        (x_ref, y_ref, rg_ref, modp_ref, mod_ref, n1_ref, w_ref, gains_ref, gb_ref, cd_ref, sd_ref, cg_ref, sg_ref,
         pm_ref, xo_ref, dq_ref, dk_ref, dv_ref, gq_ref, gk_ref, gv_ref, mqk_ref, mv_ref, mo_ref, mg_ref) = refs
    else:
        (x_ref, mod_ref, n1_ref, w_ref, gains_ref, gb_ref, cd_ref, sd_ref, cg_ref, sg_ref,
         pm_ref, dq_ref, dk_ref, dv_ref, gq_ref, gk_ref, gv_ref, mqk_ref, mv_ref, mo_ref, mg_ref) = refs
    x = x_ref[...]
    if has_moe:
        g2 = modp_ref[0, :, 5 * D_MODEL:6 * D_MODEL]
        rg = rg_ref[...]
        x = x + g2 * (rg[:, 4:5] * _load_row_tiles(y_ref, (0,), TILE) + rg[:, 5:6] * _load_row_tiles(y_ref, (1,), TILE))
        xo_ref[...] = x
    sh = mod_ref[0, :, 0:D_MODEL]
    sc = mod_ref[0, :, D_MODEL:2 * D_MODEL]
    xn = x * lax.rsqrt(jnp.mean(x * x, axis=-1, keepdims=True) + EPS) * n1_ref[...]
    h = (xn * (1.0 + sc) + sh).astype(BF16)
    pm = pm_ref[...]

    def proj(a, b):
        return _dot_nt(h, w_ref[a:b, :])

    def qk_group(col, gain_col, inv_dim, cos, sin, outs):
        y = proj(col, col + 512)
        for half in range(2):
            yh = y[:, half * 256:(half + 1) * 256]
            yn = yh * lax.rsqrt(_dot_f32_by_exact2(yh * yh, pm) * inv_dim + EPS)
            yn = yn * gains_ref[:, gain_col + half * 256:gain_col + (half + 1) * 256]
            for j in range(2):
                yb = yn[:, j * LANE:(j + 1) * LANE]
                ref, off = outs[half * 2 + j]
                ref[:, off:off + LANE] = (yb * cos + pltpu.roll(yb, 64, 1) * sin).astype(BF16)

    cd, sd, cg, sg = cd_ref[...], sd_ref[...], cg_ref[...], sg_ref[...]
    qk_group(C_DQ, 0, 1.0 / DIFF_HEAD_DIM, cd, sd, [(dq_ref, b * LANE) for b in range(4)])
    qk_group(C_DK, 512, 1.0 / DIFF_HEAD_DIM, cd, sd, [(dk_ref, b * LANE) for b in range(4)])
    qk_group(C_GQ, 1024, 1.0 / GQA_HEAD_DIM, cg, sg, [(gq_ref, 0), (gq_ref, LANE), (gq_ref, 2 * LANE), (gk_ref, 0)])
    dv_ref[...] = proj(C_DV, C_GQ).astype(BF16)
    rest = proj(C_GV, C_END)
    gv_ref[...] = rest[:, 0:C_MQK - C_GV].astype(BF16)
    mqk_ref[...] = rest[:, C_MQK - C_GV:C_MV - C_GV]
    mv_ref[...] = rest[:, C_MV - C_GV:C_MO - C_GV].astype(BF16)
    mo_ref[...] = rest[:, C_MO - C_GV:C_MG - C_GV]
    mg_ref[...] = rest[:, C_MG - C_GV:C_END - C_GV] + gb_ref[...]


def _inproj(x, moe_in, mod_l, n1g, w_in_l, gains, gate_b, tabs, pair_mat, nt):
    has_moe = moe_in is not None
    n_tiles = nt // TILE
    tile_map = lambda i: (i, 0)
    mod_spec = pl.BlockSpec((1, 1, 6 * D_MODEL), lambda i: (_mod_row(i, SEG_TILES, 0), 0, 0))
    tab_spec = pl.BlockSpec((TILE, LANE), lambda i: (i % SEG_TILES, 0))

    def full(shape):
        return pl.BlockSpec(shape, lambda i: (0,) * len(shape))

    in_specs = [pl.BlockSpec((TILE, D_MODEL), tile_map)]
    args = [x]
    if has_moe:
        y_flat, rg, mod_prev = moe_in
        in_specs += [pl.BlockSpec((2, TILE * ROW_TILES, LANE), lambda i: (0, i, 0)), pl.BlockSpec((TILE, LANE), tile_map), mod_spec]
        args += [y_flat, rg, mod_prev]
    in_specs += [mod_spec, full((1, D_MODEL)), full((C_END, D_MODEL)), full((1, 1536)), full((1, LANE)),
                 tab_spec, tab_spec, tab_spec, tab_spec, full((2 * LANE, 2 * LANE))]
    args += [mod_l, n1g, w_in_l, gains, gate_b, tabs[0], tabs[1], tabs[2], tabs[3], pair_mat]

    def o(width, dtype):
        return pl.BlockSpec((TILE, width), tile_map), jax.ShapeDtypeStruct((nt, width), dtype)

    outs = []
    if has_moe:
        outs.append(o(D_MODEL, F32))
    outs += [o(512, BF16), o(512, BF16), o(512, BF16), o(384, BF16), o(LANE, BF16), o(LANE, BF16),
             o(512, F32), o(256, BF16), o(256, F32), o(LANE, F32)]
    res = pl.pallas_call(
        functools.partial(_inproj_kernel, has_moe),
        grid=(n_tiles,),
        in_specs=in_specs,
        out_specs=[s for s, _ in outs],
        out_shape=[s for _, s in outs],
        compiler_params=_cparams(("arbitrary",)),
        name="inproj_moe" if has_moe else "inproj",
    )(*args)
    if has_moe:
        return res[0], res[1:]
    return x, res


def _lambda_value(lam_ref, lam_init):
    lam = lam_ref[...]
    s01 = jnp.sum(lam[0:1] * lam[1:2], axis=-1, keepdims=True)
    s23 = jnp.sum(lam[2:3] * lam[3:4], axis=-1, keepdims=True)
    return jnp.exp(s01) - jnp.exp(s23) + lam_init


LOG2E = 1.4426950408889634
SAFE_LOG2_RANGE = 60.0


def _exp_scores(s, stabilise):
    if stabilise:
        s = s - jnp.max(s, axis=-1, keepdims=True)
    return jnp.exp2(s).astype(BF16)


def _pv(q, kk, kmask, vv, stabilise):
    return _dot(_exp_scores(_dot_nt(q, kk * kmask), stabilise), vv)


def _score_bound(q_gain, k_gain, head_dim):
    return (1.02 * LOG2E * math.sqrt(head_dim)) * jnp.max(jnp.abs(q_gain), axis=-1) * jnp.max(jnp.abs(k_gain), axis=-1)


def _attn_branches(with_ctx, bound_ref, run, k_ref, v_ref):
    def on_keys(rows):
        small = bound_ref[0] <= SAFE_LOG2_RANGE

        @pl.when(small)
        def _():
            run(k_ref[0:rows, :], v_ref[0:rows, :], False)

        @pl.when(jnp.logical_not(small))
        def _():
            run(k_ref[0:rows, :], v_ref[0:rows, :], True)

    if not with_ctx:
        on_keys(SEG_ROWS)
        return
    t = pl.program_id(2)

    @pl.when(t == 0)
    def _():
        on_keys(TILE)

    @pl.when(t > 0)
    def _():
        on_keys(SEG_ROWS)


def _attn_maps(with_ctx):
    if with_ctx:
        m = lambda b, h, t: (b * SEG_TILES + t, h)
        return SEG_TILES, m, m
    nq = SEG_TILES - 1
    return nq, (lambda b, h, t: (b * SEG_TILES + 1 + t, h)), (lambda b, h, t: (b * nq + t, h))


def _lane_masks():
    lane = np.arange(LANE)
    even = (lane // 32) % 2 == 0
    rows = [even, ~even, lane < 64, lane >= 64, lane == 64, lane == 0, lane == DIFF_V_DIM, lane < 0]
    return jnp.asarray(np.stack(rows).astype(np.float32))


def _mask_row(lm_ref, r):
    return lm_ref[r:r + 1, :].astype(BF16)


def _diff_attn_kernel(lam_init, with_ctx, bound_ref, q_ref, k_ref, v_ref, lam_ref, sg_ref, lm_ref, o_ref):
    def run(kk, vv, stabilise):
        q = q_ref[...]
        v1 = vv + _mask_row(lm_ref, 6)
        o1 = _pv(q, kk, _mask_row(lm_ref, 0), v1, stabilise)
        o2 = _pv(q, kk, _mask_row(lm_ref, 1), v1, stabilise)
        lam = _lambda_value(lam_ref, lam_init)
        o = o1 * (1.0 / o1[:, DIFF_V_DIM:DIFF_V_DIM + 1]) - o2 * (lam / o2[:, DIFF_V_DIM:DIFF_V_DIM + 1])
        o = jnp.where(lax.broadcasted_iota(I32, o.shape, 1) < DIFF_V_DIM, o, 0.0)
        ms = jnp.sum(o * o, axis=-1, keepdims=True) * (1.0 / DIFF_V_DIM)
        o_ref[...] = (o * lax.rsqrt(ms + EPS) * sg_ref[...] * (1.0 - lam_init)).astype(BF16)

    _attn_branches(with_ctx, bound_ref, run, k_ref, v_ref)


def _diff_attention(bound, dq, dk, dv, lam_pad, subln, lam_init, nt, with_ctx):
    nb = nt // SEG_ROWS
    nq, q_map, o_map = _attn_maps(with_ctx)
    kv_map = lambda b, h, t: (b, h)
    const = lambda b, h, t: (0, 0)
    return pl.pallas_call(
        functools.partial(_diff_attn_kernel, lam_init, with_ctx),
        grid=(nb, DIFF_HEADS, nq),
        in_specs=[pl.BlockSpec(memory_space=pltpu.SMEM),
                  pl.BlockSpec((TILE, LANE), q_map), pl.BlockSpec((SEG_ROWS, LANE), kv_map),
                  pl.BlockSpec((SEG_ROWS, LANE), kv_map), pl.BlockSpec((8, LANE), const),
                  pl.BlockSpec((1, LANE), const), pl.BlockSpec((8, LANE), const)],
        out_specs=pl.BlockSpec((TILE, LANE), o_map),
        out_shape=jax.ShapeDtypeStruct((nb * nq * TILE, 512), BF16),
        compiler_params=_cparams(("arbitrary", "arbitrary", "arbitrary")),
        name="diff_attn",
    )(bound, dq, dk, dv, lam_pad, subln, _lane_masks())


def _gqa_attn_kernel(with_ctx, bound_ref, q_ref, k_ref, v_ref, lm_ref, o_ref):
    def run(kk, vv, stabilise):
        q = q_ref[...]
        va = vv * _mask_row(lm_ref, 2) + _mask_row(lm_ref, 4)
        vb = vv * _mask_row(lm_ref, 3) + _mask_row(lm_ref, 5)
        oa = _pv(q, kk, _mask_row(lm_ref, 0), va, stabilise)
        ob = _pv(q, kk, _mask_row(lm_ref, 1), vb, stabilise)
        lane = lax.broadcasted_iota(I32, oa.shape, 1)
        o_ref[...] = jnp.where(lane < GQA_HEAD_DIM, oa * (1.0 / oa[:, GQA_HEAD_DIM:GQA_HEAD_DIM + 1]),
                               ob * (1.0 / ob[:, 0:1])).astype(BF16)

    _attn_branches(with_ctx, bound_ref, run, k_ref, v_ref)


def _gqa_attention(bound, gq, gk, gv, nt, with_ctx):
    nb = nt // SEG_ROWS
    nq, q_map, o_map = _attn_maps(with_ctx)
    kv_map = lambda b, p, t: (b, 0)
    return pl.pallas_call(
        functools.partial(_gqa_attn_kernel, with_ctx),
        grid=(nb, 3, nq),
        in_specs=[pl.BlockSpec(memory_space=pltpu.SMEM),
                  pl.BlockSpec((TILE, LANE), q_map), pl.BlockSpec((SEG_ROWS, LANE), kv_map),
                  pl.BlockSpec((SEG_ROWS, LANE), kv_map), pl.BlockSpec((8, LANE), lambda b, p, t: (0, 0))],
        out_specs=pl.BlockSpec((TILE, LANE), o_map),
        out_shape=jax.ShapeDtypeStruct((nb * nq * TILE, 384), BF16),
        compiler_params=_cparams(("arbitrary", "arbitrary", "arbitrary")),
        name="gqa_attn",
    )(bound, gq, gk, gv, _lane_masks())


def _mlstm_kernel(nbk, mqk_ref, mv_ref, mg_ref, mgt_ref, cw_ref, cb_ref, h_ref, qk_s, bcf_s, bcb_s, brf_s, brb_s):
    L = MLSTM_CHUNK
    h_ref[...] = jnp.zeros_like(h_ref)
    w0, w1, w2, cb = cw_ref[0:1, :], cw_ref[1:2, :], cw_ref[2:3, :], cb_ref[...]
    rid = lax.broadcasted_iota(I32, (TILE, 512), 0)
    kscale = jnp.where(lax.broadcasted_iota(I32, (1, 512), 1) < 256, 1.0, MLSTM_HEAD_DIM ** -0.5)
    zrow = jnp.zeros((1, 512), F32)
    for bc in range(nbk * SEG_TILES):
        c = bc % SEG_TILES
        r0 = bc * TILE
        xc = mqk_ref[r0:r0 + TILE, :]
        prev = zrow if c in (0, 1) else mqk_ref[r0 - 1:r0, :]
        nxt = zrow if c in (0, SEG_TILES - 1) else mqk_ref[r0 + TILE:r0 + TILE + 1, :]
        up = jnp.where(rid == 0, prev, pltpu.roll(xc, 1, 0))
        dn = jnp.where(rid == TILE - 1, nxt, pltpu.roll(xc, TILE - 1, 0))
        y = w0 * up + w1 * xc + w2 * dn + cb
        qk_s[r0:r0 + TILE, :] = (_silu(y) * kscale).astype(BF16)

    ti = lax.broadcasted_iota(I32, (L, L), 0)
    si = lax.broadcasted_iota(I32, (L, L), 1)
    tri_le = (si <= ti)
    m_le = tri_le.astype(BF16)
    m_ge = (si >= ti).astype(BF16)
    row2 = lax.broadcasted_iota(I32, (2 * L, L), 0)
    trow = row2 % L
    scol = lax.broadcasted_iota(I32, (2 * L, L), 1)
    top2 = row2 < L
    mask_f = scol <= trow
    mask_b = scol >= trow
    lane_lo = lax.broadcasted_iota(I32, (L, LANE), 1) < L
    lane_lo256 = lax.broadcasted_iota(I32, (L, 2 * LANE), 1) % LANE < L
    rr = lax.broadcasted_iota(I32, (LANE, LANE), 0)
    cc = lax.broadcasted_iota(I32, (LANE, LANE), 1)
    blockdiag = (rr < L) == (cc < L)
    rows_lo = lax.broadcasted_iota(I32, (LANE, 1), 0) < L
    top_col = lax.broadcasted_iota(I32, (2 * L, 1), 0) < L

    def chain(bb, c, is_fwd, p, gcol, bcol_all, grow, brow_all, state):
        ct, nm, m0, m1 = state
        h0, h1 = 2 * p, 2 * p + 1
        gi, gf = (0, 4) if is_fwd else (8, 12)
        r0 = pl.multiple_of(bb * SEG_ROWS + c * L, L)
        qb = qk_s[pl.ds(r0, L), p * LANE:(p + 1) * LANE]
        kb = qk_s[pl.ds(r0, L), 256 + p * LANE:256 + (p + 1) * LANE]
        v128 = mv_ref[pl.ds(r0, L), p * LANE:(p + 1) * LANE]

        def stack_cols(arr, j0, j1):
            return jnp.concatenate([arr[:, j0:j0 + 1], arr[:, j1:j1 + 1]], axis=0)

        def stack_rows(arr, j0, j1):
            return jnp.where(top2, arr[j0:j0 + 1, :], arr[j1:j1 + 1, :])

        bcol = stack_cols(bcol_all, gf + h0, gf + h1)
        licol = stack_cols(gcol, gi + h0, gi + h1)
        crow = stack_rows(grow, gi + h0, gi + h1) - stack_rows(brow_all, gf + h0, gf + h1)
        cm = jnp.where(mask_f if is_fwd else mask_b, crow, -jnp.inf)
        mcol = jnp.where(top_col, m0, m1)
        u = jnp.maximum(mcol, jnp.max(cm, axis=-1, keepdims=True))
        zq = jnp.zeros_like(qb)
        qstack = jnp.concatenate([jnp.where(lane_lo, qb, zq), jnp.where(lane_lo, zq, qb)], axis=0)
        w = (jnp.exp(cm - u) * _dot_nt(qstack, kb)).astype(BF16)
        wv = _dot(w, jnp.concatenate([v128, jnp.ones_like(v128)], axis=1))
        wv = jnp.where(lane_lo256, wv[:L], wv[L:])
        qcn = _dot(qb, jnp.concatenate([ct, nm], axis=1).astype(BF16))
        a = jnp.exp(mcol - u)
        emt = jnp.exp(-(bcol + u))
        a128 = jnp.where(lane_lo, a[:L], a[L:])
        emt128 = jnp.where(lane_lo, emt[:L], emt[L:])
        num = wv[:, :LANE] + a128 * qcn[:, :LANE]
        den = wv[:, LANE:] + a128 * qcn[:, LANE:]
        h_ref[pl.ds(r0, L), p * LANE:(p + 1) * LANE] += num / jnp.maximum(jnp.abs(den), emt128)
        e0 = (L - 1) if is_fwd else 0
        bend0 = bcol[e0:e0 + 1, :]
        bend1 = bcol[L + e0:L + e0 + 1, :]
        bend = jnp.where(top_col, bend0, bend1)
        g = bend - bcol + licol
        m0n = jnp.maximum(bend0 + m0, jnp.max(g[:L], axis=0, keepdims=True))
        m1n = jnp.maximum(bend1 + m1, jnp.max(g[L:], axis=0, keepdims=True))
        ws = jnp.exp(g - jnp.where(top_col, m0n, m1n))
        ae0 = jnp.exp(bend0 + m0 - m0n)
        ae1 = jnp.exp(bend1 + m1 - m1n)
        ws128 = jnp.where(lane_lo, ws[:L], ws[L:])
        vw = jnp.concatenate([v128.astype(F32) * ws128, ws128], axis=1).astype(BF16)
        upd = _dot_tn(kb, vw)
        ae = jnp.where(rows_lo, ae0, ae1)
        ct_new = ae * ct + jnp.where(blockdiag, upd[:, :LANE], 0.0)
        nm_new = ae * nm + jnp.where(blockdiag, upd[:, LANE:], 0.0)
        return ct_new, nm_new, m0n, m1n

    def gate_sums(j, _):
        for bb in range(nbk):
            r0 = pl.multiple_of(bb * SEG_ROWS + j * L, L)
            lf_col = _log_sigmoid(mg_ref[pl.ds(r0, L), :])
            lf_row = _log_sigmoid(mgt_ref[bb, j])
            bcf_s[pl.ds(r0, L), :] = _exact_by_dot_f32(m_le, lf_col)
            bcb_s[pl.ds(r0, L), :] = _exact_by_dot_f32(m_ge, lf_col)
            brf_s[bb * N_CHUNKS + j] = _dot_f32_by_exact(lf_row, m_ge)
            brb_s[bb * N_CHUNKS + j] = _dot_f32_by_exact(lf_row, m_le)
        return 0

    lax.fori_loop(0, N_CHUNKS, gate_sums, 0, unroll=4)

    def body(i, carry):
        cf = i
        cbk = jnp.where(i < CTX_CHUNKS, CTX_CHUNKS - 1 - i, N_CHUNKS + CTX_CHUNKS - 1 - i)
        new = []
        for bb in range(nbk):
            for d, c in enumerate((cf, cbk)):
                is_fwd = d == 0
                r0 = pl.multiple_of(bb * SEG_ROWS + c * L, L)
                gcol = mg_ref[pl.ds(r0, L), :]
                grow = mgt_ref[bb, c]
                bcol_all = (bcf_s if is_fwd else bcb_s)[pl.ds(r0, L), :]
                brow_all = (brf_s if is_fwd else brb_s)[bb * N_CHUNKS + c]
                for p in range(2):
                    st = carry[(bb * 2 + d) * 2 + p]
                    new.append(chain(bb, c, is_fwd, p, gcol, bcol_all, grow, brow_all, st))
        return tuple(new)

    z = (jnp.zeros((LANE, LANE), F32), jnp.zeros((LANE, LANE), F32), jnp.zeros((1, 1), F32), jnp.zeros((1, 1), F32))
    lax.fori_loop(0, N_CHUNKS, body, (z,) * (4 * nbk), unroll=9)


def _mlstm(mqk, mv, mg, mgt, conv_w, conv_b, nt):
    nb = nt // SEG_ROWS
    nbk = 1
    blk = lambda w: pl.BlockSpec((nbk * SEG_ROWS, w), lambda g: (g, 0))
    return pl.pallas_call(
        functools.partial(_mlstm_kernel, nbk),
        grid=(nb // nbk,),
        in_specs=[blk(512), blk(256), blk(LANE),
                  pl.BlockSpec((nbk, N_CHUNKS, 16, MLSTM_CHUNK), lambda g: (g, 0, 0, 0)),
                  pl.BlockSpec((8, 512), lambda g: (0, 0)), pl.BlockSpec((1, 512), lambda g: (0, 0))],
        out_specs=blk(256),
        out_shape=jax.ShapeDtypeStruct((nt, 256), F32),
        scratch_shapes=[pltpu.VMEM((nbk * SEG_ROWS, 512), BF16), pltpu.VMEM((nbk * SEG_ROWS, LANE), F32),
                        pltpu.VMEM((nbk * SEG_ROWS, LANE), F32), pltpu.VMEM((nbk * N_CHUNKS, 16, MLSTM_CHUNK), F32),
                        pltpu.VMEM((nbk * N_CHUNKS, 16, MLSTM_CHUNK), F32)],
        compiler_params=_cparams(("arbitrary",)),
        name="mlstm",
    )(mqk, mv, mg, mgt, conv_w, conv_b)


def _outproj_kernel(do_ref, go_ref, hs_ref, mo_ref, x_ref, mod_ref, n2_ref, hn_ref, wo_ref, wr_hi_ref, wr_lo_ref,
                    rb_ref, hm_ref, xm_ref, f_ref, rt_ref, rtt_ref, cnt_ref, carry):
    i = pl.program_id(0)

    @pl.when(i == 0)
    def _():
        carry[...] = jnp.zeros_like(carry)

    hsum = hs_ref[...]
    ssq = _dot_f32_by_exact2(hsum * hsum, hm_ref[...])
    ml = hsum * lax.rsqrt(ssq * (1.0 / MLSTM_HEAD_DIM) + EPS) * hn_ref[...] * _sigmoid(mo_ref[...])
    acc = _dot(do_ref[...], wo_ref[0:512, :])
    acc += _dot(go_ref[...], wo_ref[512:896, :])
    acc += _dot(ml.astype(BF16), wo_ref[896:MIX_ROWS, :])
    g1 = mod_ref[0, :, 2 * D_MODEL:3 * D_MODEL]
    sh2 = mod_ref[0, :, 3 * D_MODEL:4 * D_MODEL]
    sc2 = mod_ref[0, :, 4 * D_MODEL:5 * D_MODEL]
    x = x_ref[...] + g1 * acc
    xm_ref[...] = x
    xn = x * lax.rsqrt(jnp.mean(x * x, axis=-1, keepdims=True) + EPS) * n2_ref[...]
    f = xn * (1.0 + sc2) + sh2
    _store_row_tiles(f_ref, (), TILE, f)
    f1, f2, _ = _split3(f)
    lt = _dot_nt(wr_hi_ref[...], f1) + _dot_nt(wr_hi_ref[...], f2) + _dot_nt(wr_lo_ref[...], f1) + rb_ref[...]
    row = lax.broadcasted_iota(I32, lt.shape, 0)
    neg = jnp.float32(-jnp.inf)
    big = jnp.int32(1 << 20)
    lg = jnp.where(row < MOE_GROUPS, lt, neg)
    gmax = jnp.max(lg, axis=0, keepdims=True)
    g_top = 1.0 / jnp.sum(jnp.exp(lg - gmax), axis=0, keepdims=True)
    g_idx = jnp.min(jnp.where(lg == gmax, row, big), axis=0, keepdims=True)
    in_grp = (row >= MOE_GROUPS) & (row < MOE_GROUPS + MOE_EXPERTS) & ((row - MOE_GROUPS) // MOE_EPG == g_idx)
    le = jnp.where(in_grp, lt, neg)
    v1 = jnp.max(le, axis=0, keepdims=True)
    l1 = jnp.min(jnp.where(le == v1, row, big), axis=0, keepdims=True)
    le2 = jnp.where(row == l1, neg, le)
    v2 = jnp.max(le2, axis=0, keepdims=True)
    l2 = jnp.min(jnp.where(le2 == v2, row, big), axis=0, keepdims=True)
    ex = jnp.exp(v2 - v1)
    gate1 = g_top / (1.0 + ex)
    gate2 = gate1 * ex
    oh1 = (row == l1)
    oh2 = (row == l2)
    both = oh1.astype(BF16) + oh2.astype(BF16)
    ri_ = lax.broadcasted_iota(I32, (TILE, TILE), 0)
    ci_ = lax.broadcasted_iota(I32, (TILE, TILE), 1)
    before = _dot(both, (ri_ < ci_).astype(BF16)) + carry[:, 0:1]
    rank1 = jnp.sum(jnp.where(oh1, before, 0.0), axis=0, keepdims=True)
    rank2 = jnp.sum(jnp.where(oh2, before, 0.0), axis=0, keepdims=True)
    carry[...] = carry[...] + jnp.sum(both.astype(F32), axis=1, keepdims=True)
    cnt_ref[...] = carry[...]
    packed = jnp.where(row == 0, (l1 - MOE_GROUPS).astype(F32), jnp.where(row == 1, (l2 - MOE_GROUPS).astype(F32),
             jnp.where(row == 2, rank1, jnp.where(row == 3, rank2, jnp.where(row == 4, gate1, jnp.where(row == 5, gate2, 0.0))))))
    rt_ref[...] = packed.T
    rtt_ref[...] = packed[0:8, :]


def _outproj(do, go, hs, mo, x, mod_l, n2g, hn, w_out_l, wr_hi, wr_lo, rb, head_mat, nt, with_ctx):
    nb = nt // SEG_ROWS
    nq, off = (SEG_TILES, 0) if with_ctx else (SEG_TILES - 1, 1)
    n_steps = nb * nq
    n_out = n_steps * TILE
    in_map = lambda i: (_seg_tile(i, nq, off), 0)
    out_map = lambda i: (i, 0)

    def full(shape):
        return pl.BlockSpec(shape, lambda i: (0,) * len(shape))

    in_specs = [pl.BlockSpec((TILE, 512), out_map), pl.BlockSpec((TILE, 384), out_map), pl.BlockSpec((TILE, 256), in_map),
                pl.BlockSpec((TILE, 256), in_map), pl.BlockSpec((TILE, D_MODEL), in_map),
                pl.BlockSpec((1, 1, 6 * D_MODEL), lambda i: (_mod_row(i, nq, off), 0, 0)),
                full((1, D_MODEL)), full((1, 256)), full((MIX_ROWS, D_MODEL)), full((LANE, D_MODEL)), full((LANE, D_MODEL)),
                full((LANE, 1)), full((256, 256))]
    out_specs = [pl.BlockSpec((TILE, D_MODEL), out_map), pl.BlockSpec((TILE * ROW_TILES, LANE), out_map),
                 pl.BlockSpec((TILE, LANE), out_map), pl.BlockSpec((8, TILE), lambda i: (0, i)),
                 pl.BlockSpec((LANE, LANE), lambda i: (0, 0))]
    out_shape = [jax.ShapeDtypeStruct((n_out, D_MODEL), F32), jax.ShapeDtypeStruct((n_out * ROW_TILES, LANE), F32),
                 jax.ShapeDtypeStruct((n_out, LANE), F32), jax.ShapeDtypeStruct((8, n_out), F32),
                 jax.ShapeDtypeStruct((LANE, LANE), F32)]
    return pl.pallas_call(
        _outproj_kernel,
        grid=(n_steps,),
        in_specs=in_specs,
        out_specs=out_specs,
        out_shape=out_shape,
        scratch_shapes=[pltpu.VMEM((LANE, LANE), F32)],
        compiler_params=_cparams(("arbitrary",)),
        name="outproj_route",
    )(do, go, hs, mo, x, mod_l, n2g, hn, w_out_l, wr_hi, wr_lo, rb, head_mat)


DMA_UNROLL = 8


def _for_rows(n, fn):
    groups = lax.shift_right_logical(n, int(math.log2(DMA_UNROLL)))

    def group(g, _):
        for u in range(DMA_UNROLL):
            fn(g * DMA_UNROLL + u, u % 2)
        return 0

    def single(r, _):
        fn(r, 0)
        return 0

    lax.fori_loop(0, groups, group, 0)
    lax.fori_loop(groups * DMA_UNROLL, n, single, 0)


def _expert_kernel(be_ref, nv_ref, src_ref, srcn_ref, dst_ref, f_hbm, w1_ref, w3_ref, w2_ref, y_hbm,
                   xbuf, ybuf, w1s, w3s, w2s, gsem, ssem):
    i = pl.program_id(0)
    nv = nv_ref[i]
    nv_next = nv_ref[i + 1]
    slot = i % 2

    def tile(r):
        return pl.ds(pl.multiple_of(r * ROW_TILES, ROW_TILES), ROW_TILES)

    def gather_copy(idx_ref, s, r):
        return pltpu.make_async_copy(f_hbm.at[tile(idx_ref[0, 0, r])], xbuf.at[s, tile(r)], gsem.at[s])

    def scatter_copy(r):
        return pltpu.make_async_copy(ybuf.at[tile(r)], y_hbm.at[tile(dst_ref[0, 0, r])], ssem)

    def rows(n):
        return pl.ds(0, pl.multiple_of(n * ROW_TILES, ROW_TILES))

    def gather_wait(n):
        pltpu.make_async_copy(f_hbm.at[rows(n)], xbuf.at[slot, rows(n)], gsem.at[slot]).wait()

    def scatter_wait(n):
        pltpu.make_async_copy(ybuf.at[rows(n)], y_hbm.at[rows(n)], ssem).wait()

    @pl.when(i == 0)
    def _():
        xbuf[...] = jnp.zeros_like(xbuf)
        _for_rows(nv, lambda r, pr: gather_copy(src_ref, 0, r).start(priority=pr))

    @pl.when(nv_next > 0)
    def _():
        _for_rows(nv_next, lambda r, pr: gather_copy(srcn_ref, 1 - slot, r).start(priority=pr))

    @pl.when((nv > 0) & ((i == 0) | (be_ref[i] != be_ref[jnp.maximum(i - 1, 0)])))
    def _():
        w1s[...] = w1_ref[0, 0].astype(BF16)
        w3s[...] = w3_ref[0, 0].astype(BF16)
        w2s[...] = w2_ref[0, 0].astype(BF16)

    @pl.when(nv > 0)
    def _():
        gather_wait(nv)
        xb = _load_row_tiles(xbuf, (slot,), MOE_TILE).astype(BF16)
        hh = _silu(_dot(xb, w1s[...])) * _dot(xb, w3s[...])
        y = _dot(hh.astype(BF16), w2s[...])

        @pl.when(i > 0)
        def _():
            scatter_wait(nv_ref[jnp.maximum(i - 1, 0)])

        _store_row_tiles(ybuf, (), MOE_TILE, y)
        _for_rows(nv, lambda r, pr: scatter_copy(r).start(priority=pr))

        @pl.when(nv_next == 0)
        def _():
            scatter_wait(nv)


def _experts(f, row_src, row_dst, blk_expert, blk_valid, w1, w3, w2, layer, n_tok):
    n_blk = row_src.shape[0]
    idx_spec = lambda fn: pl.BlockSpec((1, 1, MOE_TILE), fn, memory_space=pltpu.SMEM)
    w_spec = lambda shape: pl.BlockSpec((1, 1) + shape, lambda i, be, nv: (layer, be[i], 0, 0))
    grid_spec = pltpu.PrefetchScalarGridSpec(
        num_scalar_prefetch=2,
        grid=(n_blk,),
        in_specs=[idx_spec(lambda i, be, nv: (i, 0, 0)),
                  idx_spec(lambda i, be, nv: (jnp.minimum(i + 1, n_blk - 1), 0, 0)),
                  idx_spec(lambda i, be, nv: (i, 0, 0)),
                  pl.BlockSpec(memory_space=pl.ANY),
                  w_spec((D_MODEL, MOE_HIDDEN)), w_spec((D_MODEL, MOE_HIDDEN)), w_spec((MOE_HIDDEN, D_MODEL))],
        out_specs=pl.BlockSpec(memory_space=pl.ANY),
        scratch_shapes=[pltpu.VMEM((2, MOE_TILE * ROW_TILES, LANE), F32), pltpu.VMEM((MOE_TILE * ROW_TILES, LANE), F32),
                        pltpu.VMEM((D_MODEL, MOE_HIDDEN), BF16), pltpu.VMEM((D_MODEL, MOE_HIDDEN), BF16),
                        pltpu.VMEM((MOE_HIDDEN, D_MODEL), BF16),
                        pltpu.SemaphoreType.DMA((2,)), pltpu.SemaphoreType.DMA(())],
    )
    return pl.pallas_call(
        _expert_kernel,
        grid_spec=grid_spec,
        out_shape=jax.ShapeDtypeStruct((2 * n_tok * ROW_TILES, LANE), F32),
        compiler_params=_cparams(("arbitrary",)),
        name="experts",
    )(blk_expert, blk_valid, row_src, row_src, row_dst, f, w1, w3, w2).reshape(2, n_tok * ROW_TILES, LANE)


def _dest_kernel(rt_ref, ps_ref, o_ref):
    rt = rt_ref[...].astype(I32)
    ps = ps_ref[...]
    row = lax.broadcasted_iota(I32, (LANE, rt.shape[1]), 0)

    def dest(k):
        start = jnp.sum(jnp.where(row == rt[k:k + 1, :], ps, 0.0), axis=0, keepdims=True)
        return start.astype(I32) + rt[2 + k:3 + k, :]

    orow = lax.broadcasted_iota(I32, rt.shape, 0)
    o_ref[...] = jnp.where(orow == 0, dest(0), jnp.where(orow == 1, dest(1), 0))


def _pair_dest(route_t, pad_start, n_tok):
    ps = jnp.pad(pad_start.astype(F32), (0, LANE - MOE_EXPERTS)).reshape(LANE, 1)
    cols = next(r for r in (2048, 1024, 512, TILE) if n_tok % r == 0)
    return pl.pallas_call(
        _dest_kernel,
        grid=(n_tok // cols,),
        in_specs=[pl.BlockSpec((8, cols), lambda i: (0, i)), pl.BlockSpec((LANE, 1), lambda i: (0, 0))],
        out_specs=pl.BlockSpec((8, cols), lambda i: (0, i)),
        out_shape=jax.ShapeDtypeStruct((8, n_tok), I32),
        compiler_params=_cparams(("arbitrary",)),
        name="pair_dest",
    )(route_t, ps)


INV_UNROLL = 8


def _inverse_kernel(n_pairs, dest_ref, lo_ref, hi_ref, out_ref):
    def body(g, _):
        for u in range(INV_UNROLL):
            p = g * INV_UNROLL + u
            out_ref[dest_ref[p]] = p
        return 0

    lax.fori_loop(0, n_pairs // INV_UNROLL, body, 0)

    def fill_segment(e, _):
        def fill(r, _):
            out_ref[r] = 0
            return 0
        return lax.fori_loop(lo_ref[e], hi_ref[e], fill, 0)

    lax.fori_loop(0, MOE_EXPERTS + 1, fill_segment, 0)


def _inverse_rows(dest_flat, fill_lo, fill_hi, p_rows):
    n_pairs = dest_flat.shape[0]
    smem = pl.BlockSpec(memory_space=pltpu.SMEM)
    return pl.pallas_call(
        functools.partial(_inverse_kernel, n_pairs),
        in_specs=[smem, smem, smem],
        out_specs=smem,
        out_shape=jax.ShapeDtypeStruct((p_rows,), I32),
        name="inverse_rows",
    )(dest_flat, fill_lo, fill_hi)


def _route_plan(route_t, counts, n_tok):
    n_blk = (2 * n_tok) // MOE_TILE + MOE_EXPERTS
    p_rows = n_blk * MOE_TILE
    cnt = counts[MOE_GROUPS:MOE_GROUPS + MOE_EXPERTS, 0].astype(I32)
    padded = (cnt + MOE_TILE - 1) // MOE_TILE * MOE_TILE
    pad_end = jnp.cumsum(padded)
    pad_start = pad_end - padded
    dest = _pair_dest(route_t, pad_start, n_tok)[0:2].reshape(-1)
    fill_lo = jnp.concatenate([pad_start + cnt, pad_end[-1:]]).astype(I32)
    fill_hi = jnp.concatenate([pad_end, jnp.full((1,), p_rows, I32)]).astype(I32)
    row_dst = _inverse_rows(dest, fill_lo, fill_hi, p_rows)
    row_src = jnp.where(row_dst >= n_tok, row_dst - n_tok, row_dst)
    blk_start = (jnp.arange(n_blk + 1, dtype=I32) * MOE_TILE)[:, None]
    blk_expert = jnp.minimum(jnp.sum(pad_end[None, :] <= blk_start, axis=-1), MOE_EXPERTS - 1).astype(I32)
    inside = (pad_start[None, :] <= blk_start) & (blk_start < pad_end[None, :])
    blk_valid = jnp.sum(jnp.where(inside, jnp.clip((pad_start + cnt)[None, :] - blk_start, 0, MOE_TILE), 0), axis=-1).astype(I32)
    return (row_src.reshape(n_blk, 1, MOE_TILE), row_dst.reshape(n_blk, 1, MOE_TILE), blk_expert[:n_blk], blk_valid)


def _final_kernel(x_ref, y_ref, rg_ref, mod_ref, o_ref):
    g2 = mod_ref[0, :, 5 * D_MODEL:6 * D_MODEL]
    rg = rg_ref[...]
    o_ref[...] = x_ref[...] + g2 * (rg[:, 4:5] * _load_row_tiles(y_ref, (0,), TILE) + rg[:, 5:6] * _load_row_tiles(y_ref, (1,), TILE))


def _final(x, y_flat, rg, mod_l, n_tok):
    nq = SEQ // TILE
    tile_map = lambda i: (i, 0)
    return pl.pallas_call(
        _final_kernel,
        grid=(n_tok // TILE,),
        in_specs=[pl.BlockSpec((TILE, D_MODEL), tile_map), pl.BlockSpec((2, TILE * ROW_TILES, LANE), lambda i: (0, i, 0)),
                  pl.BlockSpec((TILE, LANE), tile_map), pl.BlockSpec((1, 1, 6 * D_MODEL), lambda i: (i // nq, 0, 0))],
        out_specs=pl.BlockSpec((TILE, D_MODEL), tile_map),
        out_shape=jax.ShapeDtypeStruct((n_tok, D_MODEL), F32),
        compiler_params=_cparams(("arbitrary",)),
        name="final_residual",
    )(x, y_flat, rg, mod_l)


def _diff_rows(w, heads):
    nl, _, k = w.shape
    w = w.reshape(nl, heads, 2, 2, 2, 12, k)
    w = w.transpose(0, 1, 4, 2, 3, 5, 6)
    w = jnp.pad(w.reshape(nl, heads, 4, 24, k), ((0, 0), (0, 0), (0, 0), (0, 8), (0, 0)))
    return w.reshape(nl, heads * LANE, k)


def _gqa_rows(w, groups):
    nl, _, k = w.shape
    w = w.reshape(nl, 2, groups, 2, 2, 16, k)
    w = w.transpose(0, 2, 4, 1, 3, 5, 6)
    return w.reshape(nl, groups * LANE, k)


def _in_weight(w_in):
    wt = jnp.swapaxes(w_in, 1, 2)
    nl = wt.shape[0]
    dv = jnp.pad(wt[:, 768:1152].reshape(nl, DIFF_HEADS, DIFF_V_DIM, D_MODEL), ((0, 0), (0, 0), (0, LANE - DIFF_V_DIM), (0, 0)))
    parts = [_diff_rows(wt[:, 0:384], DIFF_HEADS), _diff_rows(wt[:, 384:768], DIFF_HEADS),
             dv.reshape(nl, DIFF_HEADS * LANE, D_MODEL), _gqa_rows(wt[:, 1152:1536], 3), _gqa_rows(wt[:, 1536:1664], 1),
             wt[:, 1664:2816], jnp.pad(wt[:, 2816:2832], ((0, 0), (0, LANE - 16), (0, 0)))]
    return jnp.concatenate(parts, axis=1).astype(BF16)


def _lane_gain(g, rows_fn):
    return rows_fn(jnp.concatenate([g, g], axis=-1)[:, :, None], 1)[:, :, 0]


def _out_weight(w_out):
    nl = w_out.shape[0]
    diff = jnp.pad(w_out[:, 0:384].reshape(nl, DIFF_HEADS, DIFF_V_DIM, D_MODEL), ((0, 0), (0, 0), (0, LANE - DIFF_V_DIM), (0, 0)))
    gqa = w_out[:, 384:768].reshape(nl, 2, 3, GQA_HEAD_DIM, D_MODEL).transpose(0, 2, 1, 3, 4)
    return jnp.concatenate([diff.reshape(nl, 512, D_MODEL), gqa.reshape(nl, 384, D_MODEL), w_out[:, 768:]], axis=1).astype(BF16)


def kernel(x, c, ctx, c_ctx, norm1_g, norm2_g, w_mod, b_mod, w_in, w_out, diff_q_norm, diff_k_norm, diff_lambda, diff_subln, gqa_q_norm, gqa_k_norm, mlstm_conv_w, mlstm_conv_b, mlstm_gate_b, mlstm_head_norm, moe_wg, moe_bg, moe_we, moe_be, moe_w1, moe_w3, moe_w2):
    B = x.shape[0]
    nt = B * SEG_ROWS
    n_lat = B * SEQ

    w_in_r = _in_weight(w_in)
    w_out_r = _out_weight(w_out)
    gq_d = _lane_gain(diff_q_norm, _diff_rows) * (LOG2E * DIFF_HEAD_DIM ** -0.5)
    gk_d = _lane_gain(diff_k_norm, _diff_rows)
    gq_g = _lane_gain(gqa_q_norm, _gqa_rows) * (LOG2E * GQA_HEAD_DIM ** -0.5)
    gk_g = _lane_gain(gqa_k_norm, _gqa_rows)
    bound_d = _score_bound(diff_q_norm, diff_k_norm, DIFF_HEAD_DIM).reshape(DEPTH, 1)
    bound_g = _score_bound(gqa_q_norm, gqa_k_norm, GQA_HEAD_DIM).reshape(DEPTH, 1)
    gains = jnp.concatenate([jnp.tile(gq_d, (1, 4)), jnp.tile(gk_d, (1, 4)), jnp.tile(gq_g, (1, 3)), gk_g], axis=1)
    gains = gains.reshape(DEPTH, 1, 1536)
    gate_b = jnp.pad(mlstm_gate_b, ((0, 0), (0, LANE - 16))).reshape(DEPTH, 1, LANE)
    subln = jnp.pad(diff_subln, ((0, 0), (0, LANE - DIFF_V_DIM))).reshape(DEPTH, 1, LANE)
    lam_pad = jnp.pad(diff_lambda, ((0, 0), (0, 4), (0, LANE - DIFF_HEAD_DIM)))
    conv_w = jnp.pad(mlstm_conv_w, ((0, 0), (0, 5), (0, 0)))
    conv_b = mlstm_conv_b.reshape(DEPTH, 1, 512)
    w_r = jnp.swapaxes(jnp.pad(jnp.concatenate([moe_wg, moe_we], axis=2), ((0, 0), (0, 0), (0, LANE - 36))), 1, 2)
    wr_hi = w_r.astype(BF16)
    wr_lo = (w_r - wr_hi.astype(F32)).astype(BF16)
    rb = jnp.pad(jnp.concatenate([moe_bg, moe_be], axis=1), ((0, 0), (0, LANE - 36))).reshape(DEPTH, LANE, 1)
    pair_mat = jnp.asarray(_PAIR_MAT, BF16)
    head_mat = jnp.asarray((np.arange(256)[:, None] // 64 == np.arange(256)[None, :] // 64).astype(np.float32), BF16)
    tabs = _rope_tables(12, 24) + _rope_tables(16, 32)

    xa = jnp.concatenate([ctx, x], axis=1).reshape(nt, D_MODEL)
    assert B <= 8, "row 8 of the modulation table is reserved for the context conditioning"
    cvec = jnp.concatenate([c, jnp.zeros((8 - B, D_MODEL), F32), c_ctx[None, :], jnp.zeros((7, D_MODEL), F32)], axis=0)
    mod = _modulation(cvec, w_mod, b_mod).reshape(DEPTH, 16, 1, 6 * D_MODEL)

    moe_in = None
    for l in range(DEPTH):
        last = l == DEPTH - 1
        lam_init = 0.8 - 0.6 * math.exp(-0.3 * l)
        xa, (dq, dk, dv, gq, gk, gv, mqk, mv, mo, mg) = _inproj(
            xa, moe_in, mod[l], norm1_g[l].reshape(1, D_MODEL), w_in_r[l], gains[l], gate_b[l], tabs, pair_mat, nt)
        do = _diff_attention(bound_d[l], dq, dk, dv, lam_pad[l], subln[l], lam_init, nt, with_ctx=not last)
        go = _gqa_attention(bound_g[l], gq, gk, gv, nt, with_ctx=not last)
        mgt = mg[:, :16].reshape(B, N_CHUNKS, MLSTM_CHUNK, 16).transpose(0, 1, 3, 2)
        hs = _mlstm(mqk, mv, mg, mgt, conv_w[l], conv_b[l], nt)
        xm, f, route, route_t, counts = _outproj(do, go, hs, mo, xa, mod[l], norm2_g[l].reshape(1, D_MODEL),
                                         mlstm_head_norm[l].reshape(1, 256), w_out_r[l], wr_hi[l], wr_lo[l], rb[l],
                                         head_mat, nt, with_ctx=not last)
        n_tok = n_lat if last else nt
        row_src, row_dst, blk_expert, blk_valid = _route_plan(route_t, counts, n_tok)
        y = _experts(f, row_src, row_dst, blk_expert, blk_valid, moe_w1, moe_w3, moe_w2, l, n_tok)
        if last:
            out = _final(xm, y, route, mod[l][:B], n_tok)
            return out.reshape(B, SEQ, D_MODEL)
        xa = xm
        moe_in = (y, route, mod[l])
```

```python
import functools
import math

import numpy as np
import jax
import jax.numpy as jnp
from jax import lax
from jax.experimental import pallas as pl
from jax.experimental.pallas import tpu as pltpu

F32 = jnp.float32
BF16 = jnp.bfloat16
I32 = jnp.int32

D_MODEL = 1024
DEPTH = 4
GRID_W = 64
CTX_LEN = 256
SEQ = 2048
ROPE_THETA = 10000.0
EPS = 1e-6

DIFF_HEADS = 4
DIFF_HEAD_DIM = 48
DIFF_V_DIM = 96
GQA_Q_HEADS = 6
GQA_KV_HEADS = 2
GQA_HEAD_DIM = 64
MLSTM_HEADS = 4
MLSTM_HEAD_DIM = 64
MLSTM_CHUNK = 64
IN_WIDTH = 2832
MOE_GROUPS = 4
MOE_EPG = 8
MOE_EXPERTS = 32
MOE_HIDDEN = 512

LANE = 128
TILE = 256
SEG_TILES = (CTX_LEN + SEQ) // TILE
SEG_ROWS = CTX_LEN + SEQ
N_CHUNKS = SEG_ROWS // MLSTM_CHUNK
CTX_CHUNKS = CTX_LEN // MLSTM_CHUNK
MOE_TILE = 256
VMEM_LIMIT = 56 * 1024 * 1024

C_DQ, C_DK, C_DV, C_GQ, C_GK, C_GV, C_MQK, C_MV, C_MO, C_MG, C_END = (
    0, 512, 1024, 1536, 1920, 2048, 2176, 2688, 2944, 3200, 3328)
MIX_ROWS = 512 + 384 + 256


def _cparams(sem):
    return pltpu.CompilerParams(dimension_semantics=sem, vmem_limit_bytes=VMEM_LIMIT)


_LANE2 = np.arange(2 * LANE)
_PAIR_MAT = ((_LANE2[:, None] // LANE == _LANE2[None, :] // LANE)
             & ((_LANE2[:, None] // 32) % 2 == (_LANE2[None, :] // 32) % 2)).astype(np.float32)


def _rope_tables(nf, pad_from):
    t = jnp.arange(SEQ, dtype=I32)
    rows = (t // GRID_W).astype(F32)
    cols = (t % GRID_W).astype(F32)
    freqs = ROPE_THETA ** (-jnp.arange(nf, dtype=F32) / nf)
    lane = np.arange(LANE)
    i = lane % 32
    typ = (lane // 32) // 2
    use_rows = i < nf
    fidx = np.where(use_rows, i, i - nf)
    valid = i < pad_from
    fidx = np.where(valid, fidx, 0)
    ang = jnp.where(jnp.asarray(use_rows)[None, :], rows[:, None], cols[:, None]) * freqs[jnp.asarray(fidx)][None, :]
    cos = jnp.where(jnp.asarray(valid)[None, :], jnp.cos(ang), 1.0)
    sin = jnp.where(jnp.asarray(valid)[None, :], jnp.sin(ang), 0.0)
    sin = sin * jnp.asarray(np.where(typ == 0, -1.0, 1.0), F32)[None, :]
    cos = jnp.concatenate([jnp.ones((CTX_LEN, LANE), F32), cos], axis=0)
    sin = jnp.concatenate([jnp.zeros((CTX_LEN, LANE), F32), sin], axis=0)
    return cos, sin


def _dot(a, b):
    return jnp.dot(a, b, preferred_element_type=F32)


def _dot_nt(a, b):
    return lax.dot_general(a, b, (((1,), (1,)), ((), ())), preferred_element_type=F32)


def _dot_tn(a, b):
    return lax.dot_general(a, b, (((0,), (0,)), ((), ())), preferred_element_type=F32)


def _split3(x):
    x1 = x.astype(BF16)
    r1 = x - x1.astype(F32)
    x2 = r1.astype(BF16)
    x3 = (r1 - x2.astype(F32)).astype(BF16)
    return x1, x2, x3


def _dot_f32_by_exact(x, m):
    x1, x2, x3 = _split3(x)
    return _dot(x1, m) + _dot(x2, m) + _dot(x3, m)


def _dot_f32_by_exact2(x, m):
    x1 = x.astype(BF16)
    x2 = (x - x1.astype(F32)).astype(BF16)
    return _dot(x1, m) + _dot(x2, m)


def _exact_by_dot_f32(m, x):
    x1, x2, x3 = _split3(x)
    return _dot(m, x1) + _dot(m, x2) + _dot(m, x3)


def _sigmoid(x):
    return 1.0 / (1.0 + jnp.exp(-x))


def _silu(x):
    return x * _sigmoid(x)


def _log_sigmoid(x):
    return jnp.minimum(x, 0.0) - jnp.log1p(jnp.exp(-jnp.abs(x)))


ROW_TILES = D_MODEL // LANE


def _store_row_tiles(ref, lead, rows, x):
    for c in range(ROW_TILES):
        ref[lead + (pl.ds(c, rows, stride=ROW_TILES), slice(None))] = x[:, c * LANE:(c + 1) * LANE]


def _load_row_tiles(ref, lead, rows):
    return jnp.concatenate([ref[lead + (pl.ds(c, rows, stride=ROW_TILES), slice(None))] for c in range(ROW_TILES)], axis=1)


def _seg_tile(i, nq, off):
    return (i // nq) * SEG_TILES + off + i % nq


def _mod_row(i, nq, off):
    return jnp.where((off + i % nq) == 0, 8, i // nq)


MOD_BN = 1536


def _mod_kernel(c_ref, w_ref, b_ref, o_ref):
    a = _silu(c_ref[...]).astype(BF16)
    o_ref[0] = _dot(a, w_ref[0].astype(BF16)) + b_ref[0]


def _modulation(cvec, w_mod, b_mod):
    nb = 6 * D_MODEL // MOD_BN
    return pl.pallas_call(
        _mod_kernel,
        grid=(DEPTH, nb),
        in_specs=[pl.BlockSpec((16, D_MODEL), lambda l, n: (0, 0)),
                  pl.BlockSpec((1, D_MODEL, MOD_BN), lambda l, n: (l, 0, n)),
                  pl.BlockSpec((1, 1, MOD_BN), lambda l, n: (l, 0, n))],
        out_specs=pl.BlockSpec((1, 16, MOD_BN), lambda l, n: (l, 0, n)),
        out_shape=jax.ShapeDtypeStruct((DEPTH, 16, 6 * D_MODEL), F32),
        compiler_params=_cparams(("arbitrary", "arbitrary")),
        name="modulation",
    )(cvec, w_mod, b_mod.reshape(DEPTH, 1, 6 * D_MODEL))


def _inproj_kernel(has_moe, *refs):
    if has_moe:
        (x_ref, y_ref, rg_ref, modp_ref, mod_ref, n1_ref, w_ref, gains_ref, gb_ref, cd_ref, sd_ref, cg_ref, sg_ref,
         pm_ref, xo_ref, dq_ref, dk_ref, dv_ref, gq_ref, gk_ref, gv_ref, mqk_ref, mv_ref, mo_ref, mg_ref) = refs
    else:
        (x_ref, mod_ref, n1_ref, w_ref, gains_ref, gb_ref, cd_ref, sd_ref, cg_ref, sg_ref,
         pm_ref, dq_ref, dk_ref, dv_ref, gq_ref, gk_ref, gv_ref, mqk_ref, mv_ref, mo_ref, mg_ref) = refs
    x = x_ref[...]
    if has_moe:
        g2 = modp_ref[0, :, 5 * D_MODEL:6 * D_MODEL]
        rg = rg_ref[...]
        x = x + g2 * (rg[:, 4:5] * _load_row_tiles(y_ref, (0,), TILE) + rg[:, 5:6] * _load_row_tiles(y_ref, (1,), TILE))
        xo_ref[...] = x
    sh = mod_ref[0, :, 0:D_MODEL]
    sc = mod_ref[0, :, D_MODEL:2 * D_MODEL]
    xn = x * lax.rsqrt(jnp.mean(x * x, axis=-1, keepdims=True) + EPS) * n1_ref[...]
    h = (xn * (1.0 + sc) + sh).astype(BF16)
    pm = pm_ref[...]

    def proj(a, b):
        return _dot_nt(h, w_ref[a:b, :])

    def qk_group(col, gain_col, inv_dim, cos, sin, outs):
        y = proj(col, col + 512)
        for half in range(2):
            yh = y[:, half * 256:(half + 1) * 256]
            yn = yh * lax.rsqrt(_dot_f32_by_exact2(yh * yh, pm) * inv_dim + EPS)
            yn = yn * gains_ref[:, gain_col + half * 256:gain_col + (half + 1) * 256]
            for j in range(2):
                yb = yn[:, j * LANE:(j + 1) * LANE]
                ref, off = outs[half * 2 + j]
                ref[:, off:off + LANE] = (yb * cos + pltpu.roll(yb, 64, 1) * sin).astype(BF16)

    cd, sd, cg, sg = cd_ref[...], sd_ref[...], cg_ref[...], sg_ref[...]
    qk_group(C_DQ, 0, 1.0 / DIFF_HEAD_DIM, cd, sd, [(dq_ref, b * LANE) for b in range(4)])
    qk_group(C_DK, 512, 1.0 / DIFF_HEAD_DIM, cd, sd, [(dk_ref, b * LANE) for b in range(4)])
    qk_group(C_GQ, 1024, 1.0 / GQA_HEAD_DIM, cg, sg, [(gq_ref, 0), (gq_ref, LANE), (gq_ref, 2 * LANE), (gk_ref, 0)])
    dv_ref[...] = proj(C_DV, C_GQ).astype(BF16)
    rest = proj(C_GV, C_END)
    gv_ref[...] = rest[:, 0:C_MQK - C_GV].astype(BF16)
    mqk_ref[...] = rest[:, C_MQK - C_GV:C_MV - C_GV]
    mv_ref[...] = rest[:, C_MV - C_GV:C_MO - C_GV].astype(BF16)
    mo_ref[...] = rest[:, C_MO - C_GV:C_MG - C_GV]
    mg_ref[...] = rest[:, C_MG - C_GV:C_END - C_GV] + gb_ref[...]


def _inproj(x, moe_in, mod_l, n1g, w_in_l, gains, gate_b, tabs, pair_mat, nt):
    has_moe = moe_in is not None
    n_tiles = nt // TILE
    tile_map = lambda i: (i, 0)
    mod_spec = pl.BlockSpec((1, 1, 6 * D_MODEL), lambda i: (_mod_row(i, SEG_TILES, 0), 0, 0))
    tab_spec = pl.BlockSpec((TILE, LANE), lambda i: (i % SEG_TILES, 0))

    def full(shape):
        return pl.BlockSpec(shape, lambda i: (0,) * len(shape))

    in_specs = [pl.BlockSpec((TILE, D_MODEL), tile_map)]
    args = [x]
    if has_moe:
        y_flat, rg, mod_prev = moe_in
        in_specs += [pl.BlockSpec((2, TILE * ROW_TILES, LANE), lambda i: (0, i, 0)), pl.BlockSpec((TILE, LANE), tile_map), mod_spec]
        args += [y_flat, rg, mod_prev]
    in_specs += [mod_spec, full((1, D_MODEL)), full((C_END, D_MODEL)), full((1, 1536)), full((1, LANE)),
                 tab_spec, tab_spec, tab_spec, tab_spec, full((2 * LANE, 2 * LANE))]
    args += [mod_l, n1g, w_in_l, gains, gate_b, tabs[0], tabs[1], tabs[2], tabs[3], pair_mat]

    def o(width, dtype):
        return pl.BlockSpec((TILE, width), tile_map), jax.ShapeDtypeStruct((nt, width), dtype)

    outs = []
    if has_moe:
        outs.append(o(D_MODEL, F32))
    outs += [o(512, BF16), o(512, BF16), o(512, BF16), o(384, BF16), o(LANE, BF16), o(LANE, BF16),
             o(512, F32), o(256, BF16), o(256, F32), o(LANE, F32)]
    res = pl.pallas_call(
        functools.partial(_inproj_kernel, has_moe),
        grid=(n_tiles,),
        in_specs=in_specs,
        out_specs=[s for s, _ in outs],
        out_shape=[s for _, s in outs],
        compiler_params=_cparams(("arbitrary",)),
        name="inproj_moe" if has_moe else "inproj",
    )(*args)
    if has_moe:
        return res[0], res[1:]
    return x, res


def _lambda_value(lam_ref, lam_init):
    lam = lam_ref[...]
    s01 = jnp.sum(lam[0:1] * lam[1:2], axis=-1, keepdims=True)
    s23 = jnp.sum(lam[2:3] * lam[3:4], axis=-1, keepdims=True)
    return jnp.exp(s01) - jnp.exp(s23) + lam_init


LOG2E = 1.4426950408889634
SAFE_LOG2_RANGE = 60.0


def _exp_scores(s, stabilise):
    if stabilise:
        s = s - jnp.max(s, axis=-1, keepdims=True)
    return jnp.exp2(s).astype(BF16)


def _pv(q, kk, kmask, vv, stabilise):
    return _dot(_exp_scores(_dot_nt(q, kk * kmask), stabilise), vv)


def _score_bound(q_gain, k_gain, head_dim):
    return (1.02 * LOG2E * math.sqrt(head_dim)) * jnp.max(jnp.abs(q_gain), axis=-1) * jnp.max(jnp.abs(k_gain), axis=-1)


def _attn_branches(with_ctx, bound_ref, run):
    def on_keys(rows):
        small = bound_ref[0] <= SAFE_LOG2_RANGE

        @pl.when(small)
        def _():
            run(rows, False)

        @pl.when(jnp.logical_not(small))
        def _():
            run(rows, True)

    if not with_ctx:
        on_keys(SEG_ROWS)
        return
    t = pl.program_id(2)

    @pl.when(t == 0)
    def _():
        on_keys(TILE)

    @pl.when(t > 0)
    def _():
        on_keys(SEG_ROWS)


def _attn_maps(with_ctx):
    if with_ctx:
        m = lambda b, h, t: (b * SEG_TILES + t, h)
        return SEG_TILES, m, m
    nq = SEG_TILES - 1
    return nq, (lambda b, h, t: (b * SEG_TILES + 1 + t, h)), (lambda b, h, t: (b * nq + t, h))


def _lane_masks():
    lane = np.arange(LANE)
    even = (lane // 32) % 2 == 0
    rows = [even, ~even, lane < 64, lane >= 64, lane == 64, lane == 0, lane == DIFF_V_DIM, lane < 0]
    return jnp.asarray(np.stack(rows).astype(np.float32))


def _mask_row(lm_ref, r):
    return lm_ref[r:r + 1, :].astype(BF16)


DIFF_HEADS_PER_STEP = 4


def _diff_attn_kernel(lam_init, with_ctx, bound_ref, q_ref, k_ref, v_ref, lam_ref, sg_ref, lm_ref, o_ref):
    def run(rows, stabilise):
        lam = _lambda_value(lam_ref, lam_init)
        for hh in range(DIFF_HEADS_PER_STEP):
            lanes = slice(hh * LANE, (hh + 1) * LANE)
            q = q_ref[:, lanes]
            kk = k_ref[0:rows, lanes]
            v1 = v_ref[0:rows, lanes] + _mask_row(lm_ref, 6)
            o1 = _pv(q, kk, _mask_row(lm_ref, 0), v1, stabilise)
            o2 = _pv(q, kk, _mask_row(lm_ref, 1), v1, stabilise)
            o = o1 * (1.0 / o1[:, DIFF_V_DIM:DIFF_V_DIM + 1]) - o2 * (lam / o2[:, DIFF_V_DIM:DIFF_V_DIM + 1])
            o = jnp.where(lax.broadcasted_iota(I32, o.shape, 1) < DIFF_V_DIM, o, 0.0)
            ms = jnp.sum(o * o, axis=-1, keepdims=True) * (1.0 / DIFF_V_DIM)
            o_ref[:, lanes] = (o * lax.rsqrt(ms + EPS) * sg_ref[...] * (1.0 - lam_init)).astype(BF16)

    _attn_branches(with_ctx, bound_ref, run)


def _diff_attention(bound, dq, dk, dv, lam_pad, subln, lam_init, nt, with_ctx):
    nb = nt // SEG_ROWS
    nq, q_map, o_map = _attn_maps(with_ctx)
    kv_map = lambda b, h, t: (b, h)
    const = lambda b, h, t: (0, 0)
    width = DIFF_HEADS_PER_STEP * LANE
    return pl.pallas_call(
        functools.partial(_diff_attn_kernel, lam_init, with_ctx),
        grid=(nb, DIFF_HEADS // DIFF_HEADS_PER_STEP, nq),
        in_specs=[pl.BlockSpec(memory_space=pltpu.SMEM),
                  pl.BlockSpec((TILE, width), q_map), pl.BlockSpec((SEG_ROWS, width), kv_map),
                  pl.BlockSpec((SEG_ROWS, width), kv_map), pl.BlockSpec((8, LANE), const),
                  pl.BlockSpec((1, LANE), const), pl.BlockSpec((8, LANE), const)],
        out_specs=pl.BlockSpec((TILE, width), o_map),
        out_shape=jax.ShapeDtypeStruct((nb * nq * TILE, 512), BF16),
        compiler_params=_cparams(("arbitrary", "arbitrary", "arbitrary")),
        name="diff_attn",
    )(bound, dq, dk, dv, lam_pad, subln, _lane_masks())


GQA_PAIRS = GQA_Q_HEADS // 2


def _gqa_attn_kernel(with_ctx, bound_ref, q_ref, k_ref, v_ref, lm_ref, o_ref):
    def run(rows, stabilise):
        kk = k_ref[0:rows, :]
        vv = v_ref[0:rows, :]
        va = vv * _mask_row(lm_ref, 2) + _mask_row(lm_ref, 4)
        vb = vv * _mask_row(lm_ref, 3) + _mask_row(lm_ref, 5)
        for p in range(GQA_PAIRS):
            lanes = slice(p * LANE, (p + 1) * LANE)
            q = q_ref[:, lanes]
            oa = _pv(q, kk, _mask_row(lm_ref, 0), va, stabilise)
            ob = _pv(q, kk, _mask_row(lm_ref, 1), vb, stabilise)
            lane = lax.broadcasted_iota(I32, oa.shape, 1)
            o_ref[:, lanes] = jnp.where(lane < GQA_HEAD_DIM, oa * (1.0 / oa[:, GQA_HEAD_DIM:GQA_HEAD_DIM + 1]),
                                        ob * (1.0 / ob[:, 0:1])).astype(BF16)

    _attn_branches(with_ctx, bound_ref, run)


def _gqa_attention(bound, gq, gk, gv, nt, with_ctx):
    nb = nt // SEG_ROWS
    nq, q_map, o_map = _attn_maps(with_ctx)
    kv_map = lambda b, p, t: (b, 0)
    return pl.pallas_call(
        functools.partial(_gqa_attn_kernel, with_ctx),
        grid=(nb, 1, nq),
        in_specs=[pl.BlockSpec(memory_space=pltpu.SMEM),
                  pl.BlockSpec((TILE, GQA_PAIRS * LANE), q_map), pl.BlockSpec((SEG_ROWS, LANE), kv_map),
                  pl.BlockSpec((SEG_ROWS, LANE), kv_map), pl.BlockSpec((8, LANE), lambda b, p, t: (0, 0))],
        out_specs=pl.BlockSpec((TILE, GQA_PAIRS * LANE), o_map),
        out_shape=jax.ShapeDtypeStruct((nb * nq * TILE, 384), BF16),
        compiler_params=_cparams(("arbitrary", "arbitrary", "arbitrary")),
        name="gqa_attn",
    )(bound, gq, gk, gv, _lane_masks())


def _mlstm_kernel(nbk, mqk_ref, mv_ref, mg_ref, mgt_ref, cw_ref, cb_ref, h_ref, qk_s, bcf_s, bcb_s, brf_s, brb_s):
    L = MLSTM_CHUNK
    h_ref[...] = jnp.zeros_like(h_ref)
    w0, w1, w2, cb = cw_ref[0:1, :], cw_ref[1:2, :], cw_ref[2:3, :], cb_ref[...]
    rid = lax.broadcasted_iota(I32, (TILE, 512), 0)
    kscale = jnp.where(lax.broadcasted_iota(I32, (1, 512), 1) < 256, 1.0, MLSTM_HEAD_DIM ** -0.5)
    zrow = jnp.zeros((1, 512), F32)
    for bc in range(nbk * SEG_TILES):
        c = bc % SEG_TILES
        r0 = bc * TILE
        xc = mqk_ref[r0:r0 + TILE, :]
        prev = zrow if c in (0, 1) else mqk_ref[r0 - 1:r0, :]
        nxt = zrow if c in (0, SEG_TILES - 1) else mqk_ref[r0 + TILE:r0 + TILE + 1, :]
        up = jnp.where(rid == 0, prev, pltpu.roll(xc, 1, 0))
        dn = jnp.where(rid == TILE - 1, nxt, pltpu.roll(xc, TILE - 1, 0))
        y = w0 * up + w1 * xc + w2 * dn + cb
        qk_s[r0:r0 + TILE, :] = (_silu(y) * kscale).astype(BF16)

    ti = lax.broadcasted_iota(I32, (L, L), 0)
    si = lax.broadcasted_iota(I32, (L, L), 1)
    tri_le = (si <= ti)
    m_le = tri_le.astype(BF16)
    m_ge = (si >= ti).astype(BF16)
    row2 = lax.broadcasted_iota(I32, (2 * L, L), 0)
    trow = row2 % L
    scol = lax.broadcasted_iota(I32, (2 * L, L), 1)
    top2 = row2 < L
    mask_f = scol <= trow
    mask_b = scol >= trow
    lane_lo = lax.broadcasted_iota(I32, (L, LANE), 1) < L
    lane_lo256 = lax.broadcasted_iota(I32, (L, 2 * LANE), 1) % LANE < L
    rr = lax.broadcasted_iota(I32, (LANE, LANE), 0)
    cc = lax.broadcasted_iota(I32, (LANE, LANE), 1)
    blockdiag = (rr < L) == (cc < L)
    rows_lo = lax.broadcasted_iota(I32, (LANE, 1), 0) < L
    top_col = lax.broadcasted_iota(I32, (2 * L, 1), 0) < L

    def chain(bb, c, is_fwd, p, gcol, bcol_all, grow, brow_all, state):
        ct, nm, m0, m1 = state
        h0, h1 = 2 * p, 2 * p + 1
        gi, gf = (0, 4) if is_fwd else (8, 12)
        r0 = pl.multiple_of(bb * SEG_ROWS + c * L, L)
        qb = qk_s[pl.ds(r0, L), p * LANE:(p + 1) * LANE]
        kb = qk_s[pl.ds(r0, L), 256 + p * LANE:256 + (p + 1) * LANE]
        v128 = mv_ref[pl.ds(r0, L), p * LANE:(p + 1) * LANE]

        def stack_cols(arr, j0, j1):
            return jnp.concatenate([arr[:, j0:j0 + 1], arr[:, j1:j1 + 1]], axis=0)

        def stack_rows(arr, j0, j1):
            return jnp.where(top2, arr[j0:j0 + 1, :], arr[j1:j1 + 1, :])

        bcol = stack_cols(bcol_all, gf + h0, gf + h1)
        licol = stack_cols(gcol, gi + h0, gi + h1)
        crow = stack_rows(grow, gi + h0, gi + h1) - stack_rows(brow_all, gf + h0, gf + h1)
        cm = jnp.where(mask_f if is_fwd else mask_b, crow, -jnp.inf)
        mcol = jnp.where(top_col, m0, m1)
        u = jnp.maximum(mcol, jnp.max(cm, axis=-1, keepdims=True))
        zq = jnp.zeros_like(qb)
        qstack = jnp.concatenate([jnp.where(lane_lo, qb, zq), jnp.where(lane_lo, zq, qb)], axis=0)
        w = (jnp.exp(cm - u) * _dot_nt(qstack, kb)).astype(BF16)
        wv = _dot(w, jnp.concatenate([v128, jnp.ones_like(v128)], axis=1))
        wv = jnp.where(lane_lo256, wv[:L], wv[L:])
        qcn = _dot(qb, jnp.concatenate([ct, nm], axis=1).astype(BF16))
        a = jnp.exp(mcol - u)
        emt = jnp.exp(-(bcol + u))
        a128 = jnp.where(lane_lo, a[:L], a[L:])
        emt128 = jnp.where(lane_lo, emt[:L], emt[L:])
        num = wv[:, :LANE] + a128 * qcn[:, :LANE]
        den = wv[:, LANE:] + a128 * qcn[:, LANE:]
        h_ref[pl.ds(r0, L), p * LANE:(p + 1) * LANE] += num / jnp.maximum(jnp.abs(den), emt128)
        e0 = (L - 1) if is_fwd else 0
        bend0 = bcol[e0:e0 + 1, :]
        bend1 = bcol[L + e0:L + e0 + 1, :]
        bend = jnp.where(top_col, bend0, bend1)
        g = bend - bcol + licol
        m0n = jnp.maximum(bend0 + m0, jnp.max(g[:L], axis=0, keepdims=True))
        m1n = jnp.maximum(bend1 + m1, jnp.max(g[L:], axis=0, keepdims=True))
        ws = jnp.exp(g - jnp.where(top_col, m0n, m1n))
        ae0 = jnp.exp(bend0 + m0 - m0n)
        ae1 = jnp.exp(bend1 + m1 - m1n)
        ws128 = jnp.where(lane_lo, ws[:L], ws[L:])
        vw = jnp.concatenate([v128.astype(F32) * ws128, ws128], axis=1).astype(BF16)
        upd = _dot_tn(kb, vw)
        ae = jnp.where(rows_lo, ae0, ae1)
        ct_new = ae * ct + jnp.where(blockdiag, upd[:, :LANE], 0.0)
        nm_new = ae * nm + jnp.where(blockdiag, upd[:, LANE:], 0.0)
        return ct_new, nm_new, m0n, m1n

    def gate_sums(j, _):
        for bb in range(nbk):
            r0 = pl.multiple_of(bb * SEG_ROWS + j * L, L)
            lf_col = _log_sigmoid(mg_ref[pl.ds(r0, L), :])
            lf_row = _log_sigmoid(mgt_ref[bb, j])
            bcf_s[pl.ds(r0, L), :] = _exact_by_dot_f32(m_le, lf_col)
            bcb_s[pl.ds(r0, L), :] = _exact_by_dot_f32(m_ge, lf_col)
            brf_s[bb * N_CHUNKS + j] = _dot_f32_by_exact(lf_row, m_ge)
            brb_s[bb * N_CHUNKS + j] = _dot_f32_by_exact(lf_row, m_le)
        return 0

    lax.fori_loop(0, N_CHUNKS, gate_sums, 0, unroll=4)

    def body(i, carry):
        cf = i
        cbk = jnp.where(i < CTX_CHUNKS, CTX_CHUNKS - 1 - i, N_CHUNKS + CTX_CHUNKS - 1 - i)
        new = []
        for bb in range(nbk):
            for d, c in enumerate((cf, cbk)):
                is_fwd = d == 0
                r0 = pl.multiple_of(bb * SEG_ROWS + c * L, L)
                gcol = mg_ref[pl.ds(r0, L), :]
                grow = mgt_ref[bb, c]
                bcol_all = (bcf_s if is_fwd else bcb_s)[pl.ds(r0, L), :]
                brow_all = (brf_s if is_fwd else brb_s)[bb * N_CHUNKS + c]
                for p in range(2):
                    st = carry[(bb * 2 + d) * 2 + p]
                    new.append(chain(bb, c, is_fwd, p, gcol, bcol_all, grow, brow_all, st))
        return tuple(new)

    z = (jnp.zeros((LANE, LANE), F32), jnp.zeros((LANE, LANE), F32), jnp.zeros((1, 1), F32), jnp.zeros((1, 1), F32))
    lax.fori_loop(0, N_CHUNKS, body, (z,) * (4 * nbk), unroll=9)


def _mlstm(mqk, mv, mg, mgt, conv_w, conv_b, nt):
    nb = nt // SEG_ROWS
    nbk = 1
    blk = lambda w: pl.BlockSpec((nbk * SEG_ROWS, w), lambda g: (g, 0))
    return pl.pallas_call(
        functools.partial(_mlstm_kernel, nbk),
        grid=(nb // nbk,),
        in_specs=[blk(512), blk(256), blk(LANE),
                  pl.BlockSpec((nbk, N_CHUNKS, 16, MLSTM_CHUNK), lambda g: (g, 0, 0, 0)),
                  pl.BlockSpec((8, 512), lambda g: (0, 0)), pl.BlockSpec((1, 512), lambda g: (0, 0))],
        out_specs=blk(256),
        out_shape=jax.ShapeDtypeStruct((nt, 256), F32),
        scratch_shapes=[pltpu.VMEM((nbk * SEG_ROWS, 512), BF16), pltpu.VMEM((nbk * SEG_ROWS, LANE), F32),
                        pltpu.VMEM((nbk * SEG_ROWS, LANE), F32), pltpu.VMEM((nbk * N_CHUNKS, 16, MLSTM_CHUNK), F32),
                        pltpu.VMEM((nbk * N_CHUNKS, 16, MLSTM_CHUNK), F32)],
        compiler_params=_cparams(("arbitrary",)),
        name="mlstm",
    )(mqk, mv, mg, mgt, conv_w, conv_b)


def _outproj_kernel(do_ref, go_ref, hs_ref, mo_ref, x_ref, mod_ref, n2_ref, hn_ref, wo_ref, wr_hi_ref, wr_lo_ref,
                    rb_ref, hm_ref, xm_ref, f_ref, rt_ref, rtt_ref, cnt_ref, carry):
    i = pl.program_id(0)

    @pl.when(i == 0)
    def _():
        carry[...] = jnp.zeros_like(carry)

    hsum = hs_ref[...]
    ssq = _dot_f32_by_exact2(hsum * hsum, hm_ref[...])
    ml = hsum * lax.rsqrt(ssq * (1.0 / MLSTM_HEAD_DIM) + EPS) * hn_ref[...] * _sigmoid(mo_ref[...])
    acc = _dot(do_ref[...], wo_ref[0:512, :])
    acc += _dot(go_ref[...], wo_ref[512:896, :])
    acc += _dot(ml.astype(BF16), wo_ref[896:MIX_ROWS, :])
    g1 = mod_ref[0, :, 2 * D_MODEL:3 * D_MODEL]
    sh2 = mod_ref[0, :, 3 * D_MODEL:4 * D_MODEL]
    sc2 = mod_ref[0, :, 4 * D_MODEL:5 * D_MODEL]
    x = x_ref[...] + g1 * acc
    xm_ref[...] = x
    xn = x * lax.rsqrt(jnp.mean(x * x, axis=-1, keepdims=True) + EPS) * n2_ref[...]
    f = xn * (1.0 + sc2) + sh2
    _store_row_tiles(f_ref, (), TILE, f)
    f1, f2, _ = _split3(f)
    lt = _dot_nt(wr_hi_ref[...], f1) + _dot_nt(wr_hi_ref[...], f2) + _dot_nt(wr_lo_ref[...], f1) + rb_ref[...]
    row = lax.broadcasted_iota(I32, lt.shape, 0)
    neg = jnp.float32(-jnp.inf)
    big = jnp.int32(1 << 20)
    lg = jnp.where(row < MOE_GROUPS, lt, neg)
    gmax = jnp.max(lg, axis=0, keepdims=True)
    g_top = 1.0 / jnp.sum(jnp.exp(lg - gmax), axis=0, keepdims=True)
    g_idx = jnp.min(jnp.where(lg == gmax, row, big), axis=0, keepdims=True)
    in_grp = (row >= MOE_GROUPS) & (row < MOE_GROUPS + MOE_EXPERTS) & ((row - MOE_GROUPS) // MOE_EPG == g_idx)
    le = jnp.where(in_grp, lt, neg)
    v1 = jnp.max(le, axis=0, keepdims=True)
    l1 = jnp.min(jnp.where(le == v1, row, big), axis=0, keepdims=True)
    le2 = jnp.where(row == l1, neg, le)
    v2 = jnp.max(le2, axis=0, keepdims=True)
    l2 = jnp.min(jnp.where(le2 == v2, row, big), axis=0, keepdims=True)
    ex = jnp.exp(v2 - v1)
    gate1 = g_top / (1.0 + ex)
    gate2 = gate1 * ex
    oh1 = (row == l1)
    oh2 = (row == l2)
    both = oh1.astype(BF16) + oh2.astype(BF16)
    ri_ = lax.broadcasted_iota(I32, (TILE, TILE), 0)
    ci_ = lax.broadcasted_iota(I32, (TILE, TILE), 1)
    before = _dot(both, (ri_ < ci_).astype(BF16)) + carry[:, 0:1]
    rank1 = jnp.sum(jnp.where(oh1, before, 0.0), axis=0, keepdims=True)
    rank2 = jnp.sum(jnp.where(oh2, before, 0.0), axis=0, keepdims=True)
    carry[...] = carry[...] + jnp.sum(both.astype(F32), axis=1, keepdims=True)
    cnt_ref[...] = carry[...]
    packed = jnp.where(row == 0, (l1 - MOE_GROUPS).astype(F32), jnp.where(row == 1, (l2 - MOE_GROUPS).astype(F32),
             jnp.where(row == 2, rank1, jnp.where(row == 3, rank2, jnp.where(row == 4, gate1, jnp.where(row == 5, gate2, 0.0))))))
    rt_ref[...] = packed.T
    rtt_ref[...] = packed[0:8, :]


def _outproj(do, go, hs, mo, x, mod_l, n2g, hn, w_out_l, wr_hi, wr_lo, rb, head_mat, nt, with_ctx):
    nb = nt // SEG_ROWS
    nq, off = (SEG_TILES, 0) if with_ctx else (SEG_TILES - 1, 1)
    n_steps = nb * nq
    n_out = n_steps * TILE
    in_map = lambda i: (_seg_tile(i, nq, off), 0)
    out_map = lambda i: (i, 0)

    def full(shape):
        return pl.BlockSpec(shape, lambda i: (0,) * len(shape))

    in_specs = [pl.BlockSpec((TILE, 512), out_map), pl.BlockSpec((TILE, 384), out_map), pl.BlockSpec((TILE, 256), in_map),
                pl.BlockSpec((TILE, 256), in_map), pl.BlockSpec((TILE, D_MODEL), in_map),
                pl.BlockSpec((1, 1, 6 * D_MODEL), lambda i: (_mod_row(i, nq, off), 0, 0)),
                full((1, D_MODEL)), full((1, 256)), full((MIX_ROWS, D_MODEL)), full((LANE, D_MODEL)), full((LANE, D_MODEL)),
                full((LANE, 1)), full((256, 256))]
    out_specs = [pl.BlockSpec((TILE, D_MODEL), out_map), pl.BlockSpec((TILE * ROW_TILES, LANE), out_map),
                 pl.BlockSpec((TILE, LANE), out_map), pl.BlockSpec((8, TILE), lambda i: (0, i)),
                 pl.BlockSpec((LANE, LANE), lambda i: (0, 0))]
    out_shape = [jax.ShapeDtypeStruct((n_out, D_MODEL), F32), jax.ShapeDtypeStruct((n_out * ROW_TILES, LANE), F32),
                 jax.ShapeDtypeStruct((n_out, LANE), F32), jax.ShapeDtypeStruct((8, n_out), F32),
                 jax.ShapeDtypeStruct((LANE, LANE), F32)]
    return pl.pallas_call(
        _outproj_kernel,
        grid=(n_steps,),
        in_specs=in_specs,
        out_specs=out_specs,
        out_shape=out_shape,
        scratch_shapes=[pltpu.VMEM((LANE, LANE), F32)],
        compiler_params=_cparams(("arbitrary",)),
        name="outproj_route",
    )(do, go, hs, mo, x, mod_l, n2g, hn, w_out_l, wr_hi, wr_lo, rb, head_mat)


DMA_UNROLL = 8


def _for_rows(n, fn):
    groups = lax.shift_right_logical(n, int(math.log2(DMA_UNROLL)))

    def group(g, _):
        for u in range(DMA_UNROLL):
            fn(g * DMA_UNROLL + u, u % 2)
        return 0

    def single(r, _):
        fn(r, 0)
        return 0

    lax.fori_loop(0, groups, group, 0)
    lax.fori_loop(groups * DMA_UNROLL, n, single, 0)


def _expert_kernel(be_ref, nv_ref, src_ref, srcn_ref, dst_ref, f_hbm, w1_ref, w3_ref, w2_ref, y_hbm,
                   xbuf, ybuf, w1s, w3s, w2s, gsem, ssem):
    i = pl.program_id(0)
    nv = nv_ref[i]
    nv_next = nv_ref[i + 1]
    slot = i % 2

    def tile(r):
        return pl.ds(pl.multiple_of(r * ROW_TILES, ROW_TILES), ROW_TILES)

    def gather_copy(idx_ref, s, r):
        return pltpu.make_async_copy(f_hbm.at[tile(idx_ref[0, 0, r])], xbuf.at[s, tile(r)], gsem.at[s])

    def scatter_copy(r):
        return pltpu.make_async_copy(ybuf.at[tile(r)], y_hbm.at[tile(dst_ref[0, 0, r])], ssem)

    def rows(n):
        return pl.ds(0, pl.multiple_of(n * ROW_TILES, ROW_TILES))

    def gather_wait(n):
        pltpu.make_async_copy(f_hbm.at[rows(n)], xbuf.at[slot, rows(n)], gsem.at[slot]).wait()

    def scatter_wait(n):
        pltpu.make_async_copy(ybuf.at[rows(n)], y_hbm.at[rows(n)], ssem).wait()

    @pl.when(i == 0)
    def _():
        xbuf[...] = jnp.zeros_like(xbuf)
        _for_rows(nv, lambda r, pr: gather_copy(src_ref, 0, r).start(priority=pr))

    @pl.when(nv_next > 0)
    def _():
        _for_rows(nv_next, lambda r, pr: gather_copy(srcn_ref, 1 - slot, r).start(priority=pr))

    @pl.when((nv > 0) & ((i == 0) | (be_ref[i] != be_ref[jnp.maximum(i - 1, 0)])))
    def _():
        w1s[...] = w1_ref[0, 0].astype(BF16)
        w3s[...] = w3_ref[0, 0].astype(BF16)
        w2s[...] = w2_ref[0, 0].astype(BF16)

    @pl.when(nv > 0)
    def _():
        gather_wait(nv)
        xb = _load_row_tiles(xbuf, (slot,), MOE_TILE).astype(BF16)
        hh = _silu(_dot(xb, w1s[...])) * _dot(xb, w3s[...])
        y = _dot(hh.astype(BF16), w2s[...])

        @pl.when(i > 0)
        def _():
            scatter_wait(nv_ref[jnp.maximum(i - 1, 0)])

        _store_row_tiles(ybuf, (), MOE_TILE, y)
        _for_rows(nv, lambda r, pr: scatter_copy(r).start(priority=pr))

        @pl.when(nv_next == 0)
        def _():
            scatter_wait(nv)


def _experts(f, row_src, row_dst, blk_expert, blk_valid, w1, w3, w2, layer, n_tok):
    n_blk = row_src.shape[0]
    idx_spec = lambda fn: pl.BlockSpec((1, 1, MOE_TILE), fn, memory_space=pltpu.SMEM)
    w_spec = lambda shape: pl.BlockSpec((1, 1) + shape, lambda i, be, nv: (layer, be[i], 0, 0))
    grid_spec = pltpu.PrefetchScalarGridSpec(
        num_scalar_prefetch=2,
        grid=(n_blk,),
        in_specs=[idx_spec(lambda i, be, nv: (i, 0, 0)),
                  idx_spec(lambda i, be, nv: (jnp.minimum(i + 1, n_blk - 1), 0, 0)),
                  idx_spec(lambda i, be, nv: (i, 0, 0)),
                  pl.BlockSpec(memory_space=pl.ANY),
                  w_spec((D_MODEL, MOE_HIDDEN)), w_spec((D_MODEL, MOE_HIDDEN)), w_spec((MOE_HIDDEN, D_MODEL))],
        out_specs=pl.BlockSpec(memory_space=pl.ANY),
        scratch_shapes=[pltpu.VMEM((2, MOE_TILE * ROW_TILES, LANE), F32), pltpu.VMEM((MOE_TILE * ROW_TILES, LANE), F32),
                        pltpu.VMEM((D_MODEL, MOE_HIDDEN), BF16), pltpu.VMEM((D_MODEL, MOE_HIDDEN), BF16),
                        pltpu.VMEM((MOE_HIDDEN, D_MODEL), BF16),
                        pltpu.SemaphoreType.DMA((2,)), pltpu.SemaphoreType.DMA(())],
    )
    return pl.pallas_call(
        _expert_kernel,
        grid_spec=grid_spec,
        out_shape=jax.ShapeDtypeStruct((2 * n_tok * ROW_TILES, LANE), F32),
        compiler_params=_cparams(("arbitrary",)),
        name="experts",
    )(blk_expert, blk_valid, row_src, row_src, row_dst, f, w1, w3, w2).reshape(2, n_tok * ROW_TILES, LANE)


def _dest_kernel(rt_ref, ps_ref, o_ref):
    rt = rt_ref[...].astype(I32)
    ps = ps_ref[...]
    row = lax.broadcasted_iota(I32, (LANE, rt.shape[1]), 0)

    def dest(k):
        start = jnp.sum(jnp.where(row == rt[k:k + 1, :], ps, 0.0), axis=0, keepdims=True)
        return start.astype(I32) + rt[2 + k:3 + k, :]

    orow = lax.broadcasted_iota(I32, rt.shape, 0)
    o_ref[...] = jnp.where(orow == 0, dest(0), jnp.where(orow == 1, dest(1), 0))


def _pair_dest(route_t, pad_start, n_tok):
    ps = jnp.pad(pad_start.astype(F32), (0, LANE - MOE_EXPERTS)).reshape(LANE, 1)
    cols = next(r for r in (2048, 1024, 512, TILE) if n_tok % r == 0)
    return pl.pallas_call(
        _dest_kernel,
        grid=(n_tok // cols,),
        in_specs=[pl.BlockSpec((8, cols), lambda i: (0, i)), pl.BlockSpec((LANE, 1), lambda i: (0, 0))],
        out_specs=pl.BlockSpec((8, cols), lambda i: (0, i)),
        out_shape=jax.ShapeDtypeStruct((8, n_tok), I32),
        compiler_params=_cparams(("arbitrary",)),
        name="pair_dest",
    )(route_t, ps)


INV_UNROLL = 8


def _inverse_kernel(n_pairs, dest_ref, lo_ref, hi_ref, out_ref):
    def body(g, _):
        for u in range(INV_UNROLL):
            p = g * INV_UNROLL + u
            out_ref[dest_ref[p]] = p
        return 0

    lax.fori_loop(0, n_pairs // INV_UNROLL, body, 0)

    def fill_segment(e, _):
        def fill(r, _):
            out_ref[r] = 0
            return 0
        return lax.fori_loop(lo_ref[e], hi_ref[e], fill, 0)

    lax.fori_loop(0, MOE_EXPERTS + 1, fill_segment, 0)


def _inverse_rows(dest_flat, fill_lo, fill_hi, p_rows):
    n_pairs = dest_flat.shape[0]
    smem = pl.BlockSpec(memory_space=pltpu.SMEM)
    return pl.pallas_call(
        functools.partial(_inverse_kernel, n_pairs),
        in_specs=[smem, smem, smem],
        out_specs=smem,
        out_shape=jax.ShapeDtypeStruct((p_rows,), I32),
        name="inverse_rows",
    )(dest_flat, fill_lo, fill_hi)


def _route_plan(route_t, counts, n_tok):
    n_blk = (2 * n_tok) // MOE_TILE + MOE_EXPERTS
    p_rows = n_blk * MOE_TILE
    cnt = counts[MOE_GROUPS:MOE_GROUPS + MOE_EXPERTS, 0].astype(I32)
    padded = (cnt + MOE_TILE - 1) // MOE_TILE * MOE_TILE
    pad_end = jnp.cumsum(padded)
    pad_start = pad_end - padded
    dest = _pair_dest(route_t, pad_start, n_tok)[0:2].reshape(-1)
    fill_lo = jnp.concatenate([pad_start + cnt, pad_end[-1:]]).astype(I32)
    fill_hi = jnp.concatenate([pad_end, jnp.full((1,), p_rows, I32)]).astype(I32)
    row_dst = _inverse_rows(dest, fill_lo, fill_hi, p_rows)
    row_src = jnp.where(row_dst >= n_tok, row_dst - n_tok, row_dst)
    blk_start = (jnp.arange(n_blk + 1, dtype=I32) * MOE_TILE)[:, None]
    blk_expert = jnp.minimum(jnp.sum(pad_end[None, :] <= blk_start, axis=-1), MOE_EXPERTS - 1).astype(I32)
    inside = (pad_start[None, :] <= blk_start) & (blk_start < pad_end[None, :])
    blk_valid = jnp.sum(jnp.where(inside, jnp.clip((pad_start + cnt)[None, :] - blk_start, 0, MOE_TILE), 0), axis=-1).astype(I32)
    return (row_src.reshape(n_blk, 1, MOE_TILE), row_dst.reshape(n_blk, 1, MOE_TILE), blk_expert[:n_blk], blk_valid)


def _final_kernel(x_ref, y_ref, rg_ref, mod_ref, o_ref):
    g2 = mod_ref[0, :, 5 * D_MODEL:6 * D_MODEL]
    rg = rg_ref[...]
    o_ref[...] = x_ref[...] + g2 * (rg[:, 4:5] * _load_row_tiles(y_ref, (0,), TILE) + rg[:, 5:6] * _load_row_tiles(y_ref, (1,), TILE))


def _final(x, y_flat, rg, mod_l, n_tok):
    nq = SEQ // TILE
    tile_map = lambda i: (i, 0)
    return pl.pallas_call(
        _final_kernel,
        grid=(n_tok // TILE,),
        in_specs=[pl.BlockSpec((TILE, D_MODEL), tile_map), pl.BlockSpec((2, TILE * ROW_TILES, LANE), lambda i: (0, i, 0)),
                  pl.BlockSpec((TILE, LANE), tile_map), pl.BlockSpec((1, 1, 6 * D_MODEL), lambda i: (i // nq, 0, 0))],
        out_specs=pl.BlockSpec((TILE, D_MODEL), tile_map),
        out_shape=jax.ShapeDtypeStruct((n_tok, D_MODEL), F32),
        compiler_params=_cparams(("arbitrary",)),
        name="final_residual",
    )(x, y_flat, rg, mod_l)


def _diff_rows(w, heads):
    nl, _, k = w.shape
    w = w.reshape(nl, heads, 2, 2, 2, 12, k)
    w = w.transpose(0, 1, 4, 2, 3, 5, 6)
    w = jnp.pad(w.reshape(nl, heads, 4, 24, k), ((0, 0), (0, 0), (0, 0), (0, 8), (0, 0)))
    return w.reshape(nl, heads * LANE, k)


def _gqa_rows(w, groups):
    nl, _, k = w.shape
    w = w.reshape(nl, 2, groups, 2, 2, 16, k)
    w = w.transpose(0, 2, 4, 1, 3, 5, 6)
    return w.reshape(nl, groups * LANE, k)


def _in_weight(w_in):
    wt = jnp.swapaxes(w_in, 1, 2)
    nl = wt.shape[0]
    dv = jnp.pad(wt[:, 768:1152].reshape(nl, DIFF_HEADS, DIFF_V_DIM, D_MODEL), ((0, 0), (0, 0), (0, LANE - DIFF_V_DIM), (0, 0)))
    parts = [_diff_rows(wt[:, 0:384], DIFF_HEADS), _diff_rows(wt[:, 384:768], DIFF_HEADS),
             dv.reshape(nl, DIFF_HEADS * LANE, D_MODEL), _gqa_rows(wt[:, 1152:1536], 3), _gqa_rows(wt[:, 1536:1664], 1),
             wt[:, 1664:2816], jnp.pad(wt[:, 2816:2832], ((0, 0), (0, LANE - 16), (0, 0)))]
    return jnp.concatenate(parts, axis=1).astype(BF16)


def _lane_gain(g, rows_fn):
    return rows_fn(jnp.concatenate([g, g], axis=-1)[:, :, None], 1)[:, :, 0]


def _out_weight(w_out):
    nl = w_out.shape[0]
    diff = jnp.pad(w_out[:, 0:384].reshape(nl, DIFF_HEADS, DIFF_V_DIM, D_MODEL), ((0, 0), (0, 0), (0, LANE - DIFF_V_DIM), (0, 0)))
    gqa = w_out[:, 384:768].reshape(nl, 2, 3, GQA_HEAD_DIM, D_MODEL).transpose(0, 2, 1, 3, 4)
    return jnp.concatenate([diff.reshape(nl, 512, D_MODEL), gqa.reshape(nl, 384, D_MODEL), w_out[:, 768:]], axis=1).astype(BF16)


def kernel(x, c, ctx, c_ctx, norm1_g, norm2_g, w_mod, b_mod, w_in, w_out, diff_q_norm, diff_k_norm, diff_lambda, diff_subln, gqa_q_norm, gqa_k_norm, mlstm_conv_w, mlstm_conv_b, mlstm_gate_b, mlstm_head_norm, moe_wg, moe_bg, moe_we, moe_be, moe_w1, moe_w3, moe_w2):
    B = x.shape[0]
    nt = B * SEG_ROWS
    n_lat = B * SEQ

    w_in_r = _in_weight(w_in)
    w_out_r = _out_weight(w_out)
    gq_d = _lane_gain(diff_q_norm, _diff_rows) * (LOG2E * DIFF_HEAD_DIM ** -0.5)
    gk_d = _lane_gain(diff_k_norm, _diff_rows)
    gq_g = _lane_gain(gqa_q_norm, _gqa_rows) * (LOG2E * GQA_HEAD_DIM ** -0.5)
    gk_g = _lane_gain(gqa_k_norm, _gqa_rows)
    bound_d = _score_bound(diff_q_norm, diff_k_norm, DIFF_HEAD_DIM).reshape(DEPTH, 1)
    bound_g = _score_bound(gqa_q_norm, gqa_k_norm, GQA_HEAD_DIM).reshape(DEPTH, 1)
    gains = jnp.concatenate([jnp.tile(gq_d, (1, 4)), jnp.tile(gk_d, (1, 4)), jnp.tile(gq_g, (1, 3)), gk_g], axis=1)
    gains = gains.reshape(DEPTH, 1, 1536)
    gate_b = jnp.pad(mlstm_gate_b, ((0, 0), (0, LANE - 16))).reshape(DEPTH, 1, LANE)
    subln = jnp.pad(diff_subln, ((0, 0), (0, LANE - DIFF_V_DIM))).reshape(DEPTH, 1, LANE)
    lam_pad = jnp.pad(diff_lambda, ((0, 0), (0, 4), (0, LANE - DIFF_HEAD_DIM)))
    conv_w = jnp.pad(mlstm_conv_w, ((0, 0), (0, 5), (0, 0)))
    conv_b = mlstm_conv_b.reshape(DEPTH, 1, 512)
    w_r = jnp.swapaxes(jnp.pad(jnp.concatenate([moe_wg, moe_we], axis=2), ((0, 0), (0, 0), (0, LANE - 36))), 1, 2)
    wr_hi = w_r.astype(BF16)
    wr_lo = (w_r - wr_hi.astype(F32)).astype(BF16)
    rb = jnp.pad(jnp.concatenate([moe_bg, moe_be], axis=1), ((0, 0), (0, LANE - 36))).reshape(DEPTH, LANE, 1)
    pair_mat = jnp.asarray(_PAIR_MAT, BF16)
    head_mat = jnp.asarray((np.arange(256)[:, None] // 64 == np.arange(256)[None, :] // 64).astype(np.float32), BF16)
    tabs = _rope_tables(12, 24) + _rope_tables(16, 32)

    xa = jnp.concatenate([ctx, x], axis=1).reshape(nt, D_MODEL)
    assert B <= 8, "row 8 of the modulation table is reserved for the context conditioning"
    cvec = jnp.concatenate([c, jnp.zeros((8 - B, D_MODEL), F32), c_ctx[None, :], jnp.zeros((7, D_MODEL), F32)], axis=0)
    mod = _modulation(cvec, w_mod, b_mod).reshape(DEPTH, 16, 1, 6 * D_MODEL)

    moe_in = None
    for l in range(DEPTH):
        last = l == DEPTH - 1
        lam_init = 0.8 - 0.6 * math.exp(-0.3 * l)
        xa, (dq, dk, dv, gq, gk, gv, mqk, mv, mo, mg) = _inproj(
            xa, moe_in, mod[l], norm1_g[l].reshape(1, D_MODEL), w_in_r[l], gains[l], gate_b[l], tabs, pair_mat, nt)
        do = _diff_attention(bound_d[l], dq, dk, dv, lam_pad[l], subln[l], lam_init, nt, with_ctx=not last)
        go = _gqa_attention(bound_g[l], gq, gk, gv, nt, with_ctx=not last)
        mgt = mg[:, :16].reshape(B, N_CHUNKS, MLSTM_CHUNK, 16).transpose(0, 1, 3, 2)
        hs = _mlstm(mqk, mv, mg, mgt, conv_w[l], conv_b[l], nt)
        xm, f, route, route_t, counts = _outproj(do, go, hs, mo, xa, mod[l], norm2_g[l].reshape(1, D_MODEL),
                                         mlstm_head_norm[l].reshape(1, 256), w_out_r[l], wr_hi[l], wr_lo[l], rb[l],
                                         head_mat, nt, with_ctx=not last)
        n_tok = n_lat if last else nt
        row_src, row_dst, blk_expert, blk_valid = _route_plan(route_t, counts, n_tok)
        y = _experts(f, row_src, row_dst, blk_expert, blk_valid, moe_w1, moe_w3, moe_w2, l, n_tok)
        if last:
            out = _final(xm, y, route, mod[l][:B], n_tok)
            return out.reshape(B, SEQ, D_MODEL)
        xa = xm
        moe_in = (y, route, mod[l])
```

```python
import functools
import math

import numpy as np
import jax
import jax.numpy as jnp
from jax import lax
from jax.experimental import pallas as pl
from jax.experimental.pallas import tpu as pltpu

F32 = jnp.float32
BF16 = jnp.bfloat16
I32 = jnp.int32

D_MODEL = 1024
DEPTH = 4
GRID_W = 64
CTX_LEN = 256
SEQ = 2048
ROPE_THETA = 10000.0
EPS = 1e-6

DIFF_HEADS = 4
DIFF_HEAD_DIM = 48
DIFF_V_DIM = 96
GQA_Q_HEADS = 6
GQA_KV_HEADS = 2
GQA_HEAD_DIM = 64
MLSTM_HEADS = 4
MLSTM_HEAD_DIM = 64
MLSTM_CHUNK = 64
IN_WIDTH = 2832
MOE_GROUPS = 4
MOE_EPG = 8
MOE_EXPERTS = 32
MOE_HIDDEN = 512

LANE = 128
TILE = 256
SEG_TILES = (CTX_LEN + SEQ) // TILE
SEG_ROWS = CTX_LEN + SEQ
N_CHUNKS = SEG_ROWS // MLSTM_CHUNK
CTX_CHUNKS = CTX_LEN // MLSTM_CHUNK
MOE_TILE = 256
VMEM_LIMIT = 56 * 1024 * 1024

C_DQ, C_DK, C_DV, C_GQ, C_GK, C_GV, C_MQK, C_MV, C_MO, C_MG, C_END = (
    0, 512, 1024, 1536, 1920, 2048, 2176, 2688, 2944, 3200, 3328)
MIX_ROWS = 512 + 384 + 256


def _cparams(sem):
    return pltpu.CompilerParams(dimension_semantics=sem, vmem_limit_bytes=VMEM_LIMIT)


_LANE2 = np.arange(2 * LANE)
_PAIR_MAT = ((_LANE2[:, None] // LANE == _LANE2[None, :] // LANE)
             & ((_LANE2[:, None] // 32) % 2 == (_LANE2[None, :] // 32) % 2)).astype(np.float32)


def _rope_tables(nf, pad_from):
    t = jnp.arange(SEQ, dtype=I32)
    rows = (t // GRID_W).astype(F32)
    cols = (t % GRID_W).astype(F32)
    freqs = ROPE_THETA ** (-jnp.arange(nf, dtype=F32) / nf)
    lane = np.arange(LANE)
    i = lane % 32
    typ = (lane // 32) // 2
    use_rows = i < nf
    fidx = np.where(use_rows, i, i - nf)
    valid = i < pad_from
    fidx = np.where(valid, fidx, 0)
    ang = jnp.where(jnp.asarray(use_rows)[None, :], rows[:, None], cols[:, None]) * freqs[jnp.asarray(fidx)][None, :]
    cos = jnp.where(jnp.asarray(valid)[None, :], jnp.cos(ang), 1.0)
    sin = jnp.where(jnp.asarray(valid)[None, :], jnp.sin(ang), 0.0)
    sin = sin * jnp.asarray(np.where(typ == 0, -1.0, 1.0), F32)[None, :]
    cos = jnp.concatenate([jnp.ones((CTX_LEN, LANE), F32), cos], axis=0)
    sin = jnp.concatenate([jnp.zeros((CTX_LEN, LANE), F32), sin], axis=0)
    return cos, sin


def _dot(a, b):
    return jnp.dot(a, b, preferred_element_type=F32)


def _dot_nt(a, b):
    return lax.dot_general(a, b, (((1,), (1,)), ((), ())), preferred_element_type=F32)


def _dot_tn(a, b):
    return lax.dot_general(a, b, (((0,), (0,)), ((), ())), preferred_element_type=F32)


def _split3(x):
    x1 = x.astype(BF16)
    r1 = x - x1.astype(F32)
    x2 = r1.astype(BF16)
    x3 = (r1 - x2.astype(F32)).astype(BF16)
    return x1, x2, x3


def _dot_f32_by_exact(x, m):
    x1, x2, x3 = _split3(x)
    return _dot(x1, m) + _dot(x2, m) + _dot(x3, m)


def _dot_f32_by_exact2(x, m):
    x1 = x.astype(BF16)
    x2 = (x - x1.astype(F32)).astype(BF16)
    return _dot(x1, m) + _dot(x2, m)


def _exact_by_dot_f32(m, x):
    x1, x2, x3 = _split3(x)
    return _dot(m, x1) + _dot(m, x2) + _dot(m, x3)


def _sigmoid(x):
    return 1.0 / (1.0 + jnp.exp(-x))


def _silu(x):
    return x * _sigmoid(x)


def _log_sigmoid(x):
    return jnp.minimum(x, 0.0) - jnp.log1p(jnp.exp(-jnp.abs(x)))


ROW_TILES = D_MODEL // LANE


def _store_row_tiles(ref, lead, rows, x):
    for c in range(ROW_TILES):
        ref[lead + (pl.ds(c, rows, stride=ROW_TILES), slice(None))] = x[:, c * LANE:(c + 1) * LANE]


def _load_row_tiles(ref, lead, rows):
    return jnp.concatenate([ref[lead + (pl.ds(c, rows, stride=ROW_TILES), slice(None))] for c in range(ROW_TILES)], axis=1)


def _seg_tile(i, nq, off):
    return (i // nq) * SEG_TILES + off + i % nq


def _mod_row(i, nq, off):
    return jnp.where((off + i % nq) == 0, 8, i // nq)


MOD_BN = 1536


def _mod_kernel(c_ref, w_ref, b_ref, o_ref):
    a = _silu(c_ref[...]).astype(BF16)
    o_ref[0] = _dot(a, w_ref[0].astype(BF16)) + b_ref[0]


def _modulation(cvec, w_mod, b_mod):
    nb = 6 * D_MODEL // MOD_BN
    return pl.pallas_call(
        _mod_kernel,
        grid=(DEPTH, nb),
        in_specs=[pl.BlockSpec((16, D_MODEL), lambda l, n: (0, 0)),
                  pl.BlockSpec((1, D_MODEL, MOD_BN), lambda l, n: (l, 0, n)),
                  pl.BlockSpec((1, 1, MOD_BN), lambda l, n: (l, 0, n))],
        out_specs=pl.BlockSpec((1, 16, MOD_BN), lambda l, n: (l, 0, n)),
        out_shape=jax.ShapeDtypeStruct((DEPTH, 16, 6 * D_MODEL), F32),
        compiler_params=_cparams(("arbitrary", "arbitrary")),
        name="modulation",
    )(cvec, w_mod, b_mod.reshape(DEPTH, 1, 6 * D_MODEL))


def _inproj_kernel(has_moe, *refs):
    if has_moe:
        (x_ref, y_ref, rg_ref, modp_ref, mod_ref, n1_ref, w_ref, gains_ref, gb_ref, cd_ref, sd_ref, cg_ref, sg_ref,
         pm_ref, xo_ref, dq_ref, dk_ref, dv_ref, gq_ref, gk_ref, gv_ref, mqk_ref, mv_ref, mo_ref, mg_ref) = refs
    else:
        (x_ref, mod_ref, n1_ref, w_ref, gains_ref, gb_ref, cd_ref, sd_ref, cg_ref, sg_ref,
         pm_ref, dq_ref, dk_ref, dv_ref, gq_ref, gk_ref, gv_ref, mqk_ref, mv_ref, mo_ref, mg_ref) = refs
    x = x_ref[...]
    if has_moe:
        g2 = modp_ref[0, :, 5 * D_MODEL:6 * D_MODEL]
        rg = rg_ref[...]
        x = x + g2 * (rg[:, 4:5] * _load_row_tiles(y_ref, (0,), TILE) + rg[:, 5:6] * _load_row_tiles(y_ref, (1,), TILE))
        xo_ref[...] = x
    sh = mod_ref[0, :, 0:D_MODEL]
    sc = mod_ref[0, :, D_MODEL:2 * D_MODEL]
    xn = x * lax.rsqrt(jnp.mean(x * x, axis=-1, keepdims=True) + EPS) * n1_ref[...]
    h = (xn * (1.0 + sc) + sh).astype(BF16)
    pm = pm_ref[...]

    def proj(a, b):
        return _dot_nt(h, w_ref[a:b, :])

    def qk_group(col, gain_col, inv_dim, cos, sin, outs):
        y = proj(col, col + 512)
        for half in range(2):
            yh = y[:, half * 256:(half + 1) * 256]
            yn = yh * lax.rsqrt(_dot_f32_by_exact2(yh * yh, pm) * inv_dim + EPS)
            yn = yn * gains_ref[:, gain_col + half * 256:gain_col + (half + 1) * 256]
            for j in range(2):
                yb = yn[:, j * LANE:(j + 1) * LANE]
                ref, off = outs[half * 2 + j]
                ref[:, off:off + LANE] = (yb * cos + pltpu.roll(yb, 64, 1) * sin).astype(BF16)

    cd, sd, cg, sg = cd_ref[...], sd_ref[...], cg_ref[...], sg_ref[...]
    qk_group(C_DQ, 0, 1.0 / DIFF_HEAD_DIM, cd, sd, [(dq_ref, b * LANE) for b in range(4)])
    qk_group(C_DK, 512, 1.0 / DIFF_HEAD_DIM, cd, sd, [(dk_ref, b * LANE) for b in range(4)])
    qk_group(C_GQ, 1024, 1.0 / GQA_HEAD_DIM, cg, sg, [(gq_ref, 0), (gq_ref, LANE), (gq_ref, 2 * LANE), (gk_ref, 0)])
    dv_ref[...] = proj(C_DV, C_GQ).astype(BF16)
    rest = proj(C_GV, C_END)
    gv_ref[...] = rest[:, 0:C_MQK - C_GV].astype(BF16)
    mqk_ref[...] = rest[:, C_MQK - C_GV:C_MV - C_GV]
    mv_ref[...] = rest[:, C_MV - C_GV:C_MO - C_GV].astype(BF16)
    mo_ref[...] = rest[:, C_MO - C_GV:C_MG - C_GV]
    mg_ref[...] = rest[:, C_MG - C_GV:C_END - C_GV] + gb_ref[...]


def _inproj(x, moe_in, mod_l, n1g, w_in_l, gains, gate_b, tabs, pair_mat, nt):
    has_moe = moe_in is not None
    n_tiles = nt // TILE
    tile_map = lambda i: (i, 0)
    mod_spec = pl.BlockSpec((1, 1, 6 * D_MODEL), lambda i: (_mod_row(i, SEG_TILES, 0), 0, 0))
    tab_spec = pl.BlockSpec((TILE, LANE), lambda i: (i % SEG_TILES, 0))

    def full(shape):
        return pl.BlockSpec(shape, lambda i: (0,) * len(shape))

    in_specs = [pl.BlockSpec((TILE, D_MODEL), tile_map)]
    args = [x]
    if has_moe:
        y_flat, rg, mod_prev = moe_in
        in_specs += [pl.BlockSpec((2, TILE * ROW_TILES, LANE), lambda i: (0, i, 0)), pl.BlockSpec((TILE, LANE), tile_map), mod_spec]
        args += [y_flat, rg, mod_prev]
    in_specs += [mod_spec, full((1, D_MODEL)), full((C_END, D_MODEL)), full((1, 1536)), full((1, LANE)),
                 tab_spec, tab_spec, tab_spec, tab_spec, full((2 * LANE, 2 * LANE))]
    args += [mod_l, n1g, w_in_l, gains, gate_b, tabs[0], tabs[1], tabs[2], tabs[3], pair_mat]

    def o(width, dtype):
        return pl.BlockSpec((TILE, width), tile_map), jax.ShapeDtypeStruct((nt, width), dtype)

    outs = []
    if has_moe:
        outs.append(o(D_MODEL, F32))
    outs += [o(512, BF16), o(512, BF16), o(512, BF16), o(384, BF16), o(LANE, BF16), o(LANE, BF16),
             o(512, F32), o(256, BF16), o(256, F32), o(LANE, F32)]
    res = pl.pallas_call(
        functools.partial(_inproj_kernel, has_moe),
        grid=(n_tiles,),
        in_specs=in_specs,
        out_specs=[s for s, _ in outs],
        out_shape=[s for _, s in outs],
        compiler_params=_cparams(("arbitrary",)),
        name="inproj_moe" if has_moe else "inproj",
    )(*args)
    if has_moe:
        return res[0], res[1:]
    return x, res


def _lambda_value(lam_ref, lam_init):
    lam = lam_ref[...]
    s01 = jnp.sum(lam[0:1] * lam[1:2], axis=-1, keepdims=True)
    s23 = jnp.sum(lam[2:3] * lam[3:4], axis=-1, keepdims=True)
    return jnp.exp(s01) - jnp.exp(s23) + lam_init


LOG2E = 1.4426950408889634
SAFE_LOG2_RANGE = 60.0


def _exp_scores(s, stabilise):
    if stabilise:
        s = s - jnp.max(s, axis=-1, keepdims=True)
    return jnp.exp2(s).astype(BF16)


def _pv(q, kk, kmask, vv, stabilise):
    return _dot(_exp_scores(_dot_nt(q, kk * kmask), stabilise), vv)


def _score_bound(q_gain, k_gain, head_dim):
    return (1.02 * LOG2E * math.sqrt(head_dim)) * jnp.max(jnp.abs(q_gain), axis=-1) * jnp.max(jnp.abs(k_gain), axis=-1)


def _attn_branches(with_ctx, bound_ref, run):
    def on_keys(rows):
        small = bound_ref[0] <= SAFE_LOG2_RANGE

        @pl.when(small)
        def _():
            run(rows, False)

        @pl.when(jnp.logical_not(small))
        def _():
            run(rows, True)

    if not with_ctx:
        on_keys(SEG_ROWS)
        return
    t = pl.program_id(2)

    @pl.when(t == 0)
    def _():
        on_keys(TILE)

    @pl.when(t > 0)
    def _():
        on_keys(SEG_ROWS)


def _attn_maps(with_ctx):
    if with_ctx:
        m = lambda b, h, t: (b * SEG_TILES + t, h)
        return SEG_TILES, m, m
    nq = SEG_TILES - 1
    return nq, (lambda b, h, t: (b * SEG_TILES + 1 + t, h)), (lambda b, h, t: (b * nq + t, h))


def _lane_masks():
    lane = np.arange(LANE)
    even = (lane // 32) % 2 == 0
    rows = [even, ~even, lane < 64, lane >= 64, lane == 64, lane == 0, lane == DIFF_V_DIM, lane < 0]
    return jnp.asarray(np.stack(rows).astype(np.float32))


def _mask_row(lm_ref, r):
    return lm_ref[r:r + 1, :].astype(BF16)


DIFF_HEADS_PER_STEP = 4


def _diff_attn_kernel(lam_init, with_ctx, bound_ref, q_ref, k_ref, v_ref, lam_ref, sg_ref, lm_ref, o_ref):
    def run(rows, stabilise):
        lam = _lambda_value(lam_ref, lam_init)
        for hh in range(DIFF_HEADS_PER_STEP):
            lanes = slice(hh * LANE, (hh + 1) * LANE)
            q = q_ref[:, lanes]
            kk = k_ref[0:rows, lanes]
            v1 = v_ref[0:rows, lanes] + _mask_row(lm_ref, 6)
            o1 = _pv(q, kk, _mask_row(lm_ref, 0), v1, stabilise)
            o2 = _pv(q, kk, _mask_row(lm_ref, 1), v1, stabilise)
            o = o1 * (1.0 / o1[:, DIFF_V_DIM:DIFF_V_DIM + 1]) - o2 * (lam / o2[:, DIFF_V_DIM:DIFF_V_DIM + 1])
            o = jnp.where(lax.broadcasted_iota(I32, o.shape, 1) < DIFF_V_DIM, o, 0.0)
            ms = jnp.sum(o * o, axis=-1, keepdims=True) * (1.0 / DIFF_V_DIM)
            o_ref[:, lanes] = (o * lax.rsqrt(ms + EPS) * sg_ref[...] * (1.0 - lam_init)).astype(BF16)

    _attn_branches(with_ctx, bound_ref, run)


def _diff_attention(bound, dq, dk, dv, lam_pad, subln, lam_init, nt, with_ctx):
    nb = nt // SEG_ROWS
    nq, q_map, o_map = _attn_maps(with_ctx)
    kv_map = lambda b, h, t: (b, h)
    const = lambda b, h, t: (0, 0)
    width = DIFF_HEADS_PER_STEP * LANE
    return pl.pallas_call(
        functools.partial(_diff_attn_kernel, lam_init, with_ctx),
        grid=(nb, DIFF_HEADS // DIFF_HEADS_PER_STEP, nq),
        in_specs=[pl.BlockSpec(memory_space=pltpu.SMEM),
                  pl.BlockSpec((TILE, width), q_map), pl.BlockSpec((SEG_ROWS, width), kv_map),
                  pl.BlockSpec((SEG_ROWS, width), kv_map), pl.BlockSpec((8, LANE), const),
                  pl.BlockSpec((1, LANE), const), pl.BlockSpec((8, LANE), const)],
        out_specs=pl.BlockSpec((TILE, width), o_map),
        out_shape=jax.ShapeDtypeStruct((nb * nq * TILE, 512), BF16),
        compiler_params=_cparams(("arbitrary", "arbitrary", "arbitrary")),
        name="diff_attn",
    )(bound, dq, dk, dv, lam_pad, subln, _lane_masks())


GQA_PAIRS = GQA_Q_HEADS // 2


def _gqa_attn_kernel(with_ctx, bound_ref, q_ref, k_ref, v_ref, lm_ref, o_ref):
    def run(rows, stabilise):
        kk = k_ref[0:rows, :]
        vv = v_ref[0:rows, :]
        va = vv * _mask_row(lm_ref, 2) + _mask_row(lm_ref, 4)
        vb = vv * _mask_row(lm_ref, 3) + _mask_row(lm_ref, 5)
        for p in range(GQA_PAIRS):
            lanes = slice(p * LANE, (p + 1) * LANE)
            q = q_ref[:, lanes]
            oa = _pv(q, kk, _mask_row(lm_ref, 0), va, stabilise)
            ob = _pv(q, kk, _mask_row(lm_ref, 1), vb, stabilise)
            lane = lax.broadcasted_iota(I32, oa.shape, 1)
            o_ref[:, lanes] = jnp.where(lane < GQA_HEAD_DIM, oa * (1.0 / oa[:, GQA_HEAD_DIM:GQA_HEAD_DIM + 1]),
                                        ob * (1.0 / ob[:, 0:1])).astype(BF16)

    _attn_branches(with_ctx, bound_ref, run)


def _gqa_attention(bound, gq, gk, gv, nt, with_ctx):
    nb = nt // SEG_ROWS
    nq, q_map, o_map = _attn_maps(with_ctx)
    kv_map = lambda b, p, t: (b, 0)
    return pl.pallas_call(
        functools.partial(_gqa_attn_kernel, with_ctx),
        grid=(nb, 1, nq),
        in_specs=[pl.BlockSpec(memory_space=pltpu.SMEM),
                  pl.BlockSpec((TILE, GQA_PAIRS * LANE), q_map), pl.BlockSpec((SEG_ROWS, LANE), kv_map),
                  pl.BlockSpec((SEG_ROWS, LANE), kv_map), pl.BlockSpec((8, LANE), lambda b, p, t: (0, 0))],
        out_specs=pl.BlockSpec((TILE, GQA_PAIRS * LANE), o_map),
        out_shape=jax.ShapeDtypeStruct((nb * nq * TILE, 384), BF16),
        compiler_params=_cparams(("arbitrary", "arbitrary", "arbitrary")),
        name="gqa_attn",
    )(bound, gq, gk, gv, _lane_masks())


def _mlstm_kernel(nbk, mqk_ref, mv_ref, mg_ref, mgt_ref, cw_ref, cb_ref, h_ref, qk_s, bcf_s, bcb_s, brf_s, brb_s):
    L = MLSTM_CHUNK
    h_ref[...] = jnp.zeros_like(h_ref)
    w0, w1, w2, cb = cw_ref[0:1, :], cw_ref[1:2, :], cw_ref[2:3, :], cb_ref[...]
    rid = lax.broadcasted_iota(I32, (TILE, 512), 0)
    kscale = jnp.where(lax.broadcasted_iota(I32, (1, 512), 1) < 256, 1.0, MLSTM_HEAD_DIM ** -0.5)
    zrow = jnp.zeros((1, 512), F32)
    for bc in range(nbk * SEG_TILES):
        c = bc % SEG_TILES
        r0 = bc * TILE
        xc = mqk_ref[r0:r0 + TILE, :]
        prev = zrow if c in (0, 1) else mqk_ref[r0 - 1:r0, :]
        nxt = zrow if c in (0, SEG_TILES - 1) else mqk_ref[r0 + TILE:r0 + TILE + 1, :]
        up = jnp.where(rid == 0, prev, pltpu.roll(xc, 1, 0))
        dn = jnp.where(rid == TILE - 1, nxt, pltpu.roll(xc, TILE - 1, 0))
        y = w0 * up + w1 * xc + w2 * dn + cb
        qk_s[r0:r0 + TILE, :] = (_silu(y) * kscale).astype(BF16)

    ti = lax.broadcasted_iota(I32, (L, L), 0)
    si = lax.broadcasted_iota(I32, (L, L), 1)
    tri_le = (si <= ti)
    m_le = tri_le.astype(BF16)
    m_ge = (si >= ti).astype(BF16)
    row2 = lax.broadcasted_iota(I32, (2 * L, L), 0)
    trow = row2 % L
    scol = lax.broadcasted_iota(I32, (2 * L, L), 1)
    top2 = row2 < L
    mask_f = scol <= trow
    mask_b = scol >= trow
    lane_lo = lax.broadcasted_iota(I32, (L, LANE), 1) < L
    lane_lo256 = lax.broadcasted_iota(I32, (L, 2 * LANE), 1) % LANE < L
    rr = lax.broadcasted_iota(I32, (LANE, LANE), 0)
    cc = lax.broadcasted_iota(I32, (LANE, LANE), 1)
    blockdiag = (rr < L) == (cc < L)
    rows_lo = lax.broadcasted_iota(I32, (LANE, 1), 0) < L
    top_col = lax.broadcasted_iota(I32, (2 * L, 1), 0) < L

    def chain(bb, c, is_fwd, p, gcol, bcol_all, grow, brow_all, state):
        ct, nm, m0, m1 = state
        h0, h1 = 2 * p, 2 * p + 1
        gi, gf = (0, 4) if is_fwd else (8, 12)
        r0 = pl.multiple_of(bb * SEG_ROWS + c * L, L)
        qb = qk_s[pl.ds(r0, L), p * LANE:(p + 1) * LANE]
        kb = qk_s[pl.ds(r0, L), 256 + p * LANE:256 + (p + 1) * LANE]
        v128 = mv_ref[pl.ds(r0, L), p * LANE:(p + 1) * LANE]

        def stack_cols(arr, j0, j1):
            return jnp.concatenate([arr[:, j0:j0 + 1], arr[:, j1:j1 + 1]], axis=0)

        def stack_rows(arr, j0, j1):
            return jnp.where(top2, arr[j0:j0 + 1, :], arr[j1:j1 + 1, :])

        bcol = stack_cols(bcol_all, gf + h0, gf + h1)
        licol = stack_cols(gcol, gi + h0, gi + h1)
        crow = stack_rows(grow, gi + h0, gi + h1) - stack_rows(brow_all, gf + h0, gf + h1)
        cm = jnp.where(mask_f if is_fwd else mask_b, crow, -jnp.inf)
        mcol = jnp.where(top_col, m0, m1)
        u = jnp.maximum(mcol, jnp.max(cm, axis=-1, keepdims=True))
        zq = jnp.zeros_like(qb)
        qstack = jnp.concatenate([jnp.where(lane_lo, qb, zq), jnp.where(lane_lo, zq, qb)], axis=0)
        w = (jnp.exp(cm - u) * _dot_nt(qstack, kb)).astype(BF16)
        wv = _dot(w, jnp.concatenate([v128, jnp.ones_like(v128)], axis=1))
        wv = jnp.where(lane_lo256, wv[:L], wv[L:])
        qcn = _dot(qb, jnp.concatenate([ct, nm], axis=1).astype(BF16))
        a = jnp.exp(mcol - u)
        emt = jnp.exp(-(bcol + u))
        a128 = jnp.where(lane_lo, a[:L], a[L:])
        emt128 = jnp.where(lane_lo, emt[:L], emt[L:])
        num = wv[:, :LANE] + a128 * qcn[:, :LANE]
        den = wv[:, LANE:] + a128 * qcn[:, LANE:]
        h_ref[pl.ds(r0, L), p * LANE:(p + 1) * LANE] += num / jnp.maximum(jnp.abs(den), emt128)
        e0 = (L - 1) if is_fwd else 0
        bend0 = bcol[e0:e0 + 1, :]
        bend1 = bcol[L + e0:L + e0 + 1, :]
        bend = jnp.where(top_col, bend0, bend1)
        g = bend - bcol + licol
        m0n = jnp.maximum(bend0 + m0, jnp.max(g[:L], axis=0, keepdims=True))
        m1n = jnp.maximum(bend1 + m1, jnp.max(g[L:], axis=0, keepdims=True))
        ws = jnp.exp(g - jnp.where(top_col, m0n, m1n))
        ae0 = jnp.exp(bend0 + m0 - m0n)
        ae1 = jnp.exp(bend1 + m1 - m1n)
        ws128 = jnp.where(lane_lo, ws[:L], ws[L:])
        vw = jnp.concatenate([v128.astype(F32) * ws128, ws128], axis=1).astype(BF16)
        upd = _dot_tn(kb, vw)
        ae = jnp.where(rows_lo, ae0, ae1)
        ct_new = ae * ct + jnp.where(blockdiag, upd[:, :LANE], 0.0)
        nm_new = ae * nm + jnp.where(blockdiag, upd[:, LANE:], 0.0)
        return ct_new, nm_new, m0n, m1n

    def gate_sums(j, _):
        for bb in range(nbk):
            r0 = pl.multiple_of(bb * SEG_ROWS + j * L, L)
            lf_col = _log_sigmoid(mg_ref[pl.ds(r0, L), :])
            lf_row = _log_sigmoid(mgt_ref[bb, j])
            bcf_s[pl.ds(r0, L), :] = _exact_by_dot_f32(m_le, lf_col)
            bcb_s[pl.ds(r0, L), :] = _exact_by_dot_f32(m_ge, lf_col)
            brf_s[bb * N_CHUNKS + j] = _dot_f32_by_exact(lf_row, m_ge)
            brb_s[bb * N_CHUNKS + j] = _dot_f32_by_exact(lf_row, m_le)
        return 0

    lax.fori_loop(0, N_CHUNKS, gate_sums, 0, unroll=4)

    def body(i, carry):
        cf = i
        cbk = jnp.where(i < CTX_CHUNKS, CTX_CHUNKS - 1 - i, N_CHUNKS + CTX_CHUNKS - 1 - i)
        new = []
        for bb in range(nbk):
            for d, c in enumerate((cf, cbk)):
                is_fwd = d == 0
                r0 = pl.multiple_of(bb * SEG_ROWS + c * L, L)
                gcol = mg_ref[pl.ds(r0, L), :]
                grow = mgt_ref[bb, c]
                bcol_all = (bcf_s if is_fwd else bcb_s)[pl.ds(r0, L), :]
                brow_all = (brf_s if is_fwd else brb_s)[bb * N_CHUNKS + c]
                for p in range(2):
                    st = carry[(bb * 2 + d) * 2 + p]
                    new.append(chain(bb, c, is_fwd, p, gcol, bcol_all, grow, brow_all, st))
        return tuple(new)

    z = (jnp.zeros((LANE, LANE), F32), jnp.zeros((LANE, LANE), F32), jnp.zeros((1, 1), F32), jnp.zeros((1, 1), F32))
    lax.fori_loop(0, N_CHUNKS, body, (z,) * (4 * nbk), unroll=9)


def _mlstm(mqk, mv, mg, mgt, conv_w, conv_b, nt):
    nb = nt // SEG_ROWS
    nbk = 1
    blk = lambda w: pl.BlockSpec((nbk * SEG_ROWS, w), lambda g: (g, 0))
    return pl.pallas_call(
        functools.partial(_mlstm_kernel, nbk),
        grid=(nb // nbk,),
        in_specs=[blk(512), blk(256), blk(LANE),
                  pl.BlockSpec((nbk, N_CHUNKS, 16, MLSTM_CHUNK), lambda g: (g, 0, 0, 0)),
                  pl.BlockSpec((8, 512), lambda g: (0, 0)), pl.BlockSpec((1, 512), lambda g: (0, 0))],
        out_specs=blk(256),
        out_shape=jax.ShapeDtypeStruct((nt, 256), F32),
        scratch_shapes=[pltpu.VMEM((nbk * SEG_ROWS, 512), BF16), pltpu.VMEM((nbk * SEG_ROWS, LANE), F32),
                        pltpu.VMEM((nbk * SEG_ROWS, LANE), F32), pltpu.VMEM((nbk * N_CHUNKS, 16, MLSTM_CHUNK), F32),
                        pltpu.VMEM((nbk * N_CHUNKS, 16, MLSTM_CHUNK), F32)],
        compiler_params=_cparams(("arbitrary",)),
        name="mlstm",
    )(mqk, mv, mg, mgt, conv_w, conv_b)


OUT_SUB = 2


def _outproj_kernel(sub, do_ref, go_ref, *refs):
    per_tile = [refs[4 * s:4 * s + 4] for s in range(sub)]
    n2_ref, hn_ref, wo_ref, wr_hi_ref, wr_lo_ref, rb_ref, hm_ref, xm_ref, f_ref, rt_ref, rtt_ref, cnt_ref, carry = refs[4 * sub:]

    @pl.when(pl.program_id(0) == 0)
    def _():
        carry[...] = jnp.zeros_like(carry)

    for s, (hs_ref, mo_ref, x_ref, mod_ref) in enumerate(per_tile):
        _outproj_tile(slice(s * TILE, (s + 1) * TILE), s, do_ref, go_ref, hs_ref, mo_ref, x_ref, mod_ref, n2_ref, hn_ref, wo_ref,
                      wr_hi_ref, wr_lo_ref, rb_ref, hm_ref, xm_ref, f_ref, rt_ref, rtt_ref, cnt_ref, carry)


def _outproj_tile(rows, s, do_ref, go_ref, hs_ref, mo_ref, x_ref, mod_ref, n2_ref, hn_ref, wo_ref, wr_hi_ref, wr_lo_ref,
                  rb_ref, hm_ref, xm_ref, f_ref, rt_ref, rtt_ref, cnt_ref, carry):
    hsum = hs_ref[...]
    ssq = _dot_f32_by_exact2(hsum * hsum, hm_ref[...])
    ml = hsum * lax.rsqrt(ssq * (1.0 / MLSTM_HEAD_DIM) + EPS) * hn_ref[...] * _sigmoid(mo_ref[...])
    acc = _dot(do_ref[rows, :], wo_ref[0:512, :])
    acc += _dot(go_ref[rows, :], wo_ref[512:896, :])
    acc += _dot(ml.astype(BF16), wo_ref[896:MIX_ROWS, :])
    g1 = mod_ref[0, :, 2 * D_MODEL:3 * D_MODEL]
    sh2 = mod_ref[0, :, 3 * D_MODEL:4 * D_MODEL]
    sc2 = mod_ref[0, :, 4 * D_MODEL:5 * D_MODEL]
    x = x_ref[...] + g1 * acc
    xm_ref[rows, :] = x
    xn = x * lax.rsqrt(jnp.mean(x * x, axis=-1, keepdims=True) + EPS) * n2_ref[...]
    f = xn * (1.0 + sc2) + sh2
    for c in range(ROW_TILES):
        f_ref[pl.ds(s * TILE * ROW_TILES + c, TILE, stride=ROW_TILES), :] = f[:, c * LANE:(c + 1) * LANE]
    f1, f2, _ = _split3(f)
    lt = _dot_nt(wr_hi_ref[...], f1) + _dot_nt(wr_hi_ref[...], f2) + _dot_nt(wr_lo_ref[...], f1) + rb_ref[...]
    row = lax.broadcasted_iota(I32, lt.shape, 0)
    neg = jnp.float32(-jnp.inf)
    big = jnp.int32(1 << 20)
    lg = jnp.where(row < MOE_GROUPS, lt, neg)
    gmax = jnp.max(lg, axis=0, keepdims=True)
    g_top = 1.0 / jnp.sum(jnp.exp(lg - gmax), axis=0, keepdims=True)
    g_idx = jnp.min(jnp.where(lg == gmax, row, big), axis=0, keepdims=True)
    in_grp = (row >= MOE_GROUPS) & (row < MOE_GROUPS + MOE_EXPERTS) & ((row - MOE_GROUPS) // MOE_EPG == g_idx)
    le = jnp.where(in_grp, lt, neg)
    v1 = jnp.max(le, axis=0, keepdims=True)
    l1 = jnp.min(jnp.where(le == v1, row, big), axis=0, keepdims=True)
    le2 = jnp.where(row == l1, neg, le)
    v2 = jnp.max(le2, axis=0, keepdims=True)
    l2 = jnp.min(jnp.where(le2 == v2, row, big), axis=0, keepdims=True)
    ex = jnp.exp(v2 - v1)
    gate1 = g_top / (1.0 + ex)
    gate2 = gate1 * ex
    oh1 = (row == l1)
    oh2 = (row == l2)
    both = oh1.astype(BF16) + oh2.astype(BF16)
    ri_ = lax.broadcasted_iota(I32, (TILE, TILE), 0)
    ci_ = lax.broadcasted_iota(I32, (TILE, TILE), 1)
    before = _dot(both, (ri_ < ci_).astype(BF16)) + carry[:, 0:1]
    rank1 = jnp.sum(jnp.where(oh1, before, 0.0), axis=0, keepdims=True)
    rank2 = jnp.sum(jnp.where(oh2, before, 0.0), axis=0, keepdims=True)
    carry[...] = carry[...] + jnp.sum(both.astype(F32), axis=1, keepdims=True)
    cnt_ref[...] = carry[...]
    packed = jnp.where(row == 0, (l1 - MOE_GROUPS).astype(F32), jnp.where(row == 1, (l2 - MOE_GROUPS).astype(F32),
             jnp.where(row == 2, rank1, jnp.where(row == 3, rank2, jnp.where(row == 4, gate1, jnp.where(row == 5, gate2, 0.0))))))
    rt_ref[rows, :] = packed.T
    rtt_ref[:, rows] = packed[0:8, :]


def _outproj(do, go, hs, mo, x, mod_l, n2g, hn, w_out_l, wr_hi, wr_lo, rb, head_mat, nt, with_ctx):
    nb = nt // SEG_ROWS
    nq, off = (SEG_TILES, 0) if with_ctx else (SEG_TILES - 1, 1)
    sub = OUT_SUB if (nb * nq) % OUT_SUB == 0 else 1
    n_steps = nb * nq // sub
    n_out = nb * nq * TILE
    out_map = lambda i: (i, 0)

    def full(shape):
        return pl.BlockSpec(shape, lambda i: (0,) * len(shape))

    in_specs = [pl.BlockSpec((sub * TILE, 512), out_map), pl.BlockSpec((sub * TILE, 384), out_map)]
    per_tile_args = []
    for s in range(sub):
        in_map = lambda i, s=s: (_seg_tile(i * sub + s, nq, off), 0)
        in_specs += [pl.BlockSpec((TILE, 256), in_map), pl.BlockSpec((TILE, 256), in_map), pl.BlockSpec((TILE, D_MODEL), in_map),
                     pl.BlockSpec((1, 1, 6 * D_MODEL), lambda i, s=s: (_mod_row(i * sub + s, nq, off), 0, 0))]
        per_tile_args += [hs, mo, x, mod_l]
    in_specs += [full((1, D_MODEL)), full((1, 256)), full((MIX_ROWS, D_MODEL)), full((LANE, D_MODEL)), full((LANE, D_MODEL)),
                 full((LANE, 1)), full((256, 256))]
    out_specs = [pl.BlockSpec((sub * TILE, D_MODEL), out_map), pl.BlockSpec((sub * TILE * ROW_TILES, LANE), out_map),
                 pl.BlockSpec((sub * TILE, LANE), out_map), pl.BlockSpec((8, sub * TILE), lambda i: (0, i)),
                 pl.BlockSpec((LANE, LANE), lambda i: (0, 0))]
    out_shape = [jax.ShapeDtypeStruct((n_out, D_MODEL), F32), jax.ShapeDtypeStruct((n_out * ROW_TILES, LANE), F32),
                 jax.ShapeDtypeStruct((n_out, LANE), F32), jax.ShapeDtypeStruct((8, n_out), F32),
                 jax.ShapeDtypeStruct((LANE, LANE), F32)]
    return pl.pallas_call(
        functools.partial(_outproj_kernel, sub),
        grid=(n_steps,),
        in_specs=in_specs,
        out_specs=out_specs,
        out_shape=out_shape,
        scratch_shapes=[pltpu.VMEM((LANE, LANE), F32)],
        compiler_params=_cparams(("arbitrary",)),
        name="outproj_route",
    )(do, go, *per_tile_args, n2g, hn, w_out_l, wr_hi, wr_lo, rb, head_mat)


DMA_UNROLL = 8


def _for_rows(n, fn):
    groups = lax.shift_right_logical(n, int(math.log2(DMA_UNROLL)))

    def group(g, _):
        for u in range(DMA_UNROLL):
            fn(g * DMA_UNROLL + u, u % 2)
        return 0

    def single(r, _):
        fn(r, 0)
        return 0

    lax.fori_loop(0, groups, group, 0)
    lax.fori_loop(groups * DMA_UNROLL, n, single, 0)


def _expert_kernel(be_ref, nv_ref, src_ref, srcn_ref, dst_ref, f_hbm, w1_ref, w3_ref, w2_ref, y_hbm,
                   xbuf, ybuf, w1s, w3s, w2s, gsem, ssem):
    i = pl.program_id(0)
    nv = nv_ref[i]
    nv_next = nv_ref[i + 1]
    slot = i % 2

    def tile(r):
        return pl.ds(pl.multiple_of(r * ROW_TILES, ROW_TILES), ROW_TILES)

    def gather_copy(idx_ref, s, r):
        return pltpu.make_async_copy(f_hbm.at[tile(idx_ref[0, 0, r])], xbuf.at[s, tile(r)], gsem.at[s])

    def scatter_copy(r):
        return pltpu.make_async_copy(ybuf.at[tile(r)], y_hbm.at[tile(dst_ref[0, 0, r])], ssem)

    def rows(n):
        return pl.ds(0, pl.multiple_of(n * ROW_TILES, ROW_TILES))

    def gather_wait(n):
        pltpu.make_async_copy(f_hbm.at[rows(n)], xbuf.at[slot, rows(n)], gsem.at[slot]).wait()

    def scatter_wait(n):
        pltpu.make_async_copy(ybuf.at[rows(n)], y_hbm.at[rows(n)], ssem).wait()

    @pl.when(i == 0)
    def _():
        xbuf[...] = jnp.zeros_like(xbuf)
        _for_rows(nv, lambda r, pr: gather_copy(src_ref, 0, r).start(priority=pr))

    @pl.when(nv_next > 0)
    def _():
        _for_rows(nv_next, lambda r, pr: gather_copy(srcn_ref, 1 - slot, r).start(priority=pr))

    @pl.when((nv > 0) & ((i == 0) | (be_ref[i] != be_ref[jnp.maximum(i - 1, 0)])))
    def _():
        w1s[...] = w1_ref[0, 0].astype(BF16)
        w3s[...] = w3_ref[0, 0].astype(BF16)
        w2s[...] = w2_ref[0, 0].astype(BF16)

    @pl.when(nv > 0)
    def _():
        gather_wait(nv)
        xb = _load_row_tiles(xbuf, (slot,), MOE_TILE).astype(BF16)
        hh = _silu(_dot(xb, w1s[...])) * _dot(xb, w3s[...])
        y = _dot(hh.astype(BF16), w2s[...])

        @pl.when(i > 0)
        def _():
            scatter_wait(nv_ref[jnp.maximum(i - 1, 0)])

        _store_row_tiles(ybuf, (), MOE_TILE, y)
        _for_rows(nv, lambda r, pr: scatter_copy(r).start(priority=pr))

        @pl.when(nv_next == 0)
        def _():
            scatter_wait(nv)


def _experts(f, row_src, row_dst, blk_expert, blk_valid, w1, w3, w2, layer, n_tok):
    n_blk = row_src.shape[0]
    idx_spec = lambda fn: pl.BlockSpec((1, 1, MOE_TILE), fn, memory_space=pltpu.SMEM)
    w_spec = lambda shape: pl.BlockSpec((1, 1) + shape, lambda i, be, nv: (layer, be[i], 0, 0))
    grid_spec = pltpu.PrefetchScalarGridSpec(
        num_scalar_prefetch=2,
        grid=(n_blk,),
        in_specs=[idx_spec(lambda i, be, nv: (i, 0, 0)),
                  idx_spec(lambda i, be, nv: (jnp.minimum(i + 1, n_blk - 1), 0, 0)),
                  idx_spec(lambda i, be, nv: (i, 0, 0)),
                  pl.BlockSpec(memory_space=pl.ANY),
                  w_spec((D_MODEL, MOE_HIDDEN)), w_spec((D_MODEL, MOE_HIDDEN)), w_spec((MOE_HIDDEN, D_MODEL))],
        out_specs=pl.BlockSpec(memory_space=pl.ANY),
        scratch_shapes=[pltpu.VMEM((2, MOE_TILE * ROW_TILES, LANE), F32), pltpu.VMEM((MOE_TILE * ROW_TILES, LANE), F32),
                        pltpu.VMEM((D_MODEL, MOE_HIDDEN), BF16), pltpu.VMEM((D_MODEL, MOE_HIDDEN), BF16),
                        pltpu.VMEM((MOE_HIDDEN, D_MODEL), BF16),
                        pltpu.SemaphoreType.DMA((2,)), pltpu.SemaphoreType.DMA(())],
    )
    return pl.pallas_call(
        _expert_kernel,
        grid_spec=grid_spec,
        out_shape=jax.ShapeDtypeStruct((2 * n_tok * ROW_TILES, LANE), F32),
        compiler_params=_cparams(("arbitrary",)),
        name="experts",
    )(blk_expert, blk_valid, row_src, row_src, row_dst, f, w1, w3, w2).reshape(2, n_tok * ROW_TILES, LANE)


def _dest_kernel(rt_ref, ps_ref, o_ref):
    rt = rt_ref[...].astype(I32)
    ps = ps_ref[...]
    row = lax.broadcasted_iota(I32, (LANE, rt.shape[1]), 0)

    def dest(k):
        start = jnp.sum(jnp.where(row == rt[k:k + 1, :], ps, 0.0), axis=0, keepdims=True)
        return start.astype(I32) + rt[2 + k:3 + k, :]

    orow = lax.broadcasted_iota(I32, rt.shape, 0)
    o_ref[...] = jnp.where(orow == 0, dest(0), jnp.where(orow == 1, dest(1), 0))


def _pair_dest(route_t, pad_start, n_tok):
    ps = jnp.pad(pad_start.astype(F32), (0, LANE - MOE_EXPERTS)).reshape(LANE, 1)
    cols = next(r for r in (2048, 1024, 512, TILE) if n_tok % r == 0)
    return pl.pallas_call(
        _dest_kernel,
        grid=(n_tok // cols,),
        in_specs=[pl.BlockSpec((8, cols), lambda i: (0, i)), pl.BlockSpec((LANE, 1), lambda i: (0, 0))],
        out_specs=pl.BlockSpec((8, cols), lambda i: (0, i)),
        out_shape=jax.ShapeDtypeStruct((8, n_tok), I32),
        compiler_params=_cparams(("arbitrary",)),
        name="pair_dest",
    )(route_t, ps)


INV_UNROLL = 8


def _inverse_kernel(n_pairs, dest_ref, lo_ref, hi_ref, out_ref):
    def body(g, _):
        for u in range(INV_UNROLL):
            p = g * INV_UNROLL + u
            out_ref[dest_ref[p]] = p
        return 0

    lax.fori_loop(0, n_pairs // INV_UNROLL, body, 0)

    def fill_segment(e, _):
        def fill(r, _):
            out_ref[r] = 0
            return 0
        return lax.fori_loop(lo_ref[e], hi_ref[e], fill, 0)

    lax.fori_loop(0, MOE_EXPERTS + 1, fill_segment, 0)


def _inverse_rows(dest_flat, fill_lo, fill_hi, p_rows):
    n_pairs = dest_flat.shape[0]
    smem = pl.BlockSpec(memory_space=pltpu.SMEM)
    return pl.pallas_call(
        functools.partial(_inverse_kernel, n_pairs),
        in_specs=[smem, smem, smem],
        out_specs=smem,
        out_shape=jax.ShapeDtypeStruct((p_rows,), I32),
        name="inverse_rows",
    )(dest_flat, fill_lo, fill_hi)


def _route_plan(route_t, counts, n_tok):
    n_blk = (2 * n_tok) // MOE_TILE + MOE_EXPERTS
    p_rows = n_blk * MOE_TILE
    cnt = counts[MOE_GROUPS:MOE_GROUPS + MOE_EXPERTS, 0].astype(I32)
    padded = (cnt + MOE_TILE - 1) // MOE_TILE * MOE_TILE
    pad_end = jnp.cumsum(padded)
    pad_start = pad_end - padded
    dest = _pair_dest(route_t, pad_start, n_tok)[0:2].reshape(-1)
    fill_lo = jnp.concatenate([pad_start + cnt, pad_end[-1:]]).astype(I32)
    fill_hi = jnp.concatenate([pad_end, jnp.full((1,), p_rows, I32)]).astype(I32)
    row_dst = _inverse_rows(dest, fill_lo, fill_hi, p_rows)
    row_src = jnp.where(row_dst >= n_tok, row_dst - n_tok, row_dst)
    blk_start = (jnp.arange(n_blk + 1, dtype=I32) * MOE_TILE)[:, None]
    blk_expert = jnp.minimum(jnp.sum(pad_end[None, :] <= blk_start, axis=-1), MOE_EXPERTS - 1).astype(I32)
    inside = (pad_start[None, :] <= blk_start) & (blk_start < pad_end[None, :])
    blk_valid = jnp.sum(jnp.where(inside, jnp.clip((pad_start + cnt)[None, :] - blk_start, 0, MOE_TILE), 0), axis=-1).astype(I32)
    return (row_src.reshape(n_blk, 1, MOE_TILE), row_dst.reshape(n_blk, 1, MOE_TILE), blk_expert[:n_blk], blk_valid)


def _final_kernel(x_ref, y_ref, rg_ref, mod_ref, o_ref):
    g2 = mod_ref[0, :, 5 * D_MODEL:6 * D_MODEL]
    rg = rg_ref[...]
    o_ref[...] = x_ref[...] + g2 * (rg[:, 4:5] * _load_row_tiles(y_ref, (0,), TILE) + rg[:, 5:6] * _load_row_tiles(y_ref, (1,), TILE))


def _final(x, y_flat, rg, mod_l, n_tok):
    nq = SEQ // TILE
    tile_map = lambda i: (i, 0)
    return pl.pallas_call(
        _final_kernel,
        grid=(n_tok // TILE,),
        in_specs=[pl.BlockSpec((TILE, D_MODEL), tile_map), pl.BlockSpec((2, TILE * ROW_TILES, LANE), lambda i: (0, i, 0)),
                  pl.BlockSpec((TILE, LANE), tile_map), pl.BlockSpec((1, 1, 6 * D_MODEL), lambda i: (i // nq, 0, 0))],
        out_specs=pl.BlockSpec((TILE, D_MODEL), tile_map),
        out_shape=jax.ShapeDtypeStruct((n_tok, D_MODEL), F32),
        compiler_params=_cparams(("arbitrary",)),
        name="final_residual",
    )(x, y_flat, rg, mod_l)


def _diff_rows(w, heads):
    nl, _, k = w.shape
    w = w.reshape(nl, heads, 2, 2, 2, 12, k)
    w = w.transpose(0, 1, 4, 2, 3, 5, 6)
    w = jnp.pad(w.reshape(nl, heads, 4, 24, k), ((0, 0), (0, 0), (0, 0), (0, 8), (0, 0)))
    return w.reshape(nl, heads * LANE, k)


def _gqa_rows(w, groups):
    nl, _, k = w.shape
    w = w.reshape(nl, 2, groups, 2, 2, 16, k)
    w = w.transpose(0, 2, 4, 1, 3, 5, 6)
    return w.reshape(nl, groups * LANE, k)


def _in_weight(w_in):
    wt = jnp.swapaxes(w_in, 1, 2)
    nl = wt.shape[0]
    dv = jnp.pad(wt[:, 768:1152].reshape(nl, DIFF_HEADS, DIFF_V_DIM, D_MODEL), ((0, 0), (0, 0), (0, LANE - DIFF_V_DIM), (0, 0)))
    parts = [_diff_rows(wt[:, 0:384], DIFF_HEADS), _diff_rows(wt[:, 384:768], DIFF_HEADS),
             dv.reshape(nl, DIFF_HEADS * LANE, D_MODEL), _gqa_rows(wt[:, 1152:1536], 3), _gqa_rows(wt[:, 1536:1664], 1),
             wt[:, 1664:2816], jnp.pad(wt[:, 2816:2832], ((0, 0), (0, LANE - 16), (0, 0)))]
    return jnp.concatenate(parts, axis=1).astype(BF16)


def _lane_gain(g, rows_fn):
    return rows_fn(jnp.concatenate([g, g], axis=-1)[:, :, None], 1)[:, :, 0]


def _out_weight(w_out):
    nl = w_out.shape[0]
    diff = jnp.pad(w_out[:, 0:384].reshape(nl, DIFF_HEADS, DIFF_V_DIM, D_MODEL), ((0, 0), (0, 0), (0, LANE - DIFF_V_DIM), (0, 0)))
    gqa = w_out[:, 384:768].reshape(nl, 2, 3, GQA_HEAD_DIM, D_MODEL).transpose(0, 2, 1, 3, 4)
    return jnp.concatenate([diff.reshape(nl, 512, D_MODEL), gqa.reshape(nl, 384, D_MODEL), w_out[:, 768:]], axis=1).astype(BF16)


def kernel(x, c, ctx, c_ctx, norm1_g, norm2_g, w_mod, b_mod, w_in, w_out, diff_q_norm, diff_k_norm, diff_lambda, diff_subln, gqa_q_norm, gqa_k_norm, mlstm_conv_w, mlstm_conv_b, mlstm_gate_b, mlstm_head_norm, moe_wg, moe_bg, moe_we, moe_be, moe_w1, moe_w3, moe_w2):
    B = x.shape[0]
    nt = B * SEG_ROWS
    n_lat = B * SEQ

    w_in_r = _in_weight(w_in)
    w_out_r = _out_weight(w_out)
    gq_d = _lane_gain(diff_q_norm, _diff_rows) * (LOG2E * DIFF_HEAD_DIM ** -0.5)
    gk_d = _lane_gain(diff_k_norm, _diff_rows)
    gq_g = _lane_gain(gqa_q_norm, _gqa_rows) * (LOG2E * GQA_HEAD_DIM ** -0.5)
    gk_g = _lane_gain(gqa_k_norm, _gqa_rows)
    bound_d = _score_bound(diff_q_norm, diff_k_norm, DIFF_HEAD_DIM).reshape(DEPTH, 1)
    bound_g = _score_bound(gqa_q_norm, gqa_k_norm, GQA_HEAD_DIM).reshape(DEPTH, 1)
    gains = jnp.concatenate([jnp.tile(gq_d, (1, 4)), jnp.tile(gk_d, (1, 4)), jnp.tile(gq_g, (1, 3)), gk_g], axis=1)
    gains = gains.reshape(DEPTH, 1, 1536)
    gate_b = jnp.pad(mlstm_gate_b, ((0, 0), (0, LANE - 16))).reshape(DEPTH, 1, LANE)
    subln = jnp.pad(diff_subln, ((0, 0), (0, LANE - DIFF_V_DIM))).reshape(DEPTH, 1, LANE)
    lam_pad = jnp.pad(diff_lambda, ((0, 0), (0, 4), (0, LANE - DIFF_HEAD_DIM)))
    conv_w = jnp.pad(mlstm_conv_w, ((0, 0), (0, 5), (0, 0)))
    conv_b = mlstm_conv_b.reshape(DEPTH, 1, 512)
    w_r = jnp.swapaxes(jnp.pad(jnp.concatenate([moe_wg, moe_we], axis=2), ((0, 0), (0, 0), (0, LANE - 36))), 1, 2)
    wr_hi = w_r.astype(BF16)
    wr_lo = (w_r - wr_hi.astype(F32)).astype(BF16)
    rb = jnp.pad(jnp.concatenate([moe_bg, moe_be], axis=1), ((0, 0), (0, LANE - 36))).reshape(DEPTH, LANE, 1)
    pair_mat = jnp.asarray(_PAIR_MAT, BF16)
    head_mat = jnp.asarray((np.arange(256)[:, None] // 64 == np.arange(256)[None, :] // 64).astype(np.float32), BF16)
    tabs = _rope_tables(12, 24) + _rope_tables(16, 32)

    xa = jnp.concatenate([ctx, x], axis=1).reshape(nt, D_MODEL)
    assert B <= 8, "row 8 of the modulation table is reserved for the context conditioning"
    cvec = jnp.concatenate([c, jnp.zeros((8 - B, D_MODEL), F32), c_ctx[None, :], jnp.zeros((7, D_MODEL), F32)], axis=0)
    mod = _modulation(cvec, w_mod, b_mod).reshape(DEPTH, 16, 1, 6 * D_MODEL)

    moe_in = None
    for l in range(DEPTH):
        last = l == DEPTH - 1
        lam_init = 0.8 - 0.6 * math.exp(-0.3 * l)
        xa, (dq, dk, dv, gq, gk, gv, mqk, mv, mo, mg) = _inproj(
            xa, moe_in, mod[l], norm1_g[l].reshape(1, D_MODEL), w_in_r[l], gains[l], gate_b[l], tabs, pair_mat, nt)
        do = _diff_attention(bound_d[l], dq, dk, dv, lam_pad[l], subln[l], lam_init, nt, with_ctx=not last)
        go = _gqa_attention(bound_g[l], gq, gk, gv, nt, with_ctx=not last)
        mgt = mg[:, :16].reshape(B, N_CHUNKS, MLSTM_CHUNK, 16).transpose(0, 1, 3, 2)
        hs = _mlstm(mqk, mv, mg, mgt, conv_w[l], conv_b[l], nt)
        xm, f, route, route_t, counts = _outproj(do, go, hs, mo, xa, mod[l], norm2_g[l].reshape(1, D_MODEL),
                                         mlstm_head_norm[l].reshape(1, 256), w_out_r[l], wr_hi[l], wr_lo[l], rb[l],
                                         head_mat, nt, with_ctx=not last)
        n_tok = n_lat if last else nt
        row_src, row_dst, blk_expert, blk_valid = _route_plan(route_t, counts, n_tok)
        y = _experts(f, row_src, row_dst, blk_expert, blk_valid, moe_w1, moe_w3, moe_w2, l, n_tok)
        if last:
            out = _final(xm, y, route, mod[l][:B], n_tok)
            return out.reshape(B, SEQ, D_MODEL)
        xa = xm
        moe_in = (y, route, mod[l])
```
